```python
import math
import jax
import jax.numpy as jnp
from jax import lax
import numpy as np

D_MODEL = 1024
BATCH = 2
SEQ = 8192
DEPTH = 4
DEC_BATCH = 128
DEC_SEQ = 1
PAST_LEN = 2048
PAGE_SIZE = 128

HEAD_DIM = 64
GROUP_WIDTH = D_MODEL // 4
MIX_WIDTH = 4 * GROUP_WIDTH
NSA_HEADS = GROUP_WIDTH // HEAD_DIM
NSA_KV_HEADS = 1
NSA_GROUP = NSA_HEADS // NSA_KV_HEADS
NSA_KV_W = NSA_KV_HEADS * HEAD_DIM
CMP_BLOCK = 32
SEL_BLOCK = 64
SEL_TOPK = 16
WINDOW = 512
FORCE_SCORE = 1.0e4
SB_HEADS = GROUP_WIDTH // HEAD_DIM
DIFF_HEADS = GROUP_WIDTH // HEAD_DIM
DIFF_QK = HEAD_DIM // 2
MLP_GROUPS = 4
MLP_GROUP_DIM = GROUP_WIDTH // MLP_GROUPS
CHUNK = 128
Q_BLOCK = 128
ROPE_THETA = 500000.0
ROT_DIM = HEAD_DIM // 4
DIFF_ROT_DIM = DIFF_QK // 4
FFN_HIDDEN = -(-(8 * D_MODEL) // (3 * 256)) * 256
DN_ALPHA = (2 * DEPTH) ** 0.25
DN_BETA = (8 * DEPTH) ** -0.25
LN_EPS = 1e-5
NEG_INF = -1e30
TINY = 1e-30
SPLIT_SIZES = (GROUP_WIDTH,
               NSA_KV_W, NSA_KV_W, NSA_KV_W, NSA_KV_W, NSA_KV_W, NSA_KV_W,
               NSA_HEADS * 3,
               GROUP_WIDTH, GROUP_WIDTH, GROUP_WIDTH,
               GROUP_WIDTH, GROUP_WIDTH, GROUP_WIDTH,
               GROUP_WIDTH, GROUP_WIDTH)
D_IN = sum(SPLIT_SIZES)

kernel_name = 'nsa_sb_diff_gmlp_hybrid_step'


def layer_norm(x, g, b):
    xf = x.astype(jnp.float32)
    mu = xf.mean(-1, keepdims=True)
    var = jnp.square(xf - mu).mean(-1, keepdims=True)
    return ((xf - mu) * lax.rsqrt(var + LN_EPS) * g + b).astype(x.dtype)


def rms_norm(x, g):
    xf = x.astype(jnp.float32)
    return xf * lax.rsqrt(jnp.mean(xf * xf, -1, keepdims=True) + LN_EPS) * g


def masked_softmax(logits, mask):
    l = jnp.where(mask, logits, NEG_INF)
    m = jnp.max(l, axis=-1, keepdims=True)
    e = jnp.where(mask, jnp.exp(l - m), 0.0)
    return e / jnp.maximum(e.sum(-1, keepdims=True), TINY)


def rope(x, pos, rot_dim):
    half = rot_dim // 2
    freq = ROPE_THETA ** (-jnp.arange(half, dtype=jnp.float32) / half)
    ang = pos.astype(jnp.float32)[:, None] * freq[None, :]
    shape = (1, x.shape[1]) + (1,) * (x.ndim - 3) + (half,)
    cos, sin = jnp.cos(ang).reshape(shape), jnp.sin(ang).reshape(shape)
    xr = x[..., :rot_dim].astype(jnp.float32)
    x1, x2 = xr[..., :half], xr[..., half:]
    rot = jnp.concatenate([x1 * cos - x2 * sin, x2 * cos + x1 * sin], -1).astype(x.dtype)
    return jnp.concatenate([rot, x[..., rot_dim:]], -1)


def project(x, w_in, gate_b, pos):
    B, L, _ = x.shape
    h = jnp.einsum('bld,de->ble', x, w_in)
    qa, kc, vc, ks, vs, kw, vw, ga, qb, kb, vb, qd, kd, vd, u, v = jnp.split(
        h, [int(i) for i in np.cumsum(SPLIT_SIZES)[:-1]], axis=-1)
    shp = lambda a, *s: a.reshape((B, L) + s)
    kvh = lambda a: shp(a, NSA_KV_HEADS, HEAD_DIM)
    return {
        'qa': rope(shp(qa, NSA_KV_HEADS, NSA_GROUP, HEAD_DIM), pos, ROT_DIM),
        'ga': jax.nn.sigmoid(shp(ga + gate_b, NSA_KV_HEADS, NSA_GROUP, 3)),
        'kc': kvh(kc), 'vc': kvh(vc),
        'ks': rope(kvh(ks), pos, ROT_DIM), 'vs': kvh(vs),
        'kw': rope(kvh(kw), pos, ROT_DIM), 'vw': kvh(vw),
        'qb': shp(qb, SB_HEADS, HEAD_DIM), 'kb': shp(kb, SB_HEADS, HEAD_DIM), 'vb': shp(vb, SB_HEADS, HEAD_DIM),
        'qd': rope(shp(qd, DIFF_HEADS, 2, DIFF_QK), pos, DIFF_ROT_DIM),
        'kd': rope(shp(kd, DIFF_HEADS, 2, DIFF_QK), pos, DIFF_ROT_DIM),
        'vd': shp(vd, DIFF_HEADS, HEAD_DIM),
        'u': jax.nn.gelu(u), 'v': jax.nn.gelu(v)}


def compress(rows, pe, w1, w2):
    B, Lp, Hk, D = rows.shape
    blk = rows.reshape(B, Lp // CMP_BLOCK, CMP_BLOCK, Hk, D) + pe[None, None, :, None, :]
    flat = jnp.moveaxis(blk, 3, 2).reshape(B, Lp // CMP_BLOCK, Hk, CMP_BLOCK * D)
    return jnp.einsum('bchf,fe->bche', jax.nn.gelu(jnp.einsum('bchf,fe->bche', flat, w1)), w2)


def nsa_context(kc, vc, ks, vs, pe, w1, w2):
    B, L, Hk, D = kc.shape
    Lp = -(-L // SEL_BLOCK) * SEL_BLOCK
    pad = lambda a: jnp.pad(a, ((0, 0), (0, Lp - L), (0, 0), (0, 0)))
    blocks = lambda a: jnp.moveaxis(pad(a).reshape(B, Lp // SEL_BLOCK, SEL_BLOCK, Hk, D), 3, 1)
    return (compress(pad(kc), pe[0], w1[0], w2[0]), compress(pad(vc), pe[1], w1[1], w2[1]),
            blocks(ks), blocks(vs))


def nsa_attend(q, g, q_pos, kcmp, vcmp, ks_b, vs_b, kw, vw, kw_pos):
    B, Q, Hk, G, D = q.shape
    NC, NS = kcmp.shape[1], ks_b.shape[2]
    scale = D ** -0.5
    qf = q.astype(jnp.float32)
    lc = jnp.einsum('bqhgd,bchd->bhgqc', qf, kcmp.astype(jnp.float32)) * scale
    c_end = jnp.arange(NC) * CMP_BLOCK + (CMP_BLOCK - 1)
    pc = masked_softmax(lc, c_end[None, :] <= q_pos[:, None])
    oc = jnp.einsum('bhgqc,bchd->bqhgd', pc, vcmp.astype(jnp.float32))
    imp = pc.sum(axis=2).reshape(B, Hk, Q, NS, SEL_BLOCK // CMP_BLOCK).sum(-1)
    blk = jnp.arange(NS)[None, :]
    cur = (q_pos // SEL_BLOCK)[:, None]
    forced = (blk == 0) | (blk == cur) | (blk == cur - 1)
    valid = blk * SEL_BLOCK <= q_pos[:, None]
    score = jnp.where(forced, FORCE_SCORE, jnp.where(valid, imp, -1.0))
    _, idx = lax.top_k(score, min(SEL_TOPK, NS))
    bi = jnp.arange(B)[:, None, None, None]
    hi = jnp.arange(Hk)[None, :, None, None]
    ksel, vsel = ks_b[bi, hi, idx], vs_b[bi, hi, idx]
    kpos = idx[..., None] * SEL_BLOCK + jnp.arange(SEL_BLOCK)
    ls = jnp.einsum('bqhgd,bhqkjd->bhgqkj', qf, ksel.astype(jnp.float32)) * scale
    nk = idx.shape[-1] * SEL_BLOCK
    ms = (kpos <= q_pos[:, None, None])[:, :, None].reshape(B, Hk, 1, Q, nk)
    ps = masked_softmax(ls.reshape(B, Hk, G, Q, nk), ms).reshape(ls.shape)
    osel = jnp.einsum('bhgqkj,bhqkjd->bqhgd', ps, vsel.astype(jnp.float32))
    lw = jnp.einsum('bqhgd,blhd->bhgql', qf, kw.astype(jnp.float32)) * scale
    dpos = q_pos[:, None] - kw_pos[None, :]
    mw = (dpos >= 0) & (dpos < WINDOW) & (kw_pos[None, :] >= 0)
    ow = jnp.einsum('bhgql,blhd->bqhgd', masked_softmax(lw, mw), vw.astype(jnp.float32))
    g = g.astype(jnp.float32)
    return g[..., 0:1] * oc + g[..., 1:2] * osel + g[..., 2:3] * ow


def stick_breaking_attend(q, q_pos, k, v):
    L = k.shape[1]
    z = jnp.einsum('bqhd,blhd->bhql', q.astype(jnp.float32), k.astype(jnp.float32)) * (q.shape[-1] ** -0.5)
    mask = jnp.arange(L)[None, :] < q_pos[:, None]
    log_keep = jnp.where(mask, jax.nn.log_sigmoid(-z), 0.0)
    after = jnp.pad(lax.cumsum(log_keep, axis=3, reverse=True)[..., 1:], ((0, 0), (0, 0), (0, 0), (0, 1)))
    a = jnp.where(mask, jnp.exp(jax.nn.log_sigmoid(z) + after), 0.0)
    return jnp.einsum('bhql,blhd->bqhd', a, v.astype(jnp.float32))


def diff_attend(q, q_pos, k, v, lam):
    L = k.shape[1]
    s = jnp.einsum('bqhmd,blhmd->mbhql', q.astype(jnp.float32), k.astype(jnp.float32)) * (q.shape[-1] ** -0.5)
    p = masked_softmax(s, jnp.arange(L)[None, :] <= q_pos[:, None])
    return jnp.einsum('bhql,blhd->bqhd', p[0] - lam * p[1], v.astype(jnp.float32))


def mix_block(qa, ga, qb, qd, q_pos, kcmp, vcmp, ks_b, vs_b, kw, vw, kw_pos, kb, vb, kd, vd, lam):
    o_a = nsa_attend(qa, ga, q_pos, kcmp, vcmp, ks_b, vs_b, kw, vw, kw_pos)
    o_b = stick_breaking_attend(qb, q_pos, kb, vb)
    o_c = diff_attend(qd, q_pos, kd, vd, lam)
    return o_a, o_b, o_c


def prompt_sweep(z, ctx, lam):
    kcmp, vcmp, ks_b, vs_b = ctx
    B, S = z['qa'].shape[:2]
    pad_w = lambda a: jnp.pad(a, ((0, 0), (WINDOW, 0), (0, 0), (0, 0)))
    kw_pad, vw_pad = pad_w(z['kw']), pad_w(z['vw'])

    def body(i):
        s0 = i * Q_BLOCK
        qs = lambda a: lax.dynamic_slice_in_dim(a, s0, Q_BLOCK, axis=1)
        ws = lambda a: lax.dynamic_slice_in_dim(a, s0, WINDOW + Q_BLOCK, axis=1)
        q_pos = s0 + jnp.arange(Q_BLOCK, dtype=jnp.int32)
        kw_pos = s0 - WINDOW + jnp.arange(WINDOW + Q_BLOCK, dtype=jnp.int32)
        return mix_block(qs(z['qa']), qs(z['ga']), qs(z['qb']), qs(z['qd']), q_pos,
                         kcmp, vcmp, ks_b, vs_b, ws(kw_pad), ws(vw_pad), kw_pos,
                         z['kb'], z['vb'], z['kd'], z['vd'], lam)

    outs = lax.map(body, jnp.arange(S // Q_BLOCK, dtype=jnp.int32))
    return [jnp.moveaxis(o, 0, 1).reshape((B, S) + o.shape[3:]) for o in outs]


def gmlp_spatial(u, v, ln_g, ln_b, ws, bs):
    vn = layer_norm(v, ln_g, ln_b)
    B, L, _ = v.shape
    Lp = -(-L // CHUNK) * CHUNK
    vc = jnp.pad(vn, ((0, 0), (0, Lp - L), (0, 0))).reshape(B, Lp // CHUNK, CHUNK, MLP_GROUPS, MLP_GROUP_DIM)
    mixed = jnp.einsum('gts,bcsgd->bctgd', jnp.tril(ws), vc) + bs.T[None, None, :, :, None]
    return u * mixed.reshape(B, Lp, -1)[:, :L], vn


def layer_tail(x, o_a, o_b, o_c, o_d, lam_init, norm_g, w_o, g1, b1, wg, wu, wd, g2, b2):
    B, L, _ = x.shape
    o_c = rms_norm(o_c, norm_g) * (1.0 - lam_init)
    mixed = jnp.concatenate([o_a.reshape(B, L, -1).astype(x.dtype), o_b.reshape(B, L, -1).astype(x.dtype),
                             o_c.reshape(B, L, -1).astype(x.dtype), o_d.astype(x.dtype)], axis=-1)
    x = layer_norm(DN_ALPHA * x + mixed @ w_o, g1, b1)
    h = jax.nn.silu(x @ wg) * (x @ wu)
    return layer_norm(DN_ALPHA * x + h @ wd, g2, b2)


def gather_pages(cache, l, page_table):
    g = cache[l, page_table]
    return g.reshape((g.shape[0], g.shape[1] * g.shape[2]) + g.shape[3:])


def cat_rows(past, new):
    return jnp.concatenate([past.astype(new.dtype), new], axis=1)


def kv_rows(k, v):
    return jnp.stack([k, v], axis=2)


def setup_inputs(seed: int = 0) -> dict:
    key = jax.random.key(seed)
    k = jax.random.split(key, 28)
    nrm = lambda kk, shape, scale: scale * jax.random.normal(kk, shape, jnp.float32)
    n_pages = PAST_LEN // PAGE_SIZE
    n_used = DEC_BATCH * n_pages
    n_pool = n_used + max(n_used // 4, 1)
    w_past = min(WINDOW, PAST_LEN)
    page_table = jax.random.permutation(k[0], n_pool)[:n_used].reshape(DEC_BATCH, n_pages).astype(jnp.int32)
    return {
        'x_prompt': nrm(k[1], (BATCH, SEQ, D_MODEL), 1.0),
        'x_sample': nrm(k[2], (DEC_BATCH, DEC_SEQ, D_MODEL), 1.0),
        'cache_nsa_cmp': nrm(k[3], (DEPTH, n_pool, PAGE_SIZE, 2, NSA_KV_HEADS, HEAD_DIM), 1.0),
        'cache_nsa_sel': nrm(k[4], (DEPTH, n_pool, PAGE_SIZE, 2, NSA_KV_HEADS, HEAD_DIM), 1.0),
        'cache_nsa_win': nrm(k[5], (DEPTH, DEC_BATCH, w_past, 2, NSA_KV_HEADS, HEAD_DIM), 1.0),
        'cache_sb': nrm(k[6], (DEPTH, n_pool, PAGE_SIZE, 2, SB_HEADS, HEAD_DIM), 1.0),
        'cache_diff': nrm(k[7], (DEPTH, n_pool, PAGE_SIZE, 2, DIFF_HEADS, HEAD_DIM), 1.0),
        'page_table': page_table,
        'w_in': nrm(k[8], (DEPTH, D_MODEL, D_IN), D_MODEL ** -0.5),
        'gate_b': nrm(k[9], (DEPTH, NSA_HEADS * 3), 0.1),
        'cmp_pe': nrm(k[10], (DEPTH, 2, CMP_BLOCK, HEAD_DIM), 0.1),
        'cmp_w1': nrm(k[11], (DEPTH, 2, CMP_BLOCK * HEAD_DIM, HEAD_DIM), (CMP_BLOCK * HEAD_DIM) ** -0.5),
        'cmp_w2': nrm(k[12], (DEPTH, 2, HEAD_DIM, HEAD_DIM), HEAD_DIM ** -0.5),
        'diff_lam': nrm(k[13], (DEPTH, 4, DIFF_QK), 0.1),
        'diff_norm_g': 1.0 + nrm(k[14], (DEPTH, HEAD_DIM), 0.01),
        'gmlp_ln_g': 1.0 + nrm(k[15], (DEPTH, GROUP_WIDTH), 0.01),
        'gmlp_ln_b': nrm(k[16], (DEPTH, GROUP_WIDTH), 0.01),
        'gmlp_ws': nrm(k[17], (DEPTH, MLP_GROUPS, CHUNK, CHUNK), CHUNK ** -0.5),
        'gmlp_bs': 1.0 + nrm(k[18], (DEPTH, MLP_GROUPS, CHUNK), 0.01),
        'w_out': nrm(k[19], (DEPTH, MIX_WIDTH, D_MODEL), DN_BETA * MIX_WIDTH ** -0.5),
        'ln1_g': 1.0 + nrm(k[20], (DEPTH, D_MODEL), 0.01),
        'ln1_b': nrm(k[21], (DEPTH, D_MODEL), 0.01),
        'w_gate': nrm(k[22], (DEPTH, D_MODEL, FFN_HIDDEN), D_MODEL ** -0.5),
        'w_up': nrm(k[23], (DEPTH, D_MODEL, FFN_HIDDEN), D_MODEL ** -0.5),
        'w_down': nrm(k[24], (DEPTH, FFN_HIDDEN, D_MODEL), DN_BETA * FFN_HIDDEN ** -0.5),
        'ln2_g': 1.0 + nrm(k[25], (DEPTH, D_MODEL), 0.01),
        'ln2_b': nrm(k[26], (DEPTH, D_MODEL), 0.01),
    }


def reference(x_prompt, x_sample, cache_nsa_cmp, cache_nsa_sel, cache_nsa_win, cache_sb, cache_diff,
              page_table, w_in, gate_b, cmp_pe, cmp_w1, cmp_w2, diff_lam, diff_norm_g, gmlp_ln_g,
              gmlp_ln_b, gmlp_ws, gmlp_bs, w_out, ln1_g, ln1_b, w_gate, w_up, w_down, ln2_g, ln2_b):
    xp, xs = x_prompt, x_sample
    S, T = xp.shape[1], xs.shape[1]
    pos_p = jnp.arange(S, dtype=jnp.int32)
    pos_s = PAST_LEN + jnp.arange(T, dtype=jnp.int32)
    w_past = cache_nsa_win.shape[2]
    kw_pos_s = jnp.concatenate([PAST_LEN - w_past + jnp.arange(w_past, dtype=jnp.int32), pos_s])
    cmp_p, cmp_s, sel_p, sel_s, win_p, win_s = [], [], [], [], [], []
    sb_p, sb_s, diff_p, diff_s, gmlp_s = [], [], [], [], []
    for l in range(DEPTH):
        lam_init = 0.8 - 0.6 * math.exp(-0.3 * l)
        lq = diff_lam[l].astype(jnp.float32)
        lam = jnp.exp(jnp.sum(lq[0] * lq[1])) - jnp.exp(jnp.sum(lq[2] * lq[3])) + lam_init
        tail_w = (diff_norm_g[l], w_out[l], ln1_g[l], ln1_b[l], w_gate[l], w_up[l], w_down[l], ln2_g[l], ln2_b[l])

        zp = project(xp, w_in[l], gate_b[l], pos_p)
        ctx_p = nsa_context(zp['kc'], zp['vc'], zp['ks'], zp['vs'], cmp_pe[l], cmp_w1[l], cmp_w2[l])
        o_a, o_b, o_c = prompt_sweep(zp, ctx_p, lam)
        o_d, _ = gmlp_spatial(zp['u'], zp['v'], gmlp_ln_g[l], gmlp_ln_b[l], gmlp_ws[l], gmlp_bs[l])
        cmp_p.append(kv_rows(zp['kc'], zp['vc']))
        sel_p.append(kv_rows(zp['ks'], zp['vs']))
        win_p.append(kv_rows(zp['kw'], zp['vw'])[:, S - min(WINDOW, S):])
        sb_p.append(kv_rows(zp['kb'], zp['vb']))
        diff_p.append(kv_rows(zp['kd'].reshape(zp['vd'].shape), zp['vd']))
        xp = layer_tail(xp, o_a, o_b, o_c, o_d, lam_init, *tail_w)

        zs = project(xs, w_in[l], gate_b[l], pos_s)
        p_cmp = gather_pages(cache_nsa_cmp, l, page_table)
        p_sel = gather_pages(cache_nsa_sel, l, page_table)
        p_sb = gather_pages(cache_sb, l, page_table)
        p_diff = gather_pages(cache_diff, l, page_table)
        win = cache_nsa_win[l]
        ctx_s = nsa_context(cat_rows(p_cmp[:, :, 0], zs['kc']), cat_rows(p_cmp[:, :, 1], zs['vc']),
                            cat_rows(p_sel[:, :, 0], zs['ks']), cat_rows(p_sel[:, :, 1], zs['vs']),
                            cmp_pe[l], cmp_w1[l], cmp_w2[l])
        kw_s = cat_rows(win[:, :, 0], zs['kw'])
        vw_s = cat_rows(win[:, :, 1], zs['vw'])
        kd_past = p_diff[:, :, 0].reshape(p_diff.shape[:2] + (DIFF_HEADS, 2, DIFF_QK))
        o_a, o_b, o_c = mix_block(zs['qa'], zs['ga'], zs['qb'], zs['qd'], pos_s, *ctx_s, kw_s, vw_s, kw_pos_s,
                                  cat_rows(p_sb[:, :, 0], zs['kb']), cat_rows(p_sb[:, :, 1], zs['vb']),
                                  cat_rows(kd_past, zs['kd']), cat_rows(p_diff[:, :, 1], zs['vd']), lam)
        o_d, vn_s = gmlp_spatial(zs['u'], zs['v'], gmlp_ln_g[l], gmlp_ln_b[l], gmlp_ws[l], gmlp_bs[l])
        cmp_s.append(kv_rows(zs['kc'], zs['vc']))
        sel_s.append(kv_rows(zs['ks'], zs['vs']))
        win_s.append(kv_rows(kw_s, vw_s)[:, T:])
        sb_s.append(kv_rows(zs['kb'], zs['vb']))
        diff_s.append(kv_rows(zs['kd'].reshape(zs['vd'].shape), zs['vd']))
        gmlp_s.append(vn_s)
        xs = layer_tail(xs, o_a, o_b, o_c, o_d, lam_init, *tail_w)

    return (xp, xs, jnp.stack(cmp_p), jnp.stack(cmp_s), jnp.stack(sel_p), jnp.stack(sel_s),
            jnp.stack(win_p), jnp.stack(win_s), jnp.stack(sb_p), jnp.stack(sb_s),
            jnp.stack(diff_p), jnp.stack(diff_s), jnp.stack(gmlp_s))
```

```python
import functools
import math

import jax
import jax.numpy as jnp
from jax import lax
from jax.experimental import pallas as pl
from jax.experimental.pallas import tpu as pltpu

D_MODEL = 1024
HEAD_DIM = 64
N_HEADS = 4
GROUP_WIDTH = N_HEADS * HEAD_DIM
CMP_BLOCK = 32
SEL_BLOCK = 64
SEL_TOPK = 16
WINDOW = 512
FORCE_SCORE = 1.0e4
DIFF_QK = HEAD_DIM // 2
CHUNK = 128
ROPE_THETA = 500000.0
ROT_HALF = HEAD_DIM // 8
DIFF_ROT_HALF = DIFF_QK // 8
LN_EPS = 1e-5
NEG_INF = -1e30
TINY = 1e-30
NEG_BIG = -3.0e38

LANES = 128
VMEM_LIMIT_BYTES = 56 * 1024 * 1024

TOKEN_TILE = 256
Q_TILE = 256
KV_TILE = 256

F32 = jnp.float32
BF16 = jnp.bfloat16

RM = dict(qa=0, qa_sw=512, qb=1024, qd=1280, qd_sw=1536, u=1792, v=2048, gate=2304, cmp=2432,
          sel=2560, sel_sw=2688, win=2816, win_sw=2944, diff=3072, kd_sw=3584)
RM_PROMPT_COLS = 2560
RM_SAMPLE_COLS = 3840
TR = dict(cmp=0, sel=128, sel_sw=256, win=384, win_sw=512, sb=640, diff=1152, kd_sw=1664)
TR_ROWS = 1920


def _dot(a, b):
    return jnp.dot(a, b, preferred_element_type=F32)


def _dot_nt(a, b):
    return lax.dot_general(a, b, (((1,), (1,)), ((), ())), preferred_element_type=F32)


def _iota(shape, dim):
    return lax.broadcasted_iota(jnp.int32, shape, dim)


def _gelu(x):
    return 0.5 * x * (1.0 + jnp.tanh(0.7978845608028654 * (x + 0.044715 * (x * x * x))))


def _sigmoid(x):
    return 1.0 / (1.0 + jnp.exp(-x))


def _layer_norm(x, g, b):
    mu = jnp.mean(x, axis=-1, keepdims=True)
    xc = x - mu
    var = jnp.mean(xc * xc, axis=-1, keepdims=True)
    return xc * lax.rsqrt(var + LN_EPS) * g + b


def _softplus(z):
    return jnp.maximum(z, 0.0) + jnp.log(1.0 + jnp.exp(-jnp.abs(z)))


def _split_bf16(x):
    hi = x.astype(BF16)
    lo = (x - hi.astype(F32)).astype(BF16)
    return hi, lo


def _suffix_matrix(n):
    return (_iota((n, n), 0) >= _iota((n, n), 1)).astype(BF16)


def _params(sem):
    return pltpu.CompilerParams(dimension_semantics=sem, vmem_limit_bytes=VMEM_LIMIT_BYTES)


def _const_spec(shape):
    zeros = (0,) * len(shape)
    return pl.BlockSpec(shape, lambda *_: zeros, pipeline_mode=pl.Buffered(1))


def _proj_common(xb, wrm_ref, cq_ref, sq_ref, cd_ref, sd_ref, gb_ref, lng_ref, lnb_ref,
                 qaf_ref, qb_ref, qd_ref, g_ref):
    def rm(name, k=0, width=128):
        a = RM[name] + k * 128
        return _dot(xb, wrm_ref[:, a:a + width])

    cq, sq = cq_ref[...], sq_ref[...]
    for h in range(N_HEADS):
        q = rm("qa", h) * cq + rm("qa_sw", h) * sq
        qaf_ref[:, h * 128:(h + 1) * 128] = q.astype(BF16)
    qb_ref[...] = rm("qb", 0, 256).astype(BF16)
    cd, sd = cd_ref[...], sd_ref[...]
    for k in range(2):
        q = rm("qd", k) * cd + rm("qd_sw", k) * sd
        qd_ref[:, k * 128:(k + 1) * 128] = q.astype(BF16)
    g_ref[...] = _sigmoid(rm("gate") + gb_ref[...])
    u = _gelu(rm("u", 0, 256))
    vn = _layer_norm(_gelu(rm("v", 0, 256)), lng_ref[...], lnb_ref[...])
    return rm, u, vn


def _proj_feature_major(xb, wt_ref, cta_ref, sta_ref, ctd_ref, std_ref):
    def tr(name, k=0, height=128):
        a = TR[name] + k * 128
        return _dot_nt(wt_ref[a:a + height, :], xb)

    cta, sta = cta_ref[...], sta_ref[...]
    ctd, std = ctd_ref[...], std_ref[...]
    cmp_t = tr("cmp")
    sel_t = tr("sel") * cta + tr("sel_sw") * sta
    win_t = tr("win") * cta + tr("win_sw") * sta
    sb_t = tr("sb", 0, 512)
    kd_t = [tr("diff", k) * ctd + tr("kd_sw", k) * std for k in range(2)]
    vd_t = tr("diff", 2, 256)
    return cmp_t, sel_t, win_t, sb_t, kd_t, vd_t


def _proj_prompt_kernel(x_ref, wrm_ref, wt_ref, cq_ref, sq_ref, cd_ref, sd_ref,
                        cta_ref, sta_ref, ctd_ref, std_ref, gb_ref, lng_ref, lnb_ref, ws_ref, bsm_ref,
                        qaf_ref, qb_ref, qd_ref, g_ref, od_ref, cmprm_ref,
                        cmpt_ref, selt_ref, wint_ref, sbt_ref, difft_ref,
                        selbf_ref, winbf_ref, sbbf_ref, diffbf_ref, *, tm, tk):
    xb = x_ref[...].astype(BF16)
    rm, u, vn = _proj_common(xb, wrm_ref, cq_ref, sq_ref, cd_ref, sd_ref, gb_ref, lng_ref, lnb_ref,
                             qaf_ref, qb_ref, qd_ref, g_ref)
    cmprm_ref[...] = rm("cmp")

    tril = _iota((CHUNK, CHUNK), 1) <= _iota((CHUNK, CHUNK), 0)
    lane_group = _iota((CHUNK, GROUP_WIDTH), 1) >> 6
    wsg = [jnp.where(tril, ws_ref[g], 0.0).astype(BF16) for g in range(N_HEADS)]
    bsm = bsm_ref[...]
    for c in range(tm // CHUNK):
        rows = slice(c * CHUNK, (c + 1) * CHUNK)
        vnc = vn[rows, :]
        mixed = bsm
        for g in range(N_HEADS):
            mixed = mixed + _dot(wsg[g], jnp.where(lane_group == g, vnc, 0.0).astype(BF16))
        od_ref[rows, :] = (u[rows, :] * mixed).astype(BF16)

    cmp_t, sel_t, win_t, sb_t, kd_t, vd_t = _proj_feature_major(
        xb, wt_ref, cta_ref, sta_ref, ctd_ref, std_ref)
    cmpt_ref[...] = cmp_t
    selt_ref[...] = sel_t
    wint_ref[...] = win_t
    sbt_ref[...] = sb_t
    difft_ref[0:128, :] = kd_t[0]
    difft_ref[128:256, :] = kd_t[1]
    difft_ref[256:512, :] = vd_t
    for t in range(tm // tk):
        cols = slice(t * tk, (t + 1) * tk)
        selbf_ref[t] = sel_t[:, cols].astype(BF16)
        winbf_ref[t] = win_t[:, cols].astype(BF16)
        sbbf_ref[t] = sb_t[:, cols].astype(BF16)
        diffbf_ref[t, 0:128, :] = kd_t[0][:, cols].astype(BF16)
        diffbf_ref[t, 128:256, :] = kd_t[1][:, cols].astype(BF16)
        diffbf_ref[t, 256:512, :] = vd_t[:, cols].astype(BF16)


def _proj_sample_kernel(x_ref, wrm_ref, wt_ref, cq_ref, sq_ref, cd_ref, sd_ref,
                        cta_ref, sta_ref, ctd_ref, std_ref, gb_ref, lng_ref, lnb_ref, w00_ref, b0_ref,
                        qaf_ref, qb_ref, qd_ref, g_ref, od_ref, vn_ref,
                        selrm_ref, winrm_ref, diffrm_ref,
                        cmpt_ref, selt_ref, wint_ref, sbt_ref, difft_ref):
    xb = x_ref[...].astype(BF16)
    rm, u, vn = _proj_common(xb, wrm_ref, cq_ref, sq_ref, cd_ref, sd_ref, gb_ref, lng_ref, lnb_ref,
                             qaf_ref, qb_ref, qd_ref, g_ref)
    vn_ref[...] = vn
    od_ref[...] = (u * (vn * w00_ref[...] + b0_ref[...])).astype(BF16)

    cq, sq = cq_ref[...], sq_ref[...]
    cd, sd = cd_ref[...], sd_ref[...]
    selrm_ref[...] = rm("sel") * cq + rm("sel_sw") * sq
    winrm_ref[...] = rm("win") * cq + rm("win_sw") * sq
    for k in range(2):
        diffrm_ref[:, k * 128:(k + 1) * 128] = rm("diff", k) * cd + rm("kd_sw", k) * sd
    diffrm_ref[:, 256:512] = rm("diff", 2, 256)

    cmp_t, sel_t, win_t, sb_t, kd_t, vd_t = _proj_feature_major(
        xb, wt_ref, cta_ref, sta_ref, ctd_ref, std_ref)
    cmpt_ref[...] = cmp_t
    selt_ref[...] = sel_t
    wint_ref[...] = win_t
    sbt_ref[...] = sb_t
    difft_ref[0:128, :] = kd_t[0]
    difft_ref[128:256, :] = kd_t[1]
    difft_ref[256:512, :] = vd_t


def _proj_prompt(x, wrm, wt, tabs, gb, lng, lnb, ws, bsm):
    B, S, _ = x.shape
    tm, tk = TOKEN_TILE, KV_TILE
    nk = S // tk
    row = lambda n: pl.BlockSpec((None, tm, n), lambda b, i: (b, i, 0))
    feat = lambda n: pl.BlockSpec((None, n, tm), lambda b, i: (b, 0, i))
    tile = lambda n: pl.BlockSpec((None, tm // tk, n, tk), lambda b, i: (b, i, 0, 0))
    tab_rm = pl.BlockSpec((tm, 128), lambda b, i: (i, 0))
    tab_t = pl.BlockSpec((128, tm), lambda b, i: (0, i))
    in_specs = [row(D_MODEL), _const_spec(wrm.shape), _const_spec(wt.shape),
                tab_rm, tab_rm, tab_rm, tab_rm, tab_t, tab_t, tab_t, tab_t,
                _const_spec(gb.shape), _const_spec(lng.shape), _const_spec(lnb.shape),
                _const_spec(ws.shape), _const_spec(bsm.shape)]
    sd = jax.ShapeDtypeStruct
    out_shape = [sd((B, S, 512), BF16), sd((B, S, 256), BF16), sd((B, S, 256), BF16),
                 sd((B, S, 128), F32), sd((B, S, 256), BF16), sd((B, S, 128), F32),
                 sd((B, 128, S), F32), sd((B, 128, S), F32), sd((B, 128, S), F32),
                 sd((B, 512, S), F32), sd((B, 512, S), F32),
                 sd((B, nk, 128, tk), BF16), sd((B, nk, 128, tk), BF16),
                 sd((B, nk, 512, tk), BF16), sd((B, nk, 512, tk), BF16)]
    out_specs = [row(512), row(256), row(256), row(128), row(256), row(128),
                 feat(128), feat(128), feat(128), feat(512), feat(512),
                 tile(128), tile(128), tile(512), tile(512)]
    return pl.pallas_call(
        functools.partial(_proj_prompt_kernel, tm=tm, tk=tk),
        grid=(B, S // tm), in_specs=in_specs, out_specs=out_specs, out_shape=out_shape,
        compiler_params=_params(("parallel", "parallel")), name="proj_prompt",
    )(x, wrm, wt, *tabs, gb, lng, lnb, ws, bsm)


def _proj_sample(x, wrm, wt, tabs, gb, lng, lnb, w00, b0):
    nb = x.shape[0]
    full = lambda a: pl.BlockSpec(a.shape, lambda i: (0,) * a.ndim)
    ins = (x, wrm, wt, *tabs, gb, lng, lnb, w00, b0)
    sd = jax.ShapeDtypeStruct
    out_shape = [sd((nb, 512), BF16), sd((nb, 256), BF16), sd((nb, 256), BF16), sd((nb, 128), F32),
                 sd((nb, 256), BF16), sd((nb, 256), F32),
                 sd((nb, 128), F32), sd((nb, 128), F32), sd((nb, 512), F32),
                 sd((128, nb), F32), sd((128, nb), F32), sd((128, nb), F32),
                 sd((512, nb), F32), sd((512, nb), F32)]
    return pl.pallas_call(
        _proj_sample_kernel, grid=(1,), in_specs=[full(a) for a in ins],
        out_specs=[pl.BlockSpec(s.shape, lambda i: (0, 0)) for s in out_shape], out_shape=out_shape,
        compiler_params=_params(("arbitrary",)), name="proj_sample",
    )(*ins)


def _compress_kernel(x_ref, pe_ref, w1_ref, w2_ref, o_ref):
    h = _dot((x_ref[...] + pe_ref[...]).astype(BF16), w1_ref[...])
    o_ref[...] = _dot(_gelu(h).astype(BF16), w2_ref[...]).astype(o_ref.dtype)


def _compress(x, pe, w1, w2, out_dtype):
    n, f = x.shape
    tb = math.gcd(n, 256)
    return pl.pallas_call(
        _compress_kernel, grid=(n // tb,),
        in_specs=[pl.BlockSpec((tb, f), lambda i: (i, 0)), _const_spec(pe.shape),
                  _const_spec(w1.shape), _const_spec(w2.shape)],
        out_specs=pl.BlockSpec((tb, 128), lambda i: (i, 0)),
        out_shape=jax.ShapeDtypeStruct((n, 128), out_dtype),
        compiler_params=_params(("parallel",)), name="compress",
    )(x, pe, w1, w2)


def _select_topk(score, blk_f, n_blocks, k):
    sel = jnp.zeros(score.shape, F32)
    work = score
    for _ in range(k):
        mx = jnp.max(work, axis=1, keepdims=True)
        idx = jnp.min(jnp.where(work == mx, blk_f, float(n_blocks)), axis=1, keepdims=True)
        pick = blk_f == idx
        sel = jnp.where(pick, 1.0, sel)
        work = jnp.where(pick, NEG_BIG, work)
    return sel


def _masked_softmax(logits, mask):
    lm = jnp.where(mask, logits, NEG_INF)
    mx = jnp.max(lm, axis=1, keepdims=True)
    e = jnp.where(mask, jnp.exp(lm - mx), 0.0)
    return e / jnp.maximum(jnp.sum(e, axis=1, keepdims=True), TINY)


def _cmp_block_end(n_half):
    r = _iota((1, 2 * n_half), 1)
    return jnp.where(r < n_half, 64 * r + 31, 64 * (r - n_half) + 63)


def _nsa_prompt_kernel(qf_ref, g_ref, kvc_ref, sel_ref, win_ref, o_ref, acc_ref, m_ref, l_ref,
                       *, tq, tk, ns, topk):
    i = pl.program_id(1)
    tpos = i * tq + _iota((tq, 1), 0)
    qh = [qf_ref[:, h * 128:(h + 1) * 128] for h in range(N_HEADS)]
    kvc = kvc_ref[...]

    maskc = _cmp_block_end(ns) <= tpos
    oc, impc = [], jnp.zeros((tq, 2 * ns), F32)
    for h in range(N_HEADS):
        pc = _masked_softmax(_dot_nt(qh[h], kvc), maskc)
        oc.append(_dot(pc.astype(BF16), kvc))
        impc = impc + pc
    imp = impc[:, :ns] + impc[:, ns:]

    blk = _iota((tq, ns), 1)
    cur = tpos >> 6
    forced = (blk == 0) | (blk == cur) | (blk == cur - 1)
    valid = (blk << 6) <= tpos
    score = jnp.where(forced, FORCE_SCORE, jnp.where(valid, imp, -1.0))
    sel_bf = _select_topk(score, blk.astype(F32), ns, topk).astype(BF16)

    def attend(kv_ref, j_lo, j_hi, mask_fn):
        acc_ref[...] = jnp.zeros(acc_ref.shape, F32)
        m_ref[...] = jnp.full(m_ref.shape, NEG_INF, F32)
        l_ref[...] = jnp.zeros(l_ref.shape, F32)

        def body(j, carry):
            kt = kv_ref[j]
            ok = mask_fn(j, j * tk + _iota((1, tk), 1))
            for h in range(N_HEADS):
                sm = jnp.where(ok, _dot(qh[h], kt), NEG_INF)
                m_old = m_ref[h]
                m_new = jnp.maximum(m_old, jnp.max(sm, axis=1, keepdims=True))
                p = jnp.where(ok, jnp.exp(sm - m_new), 0.0)
                alpha = jnp.exp(m_old - m_new)
                l_ref[h] = alpha * l_ref[h] + jnp.sum(p, axis=1, keepdims=True)
                acc_ref[h] = alpha * acc_ref[h] + _dot_nt(p.astype(BF16), kt)
                m_ref[h] = m_new
            return carry

        lax.fori_loop(j_lo, j_hi, body, 0)
        return [acc_ref[h] / jnp.maximum(l_ref[h], TINY) for h in range(N_HEADS)]

    n_tiles = (i * tq) // tk + tq // tk

    def sel_mask(j, kpos):
        expand = (_iota((ns, tk), 0) == j * (tk // SEL_BLOCK) + (_iota((ns, tk), 1) >> 6))
        chosen = _dot(sel_bf, expand.astype(BF16)) > 0.5
        return chosen & (kpos <= tpos)

    def win_mask(j, kpos):
        d = tpos - kpos
        return (d >= 0) & (d < WINDOW)

    osel = attend(sel_ref, 0, n_tiles, sel_mask)
    ow = attend(win_ref, jnp.maximum(n_tiles - (tq + WINDOW) // tk, 0), n_tiles, win_mask)

    g = g_ref[...]
    lane = _iota((tq, 128), 1)
    mixed = [g[:, 3 * h:3 * h + 1] * oc[h] + g[:, 3 * h + 1:3 * h + 2] * osel[h]
             + g[:, 3 * h + 2:3 * h + 3] * ow[h] for h in range(N_HEADS)]
    for k in range(2):
        even = pltpu.roll(mixed[2 * k], 64, 1)
        o_ref[:, k * 128:(k + 1) * 128] = jnp.where(lane < 64, even, mixed[2 * k + 1]).astype(BF16)


def _nsa_prompt(qaf, gates, kvc, selbf, winbf):
    B, S, _ = qaf.shape
    tq, tk = Q_TILE, KV_TILE
    nk = S // tk
    ns = S // SEL_BLOCK
    kern = functools.partial(_nsa_prompt_kernel, tq=tq, tk=tk, ns=ns, topk=min(SEL_TOPK, ns))
    row = lambda n: pl.BlockSpec((None, tq, n), lambda b, i: (b, i, 0))
    whole = lambda a: pl.BlockSpec((None,) + a.shape[1:], lambda b, i: (b,) + (0,) * (a.ndim - 1))
    return pl.pallas_call(
        kern, grid=(B, S // tq),
        in_specs=[row(512), row(128), whole(kvc), whole(selbf), whole(winbf)],
        out_specs=row(256), out_shape=jax.ShapeDtypeStruct((B, S, 256), BF16),
        scratch_shapes=[pltpu.VMEM((N_HEADS, tq, 128), F32), pltpu.VMEM((N_HEADS, tq, 1), F32),
                        pltpu.VMEM((N_HEADS, tq, 1), F32)],
        compiler_params=_params(("parallel", "arbitrary")), name="nsa_prompt",
    )(qaf, gates, kvc, selbf, winbf)


def _sb_prompt_kernel(q_ref, kv_ref, o_ref, acc_ref, car_ref, *, tq, tk):
    i = pl.program_id(1)
    tpos = i * tq + _iota((tq, 1), 0)
    q = q_ref[...].astype(F32)
    lane_head = _iota((1, GROUP_WIDTH), 1) >> 6
    qm = [jnp.where(lane_head == h, q, 0.0).astype(BF16) for h in range(N_HEADS)]
    suffix = _suffix_matrix(tk)
    acc_ref[...] = jnp.zeros(acc_ref.shape, F32)
    car_ref[...] = jnp.zeros(car_ref.shape, F32)
    n_tiles = (i * tq) // tk + tq // tk

    def body(jj, carry):
        j = n_tiles - 1 - jj
        kt = kv_ref[j, 0:256, :]
        vt = kv_ref[j, 256:512, :]
        causal = (j * tk + _iota((1, tk), 1)) < tpos
        for h in range(N_HEADS):
            z = _dot(qm[h], kt)
            log_keep = jnp.where(causal, -_softplus(z), 0.0)
            hi, lo = _split_bf16(log_keep)
            incl = _dot(hi, suffix) + _dot(lo, suffix)
            a = jnp.where(causal, jnp.exp(z + incl + car_ref[h]), 0.0)
            acc_ref[h] += _dot_nt(a.astype(BF16), vt)
            car_ref[h] += incl[:, 0:1]
        return carry

    lax.fori_loop(0, n_tiles, body, 0)
    out = jnp.zeros((tq, GROUP_WIDTH), F32)
    for h in range(N_HEADS):
        out = jnp.where(lane_head == h, acc_ref[h], out)
    o_ref[...] = out.astype(BF16)


def _sb_prompt(qb, sbbf):
    B, S, _ = qb.shape
    tq, tk = Q_TILE, KV_TILE
    row = lambda n: pl.BlockSpec((None, tq, n), lambda b, i: (b, i, 0))
    return pl.pallas_call(
        functools.partial(_sb_prompt_kernel, tq=tq, tk=tk), grid=(B, S // tq),
        in_specs=[row(256), pl.BlockSpec((None,) + sbbf.shape[1:], lambda b, i: (b, 0, 0, 0))],
        out_specs=row(256), out_shape=jax.ShapeDtypeStruct((B, S, 256), BF16),
        scratch_shapes=[pltpu.VMEM((N_HEADS, tq, GROUP_WIDTH), F32), pltpu.VMEM((N_HEADS, tq, 1), F32)],
        compiler_params=_params(("parallel", "arbitrary")), name="sb_prompt",
    )(qb, sbbf)


def _diff_lambda(dl_ref, lam_init):
    lq = dl_ref[...]
    s1 = jnp.sum(lq[0:1, :] * lq[1:2, :], axis=1, keepdims=True)
    s2 = jnp.sum(lq[2:3, :] * lq[3:4, :], axis=1, keepdims=True)
    return jnp.exp(s1) - jnp.exp(s2) + lam_init


def _head_rms_norm(out, lane_head, ng, lam_init):
    sq = out * out
    rs = jnp.zeros(out.shape, F32)
    for h in range(N_HEADS):
        ms = jnp.sum(jnp.where(lane_head == h, sq, 0.0), axis=1, keepdims=True) * (1.0 / HEAD_DIM)
        rs = jnp.where(lane_head == h, lax.rsqrt(ms + LN_EPS), rs)
    return out * rs * ng * (1.0 - lam_init)


def _diff_prompt_kernel(li_ref, dl_ref, ng_ref, q_ref, kv_ref, o_ref, acc_ref, m_ref, l_ref, *, tq, tk):
    i = pl.program_id(1)
    tpos = i * tq + _iota((tq, 1), 0)
    lam_init = li_ref[0]
    lam = _diff_lambda(dl_ref, lam_init)
    q = q_ref[...].astype(F32)
    lane = _iota((1, GROUP_WIDTH), 1)
    lane_head, lane_half = lane >> 6, (lane >> 5) & 1
    qm = [jnp.where((lane_head == h) & (lane_half == m), q, 0.0).astype(BF16)
          for h in range(N_HEADS) for m in range(2)]
    acc_ref[...] = jnp.zeros(acc_ref.shape, F32)
    m_ref[...] = jnp.full(m_ref.shape, NEG_INF, F32)
    l_ref[...] = jnp.zeros(l_ref.shape, F32)
    n_tiles = (i * tq) // tk + tq // tk

    def body(j, carry):
        kt = kv_ref[j, 0:256, :]
        vt = kv_ref[j, 256:512, :]
        ok = (j * tk + _iota((1, tk), 1)) <= tpos
        for a in range(2 * N_HEADS):
            sm = jnp.where(ok, _dot(qm[a], kt), NEG_INF)
            m_old = m_ref[a]
            m_new = jnp.maximum(m_old, jnp.max(sm, axis=1, keepdims=True))
            p = jnp.where(ok, jnp.exp(sm - m_new), 0.0)
            alpha = jnp.exp(m_old - m_new)
            l_ref[a] = alpha * l_ref[a] + jnp.sum(p, axis=1, keepdims=True)
            acc_ref[a] = alpha * acc_ref[a] + _dot_nt(p.astype(BF16), vt)
            m_ref[a] = m_new
        return carry

    lax.fori_loop(0, n_tiles, body, 0)
    out = jnp.zeros((tq, GROUP_WIDTH), F32)
    for h in range(N_HEADS):
        o1 = acc_ref[2 * h] / jnp.maximum(l_ref[2 * h], TINY)
        o2 = acc_ref[2 * h + 1] / jnp.maximum(l_ref[2 * h + 1], TINY)
        out = jnp.where(lane_head == h, o1 - lam * o2, out)
    o_ref[...] = _head_rms_norm(out, lane_head, ng_ref[...], lam_init).astype(BF16)


def _diff_prompt(lam_init, dl, ng, qd, diffbf):
    B, S, _ = qd.shape
    tq, tk = Q_TILE, KV_TILE
    row = lambda n: pl.BlockSpec((None, tq, n), lambda b, i: (b, i, 0))
    return pl.pallas_call(
        functools.partial(_diff_prompt_kernel, tq=tq, tk=tk), grid=(B, S // tq),
        in_specs=[pl.BlockSpec(memory_space=pltpu.SMEM), _const_spec(dl.shape), _const_spec(ng.shape),
                  row(256), pl.BlockSpec((None,) + diffbf.shape[1:], lambda b, i: (b, 0, 0, 0))],
        out_specs=row(256), out_shape=jax.ShapeDtypeStruct((B, S, 256), BF16),
        scratch_shapes=[pltpu.VMEM((2 * N_HEADS, tq, GROUP_WIDTH), F32),
                        pltpu.VMEM((2 * N_HEADS, tq, 1), F32), pltpu.VMEM((2 * N_HEADS, tq, 1), F32)],
        compiler_params=_params(("parallel", "arbitrary")), name="diff_prompt",
    )(lam_init, dl, ng, qd, diffbf)


def _round_bf16(x):
    return x.astype(BF16).astype(F32)


def _softmax_with_self(s, ok, s_self):
    sm = jnp.where(ok, s, NEG_INF)
    mx = jnp.maximum(jnp.max(sm, axis=1, keepdims=True), s_self)
    e = jnp.where(ok, jnp.exp(sm - mx), 0.0)
    e_self = jnp.exp(s_self - mx)
    inv = 1.0 / (jnp.sum(e, axis=1, keepdims=True) + e_self)
    return e * inv, e_self * inv


def _nsa_sample_kernel(pt_ref, qh_ref, gs_ref, snew_ref, wnew_ref, win_ref, *rest,
                       n_pages, past, half, topk):
    del pt_ref
    cmp_refs, sel_refs = rest[:n_pages], rest[n_pages:2 * n_pages]
    o_ref, kvc_ref = rest[2 * n_pages], rest[2 * n_pages + 1]
    qh = qh_ref[...]
    qf = qh.astype(F32)

    kvc_ref[...] = jnp.zeros(kvc_ref.shape, F32)
    for p in range(n_pages):
        piece = cmp_refs[p][...]
        kvc_ref[2 * p:2 * p + 2, :] = piece[0:2, :]
        kvc_ref[half + 2 * p:half + 2 * p + 2, :] = piece[2:4, :]
    kvc = kvc_ref[...].astype(BF16)
    maskc = _cmp_block_end(half) <= past
    pc = _masked_softmax(_dot_nt(qh, kvc), maskc)
    oc = _dot(pc.astype(BF16), kvc)
    head_row = _iota((8, 1), 0) < N_HEADS
    impc = jnp.sum(jnp.where(head_row, pc, 0.0), axis=0, keepdims=True)
    imp = impc[:, :half] + impc[:, half:]

    n_sel = past // SEL_BLOCK + 1
    blk = _iota((1, half), 1)
    cur = past // SEL_BLOCK
    forced = (blk == 0) | (blk == cur) | (blk == cur - 1)
    valid = blk * SEL_BLOCK <= past
    score = jnp.where(blk < n_sel, jnp.where(forced, FORCE_SCORE, jnp.where(valid, imp, -1.0)), NEG_BIG)
    sel = _select_topk(score, blk.astype(F32), half, topk)
    sel8 = jnp.broadcast_to(sel, (8, half)).astype(BF16)
    expand = _iota((half, past), 0) == (_iota((half, past), 1) >> 6)
    chosen = _dot(sel8, expand.astype(BF16)) > 0.5

    def branch(pages, new_row, ok):
        new_b = _round_bf16(new_row)
        s = jnp.concatenate([_dot(qh, t) for t in pages], axis=1)
        s_self = jnp.sum(qf * new_b, axis=1, keepdims=True)
        p, p_self = _softmax_with_self(s, ok, s_self)
        out = p_self * new_b
        off = 0
        for t in pages:
            n = t.shape[1]
            out = out + _dot_nt(p[:, off:off + n].astype(BF16), t)
            off += n
        return out

    osel = branch([r[...].astype(BF16) for r in sel_refs], snew_ref[...], chosen)
    w_past = win_ref.shape[1]
    in_window = _iota((1, w_past), 1) >= (w_past - WINDOW + 1)
    ow = branch([win_ref[...].astype(BF16)], wnew_ref[...], in_window)
    g = gs_ref[...]
    o_ref[...] = g[:, 0:1] * oc + g[:, 1:2] * osel + g[:, 2:3] * ow


def _sb_sample_kernel(pt_ref, q_ref, *rest, n_pages, past, cb):
    del pt_ref
    pages, o_ref = rest[:n_pages], rest[n_pages]
    q8 = jnp.broadcast_to(q_ref[...].astype(F32), (8, GROUP_WIDTH))
    row_head = _iota((8, GROUP_WIDTH), 0) == (_iota((8, GROUP_WIDTH), 1) >> 6)
    qm = jnp.where(row_head, q8, 0.0).astype(BF16)
    z = jnp.concatenate([_dot(qm, pages[p][0].astype(BF16)) for p in range(n_pages)], axis=1)
    log_keep = -_softplus(z)
    suffix = _suffix_matrix(cb)
    carry = jnp.zeros((8, 1), F32)
    expo = [None] * (past // cb)
    for bi in reversed(range(past // cb)):
        cols = slice(bi * cb, (bi + 1) * cb)
        hi, lo = _split_bf16(log_keep[:, cols])
        incl = _dot(hi, suffix) + _dot(lo, suffix)
        expo[bi] = z[:, cols] + incl + carry
        carry = carry + incl[:, 0:1]
    a = jnp.exp(jnp.concatenate(expo, axis=1))
    out = jnp.zeros((8, GROUP_WIDTH), F32)
    for p in range(n_pages):
        out = out + _dot_nt(a[:, p * 128:(p + 1) * 128].astype(BF16), pages[p][1].astype(BF16))
    o_ref[...] = jnp.sum(jnp.where(row_head, out, 0.0), axis=0, keepdims=True).astype(BF16)


def _diff_sample_kernel(pt_ref, li_ref, dl_ref, ng_ref, q_ref, dnew_ref, *rest, n_pages):
    del pt_ref
    pages, o_ref = rest[:n_pages], rest[n_pages]
    lam_init = li_ref[0]
    lam = _diff_lambda(dl_ref, lam_init)
    q8 = jnp.broadcast_to(q_ref[...].astype(F32), (8, GROUP_WIDTH))
    lane = _iota((8, GROUP_WIDTH), 1)
    row_head = _iota((8, GROUP_WIDTH), 0) == (lane >> 6)
    lane_half = (lane >> 5) & 1
    new_k = _round_bf16(dnew_ref[:, 0:256])
    new_v = _round_bf16(dnew_ref[:, 256:512])
    kts = [pages[p][0].astype(BF16) for p in range(n_pages)]
    ps, ps_self = [], []
    for m in range(2):
        qm = jnp.where(row_head & (lane_half == m), q8, 0.0)
        s = jnp.concatenate([_dot(qm.astype(BF16), kt) for kt in kts], axis=1)
        s_self = jnp.sum(qm * new_k, axis=1, keepdims=True)
        p, p_self = _softmax_with_self(s, jnp.full(s.shape, True), s_self)
        ps.append(p)
        ps_self.append(p_self)
    p = ps[0] - lam * ps[1]
    out = (ps_self[0] - lam * ps_self[1]) * new_v
    for pg in range(n_pages):
        out = out + _dot_nt(p[:, pg * 128:(pg + 1) * 128].astype(BF16), pages[pg][1].astype(BF16))
    o4 = jnp.sum(jnp.where(row_head, out, 0.0), axis=0, keepdims=True)
    o_ref[...] = _head_rms_norm(o4, _iota((1, GROUP_WIDTH), 1) >> 6, ng_ref[...], lam_init).astype(BF16)


def _page_spec(block, layer, p):
    nz = (0,) * (len(block) - 2)
    return pl.BlockSpec(block, lambda b, pt: (layer, pt[b, p]) + nz)


def _nsa_sample(layer, page_table, qh, gs, snew, wnew, cmp_pool, sel_cache, win_cache):
    nb, n_pages = page_table.shape
    past = n_pages * 128
    half = max(128, -(-(past // SEL_BLOCK + 1) // 128) * 128)
    n_sel = past // SEL_BLOCK + 1
    kern = functools.partial(_nsa_sample_kernel, n_pages=n_pages, past=past, half=half,
                             topk=min(SEL_TOPK, n_sel))
    per_seq = lambda a: pl.BlockSpec((None,) + a.shape[1:], lambda b, pt: (b,) + (0,) * (a.ndim - 1))
    in_specs = [per_seq(qh), per_seq(gs), per_seq(snew), per_seq(wnew),
                pl.BlockSpec((None, None) + win_cache.shape[2:], lambda b, pt: (layer, b, 0, 0))]
    in_specs += [pl.BlockSpec((None, 4, 128), functools.partial(lambda b, pt, p: (pt[b, p], 0, 0), p=p))
                 for p in range(n_pages)]
    in_specs += [_page_spec((None, None, 128, 128), layer, p) for p in range(n_pages)]
    grid_spec = pltpu.PrefetchScalarGridSpec(
        num_scalar_prefetch=1, grid=(nb,), in_specs=in_specs,
        out_specs=pl.BlockSpec((None, 8, 128), lambda b, pt: (b, 0, 0)),
        scratch_shapes=[pltpu.VMEM((2 * half, 128), F32)])
    return pl.pallas_call(
        kern, grid_spec=grid_spec, out_shape=jax.ShapeDtypeStruct((nb, 8, 128), F32),
        compiler_params=_params(("arbitrary",)), name="nsa_sample",
    )(page_table, qh, gs, snew, wnew, win_cache, *([cmp_pool] * n_pages), *([sel_cache] * n_pages))


def _sb_sample(layer, page_table, qb, cache):
    nb, n_pages = page_table.shape
    past = n_pages * 128
    cb = math.gcd(past, 256)
    in_specs = [pl.BlockSpec((None, 1, 256), lambda b, pt: (b, 0, 0))]
    in_specs += [_page_spec((None, None, 2, 256, 128), layer, p) for p in range(n_pages)]
    grid_spec = pltpu.PrefetchScalarGridSpec(
        num_scalar_prefetch=1, grid=(nb,), in_specs=in_specs,
        out_specs=pl.BlockSpec((None, 1, 256), lambda b, pt: (b, 0, 0)))
    return pl.pallas_call(
        functools.partial(_sb_sample_kernel, n_pages=n_pages, past=past, cb=cb), grid_spec=grid_spec,
        out_shape=jax.ShapeDtypeStruct((nb, 1, 256), BF16),
        compiler_params=_params(("arbitrary",)), name="sb_sample",
    )(page_table, qb, *([cache] * n_pages))


def _diff_sample(layer, page_table, lam_init, dl, ng, qd, dnew, cache):
    nb, n_pages = page_table.shape
    const = lambda a: pl.BlockSpec(a.shape, lambda b, pt: (0,) * a.ndim)
    in_specs = [pl.BlockSpec(memory_space=pltpu.SMEM), const(dl), const(ng),
                pl.BlockSpec((None, 1, 256), lambda b, pt: (b, 0, 0)),
                pl.BlockSpec((None, 1, 512), lambda b, pt: (b, 0, 0))]
    in_specs += [_page_spec((None, None, 2, 256, 128), layer, p) for p in range(n_pages)]
    grid_spec = pltpu.PrefetchScalarGridSpec(
        num_scalar_prefetch=1, grid=(nb,), in_specs=in_specs,
        out_specs=pl.BlockSpec((None, 1, 256), lambda b, pt: (b, 0, 0)))
    return pl.pallas_call(
        functools.partial(_diff_sample_kernel, n_pages=n_pages), grid_spec=grid_spec,
        out_shape=jax.ShapeDtypeStruct((nb, 1, 256), BF16),
        compiler_params=_params(("arbitrary",)), name="diff_sample",
    )(page_table, lam_init, dl, ng, qd, dnew, *([cache] * n_pages))


def _tail_kernel(x_ref, oa_ref, ob_ref, oc_ref, od_ref, wo_ref, g1_ref, b1_ref,
                 wg_ref, wu_ref, wd_ref, g2_ref, b2_ref, y_ref, *, alpha):
    mix = (_dot(oa_ref[...], wo_ref[0:256, :]) + _dot(ob_ref[...], wo_ref[256:512, :])
           + _dot(oc_ref[...], wo_ref[512:768, :]) + _dot(od_ref[...], wo_ref[768:1024, :]))
    x1 = _layer_norm(alpha * x_ref[...] + mix, g1_ref[...], b1_ref[...])
    x1b = x1.astype(BF16)
    hg = _dot(x1b, wg_ref[...])
    hu = _dot(x1b, wu_ref[...])
    act = (hg * _sigmoid(hg) * hu).astype(BF16)
    y_ref[...] = _layer_norm(alpha * x1 + _dot(act, wd_ref[...]), g2_ref[...], b2_ref[...])


def _tail(x, oa, ob, oc, od, wo, g1, b1, wg, wu, wd, g2, b2, alpha):
    m = x.shape[0]
    tm = math.gcd(m, TOKEN_TILE)
    row = lambda n: pl.BlockSpec((tm, n), lambda i: (i, 0))
    consts = (wo, g1, b1, wg, wu, wd, g2, b2)
    return pl.pallas_call(
        functools.partial(_tail_kernel, alpha=alpha), grid=(m // tm,),
        in_specs=[row(D_MODEL), row(256), row(256), row(256), row(256)] + [_const_spec(c.shape) for c in consts],
        out_specs=row(D_MODEL), out_shape=jax.ShapeDtypeStruct((m, D_MODEL), F32),
        compiler_params=_params(("parallel",)), name="tail",
    )(x, oa, ob, oc, od, *consts)


def _rot_partner(w, group, half):
    k = w.shape[0]
    wr = w.reshape(k, -1, group)
    x1, x2 = wr[..., :half], wr[..., half:2 * half]
    return jnp.concatenate([-x2, x1, jnp.zeros_like(wr[..., 2 * half:])], axis=-1).reshape(k, -1)


def _pad_heads(w):
    k = w.shape[0]
    wr = w.reshape(k, -1, HEAD_DIM)
    return jnp.concatenate([wr, jnp.zeros_like(wr)], axis=-1).reshape(k, -1)


def _layer_weights(w, sample):
    off = [0]
    for n in (256, 64, 64, 64, 64, 64, 64, 12, 256, 256, 256, 256, 256, 256, 256, 256):
        off.append(off[-1] + n)
    col = lambda i: w[:, off[i]:off[i + 1]]
    qa, kc, vc, ks, vs, kw, vw, ga, qb, kb, vb, qd, kd, vd, u, v = [col(i) for i in range(16)]
    zeros64 = jnp.zeros((w.shape[0], 64), F32)
    qa = qa * (HEAD_DIM ** -0.5)
    qb = qb * (HEAD_DIM ** -0.5)
    qd = qd * (DIFF_QK ** -0.5)
    sel = jnp.concatenate([ks, vs], 1)
    sel_sw = jnp.concatenate([_rot_partner(ks, HEAD_DIM, ROT_HALF), zeros64], 1)
    win = jnp.concatenate([kw, vw], 1)
    win_sw = jnp.concatenate([_rot_partner(kw, HEAD_DIM, ROT_HALF), zeros64], 1)
    kd_sw = _rot_partner(kd, DIFF_QK, DIFF_ROT_HALF)
    rm = [_pad_heads(qa), _pad_heads(_rot_partner(qa, HEAD_DIM, ROT_HALF)), qb, qd,
          _rot_partner(qd, DIFF_QK, DIFF_ROT_HALF), u, v,
          jnp.concatenate([ga, jnp.zeros((w.shape[0], 128 - ga.shape[1]), F32)], 1),
          jnp.concatenate([kc, vc], 1)]
    if sample:
        rm += [sel, sel_sw, win, win_sw, kd, vd, kd_sw]
    tr = [jnp.concatenate([kc, vc], 1), sel, sel_sw, win, win_sw, kb, vb, kd, vd, kd_sw]
    return (jnp.concatenate(rm, 1).astype(BF16), jnp.concatenate(tr, 1).T.astype(BF16))


def _rope_tables(pos):
    pos = pos.astype(F32)[:, None]
    n = pos.shape[0]

    def table(half, plain, reps):
        freq = ROPE_THETA ** (-jnp.arange(half, dtype=F32) / half)
        ang = pos * freq[None, :]
        c = jnp.concatenate([jnp.cos(ang), jnp.cos(ang), jnp.ones((n, plain), F32)], 1)
        s = jnp.concatenate([jnp.sin(ang), jnp.sin(ang), jnp.zeros((n, plain), F32)], 1)
        return jnp.tile(c, (1, reps)), jnp.tile(s, (1, reps))

    ca, sa = table(ROT_HALF, 128 - 2 * ROT_HALF, 1)
    cd, sd = table(DIFF_ROT_HALF, DIFF_QK - 2 * DIFF_ROT_HALF, 4)
    return (ca, sa, cd, sd, ca.T, sa.T, cd.T, sd.T)


def _compress_weights(pe, w1, w2):
    z = jnp.zeros((CMP_BLOCK, HEAD_DIM, HEAD_DIM), F32)
    w1k = w1[0].reshape(CMP_BLOCK, HEAD_DIM, HEAD_DIM)
    w1v = w1[1].reshape(CMP_BLOCK, HEAD_DIM, HEAD_DIM)
    w1c = jnp.concatenate([jnp.concatenate([w1k, z], 2), jnp.concatenate([z, w1v], 2)], 1)
    z2 = jnp.zeros((HEAD_DIM, HEAD_DIM), F32)
    w2c = jnp.concatenate([jnp.concatenate([w2[0], z2], 1), jnp.concatenate([z2, w2[1]], 1)], 0)
    pe_flat = jnp.concatenate([pe[0], pe[1]], 1).reshape(1, CMP_BLOCK * 128)
    return pe_flat, w1c.reshape(CMP_BLOCK * 128, 128).astype(BF16), w2c.astype(BF16)


def _feature_major(x, tail_shape):
    lead = x.shape[:-2]
    n = x.shape[-1]
    y = x.reshape(lead + tail_shape + (n,))
    nd = y.ndim
    return jnp.transpose(y, tuple(range(len(lead))) + (nd - 1,) + tuple(range(len(lead), nd - 1)))


def kernel(x_prompt, x_sample, cache_nsa_cmp, cache_nsa_sel, cache_nsa_win, cache_sb, cache_diff,
           page_table, w_in, gate_b, cmp_pe, cmp_w1, cmp_w2, diff_lam, diff_norm_g, gmlp_ln_g,
           gmlp_ln_b, gmlp_ws, gmlp_bs, w_out, ln1_g, ln1_b, w_gate, w_up, w_down, ln2_g, ln2_b):
    B, S, _ = x_prompt.shape
    nb, T, _ = x_sample.shape
    depth = w_in.shape[0]
    n_pool = cache_sb.shape[1]
    n_pages = page_table.shape[1]
    past = n_pages * cache_sb.shape[2]
    w_past = cache_nsa_win.shape[2]
    assert T == 1 and cache_sb.shape[2] == 128 and S % Q_TILE == 0 and past % SEL_BLOCK == 0
    alpha = (2 * depth) ** 0.25

    to_pages = lambda c: jnp.transpose(c, (0, 1, 3, 4, 5, 2))
    sel_cache = to_pages(cache_nsa_sel).reshape(depth, n_pool, 128, 128)
    win_cache = to_pages(cache_nsa_win).reshape(depth, nb, 128, w_past)
    sb_cache = to_pages(cache_sb).reshape(depth, n_pool, 2, 256, 128)
    diff_cache = to_pages(cache_diff).reshape(depth, n_pool, 2, 256, 128)

    tabs_p = _rope_tables(jnp.arange(S, dtype=jnp.int32))
    tabs_s = _rope_tables(jnp.full((nb,), past, jnp.int32))
    row = lambda a: a.reshape(1, -1)

    xp, xs = x_prompt, x_sample.reshape(nb, D_MODEL)
    outs = [[] for _ in range(11)]
    for l in range(depth):
        lam_init = jnp.full((1,), 0.8 - 0.6 * math.exp(-0.3 * l), F32)
        wrm_p, wt = _layer_weights(w_in[l], sample=False)
        wrm_s, _ = _layer_weights(w_in[l], sample=True)
        gb = jnp.concatenate([gate_b[l], jnp.zeros((128 - gate_b.shape[1],), F32)]).reshape(1, 128)
        lng, lnb = row(gmlp_ln_g[l]), row(gmlp_ln_b[l])
        bsm = jnp.repeat(gmlp_bs[l].T, HEAD_DIM, axis=1)
        w00 = row(jnp.repeat(gmlp_ws[l][:, 0, 0], HEAD_DIM))
        b0 = row(jnp.repeat(gmlp_bs[l][:, 0], HEAD_DIM))
        pe_flat, w1c, w2c = _compress_weights(cmp_pe[l], cmp_w1[l], cmp_w2[l])
        ng = row(jnp.tile(diff_norm_g[l], N_HEADS))
        tail_w = (w_out[l].astype(BF16), row(ln1_g[l]), row(ln1_b[l]), w_gate[l].astype(BF16),
                  w_up[l].astype(BF16), w_down[l].astype(BF16), row(ln2_g[l]), row(ln2_b[l]))

        (qaf, qb, qd, gates, od, cmp_rm, cmp_t, sel_t, win_t, sb_t, diff_t,
         selbf, winbf, sbbf, diffbf) = _proj_prompt(xp, wrm_p, wt, tabs_p, gb, lng, lnb, gmlp_ws[l], bsm)
        blocks = cmp_rm.reshape(B, S // SEL_BLOCK, 2, CMP_BLOCK * 128)
        blocks = jnp.transpose(blocks, (0, 2, 1, 3)).reshape(B * (S // CMP_BLOCK), CMP_BLOCK * 128)
        kvc = _compress(blocks, pe_flat, w1c, w2c, BF16).reshape(B, S // CMP_BLOCK, 128)
        oa = _nsa_prompt(qaf, gates, kvc, selbf, winbf)
        ob = _sb_prompt(qb, sbbf)
        oc = _diff_prompt(lam_init, diff_lam[l], ng, qd, diffbf)
        flat = lambda a: a.reshape(B * S, a.shape[-1])
        xp = _tail(flat(xp), flat(oa), flat(ob), flat(oc), flat(od), *tail_w, alpha).reshape(B, S, D_MODEL)
        outs[0].append(_feature_major(cmp_t, (2, 1, HEAD_DIM)))
        outs[2].append(_feature_major(sel_t, (2, 1, HEAD_DIM)))
        outs[4].append(_feature_major(win_t[:, :, S - min(WINDOW, S):], (2, 1, HEAD_DIM)))
        outs[6].append(_feature_major(sb_t, (2, N_HEADS, HEAD_DIM)))
        outs[8].append(_feature_major(diff_t, (2, N_HEADS, HEAD_DIM)))

        (qaf, qb, qd, gates, od, vn, sel_rm, win_rm, diff_rm,
         cmp_t, sel_t, win_t, sb_t, diff_t) = _proj_sample(xs, wrm_s, wt, tabs_s, gb, lng, lnb, w00, b0)
        pool = cache_nsa_cmp[l].reshape(n_pool * 4, CMP_BLOCK * 128)
        cmp_pool = _compress(pool, pe_flat, w1c, w2c, F32).reshape(n_pool, 4, 128)[:, jnp.array([0, 2, 1, 3])]
        qh = jnp.pad(qaf.reshape(nb, N_HEADS, 128), ((0, 0), (0, 8 - N_HEADS), (0, 0)))
        gs = jnp.pad(gates[:, :3 * N_HEADS].reshape(nb, N_HEADS, 3), ((0, 0), (0, 8 - N_HEADS), (0, 125)))
        oa = _nsa_sample(l, page_table, qh, gs, sel_rm.reshape(nb, 1, 128), win_rm.reshape(nb, 1, 128),
                         cmp_pool, sel_cache, win_cache)
        oa = oa[:, :N_HEADS, HEAD_DIM:].reshape(nb, GROUP_WIDTH).astype(BF16)
        ob = _sb_sample(l, page_table, qb.reshape(nb, 1, 256), sb_cache).reshape(nb, 256)
        oc = _diff_sample(l, page_table, lam_init, diff_lam[l], ng, qd.reshape(nb, 1, 256),
                          diff_rm.reshape(nb, 1, 512), diff_cache).reshape(nb, 256)
        xs = _tail(xs, oa, ob, oc, od, *tail_w, alpha)
        new_row = lambda t, shape: _feature_major(t, shape)[:, None]
        outs[1].append(new_row(cmp_t, (2, 1, HEAD_DIM)))
        outs[3].append(new_row(sel_t, (2, 1, HEAD_DIM)))
        outs[5].append(jnp.concatenate([cache_nsa_win[l][:, T:], new_row(win_t, (2, 1, HEAD_DIM))], axis=1))
        outs[7].append(new_row(sb_t, (2, N_HEADS, HEAD_DIM)))
        outs[9].append(new_row(diff_t, (2, N_HEADS, HEAD_DIM)))
        outs[10].append(vn.reshape(nb, 1, GROUP_WIDTH))

    stacked = [jnp.stack(o) for o in outs]
    return (xp, xs.reshape(nb, 1, D_MODEL), *stacked)
```

```python
import functools
import math

import jax
import jax.numpy as jnp
from jax import lax
from jax.experimental import pallas as pl
from jax.experimental.pallas import tpu as pltpu

D_MODEL = 1024
HEAD_DIM = 64
N_HEADS = 4
GROUP_WIDTH = N_HEADS * HEAD_DIM
CMP_BLOCK = 32
SEL_BLOCK = 64
SEL_TOPK = 16
WINDOW = 512
FORCE_SCORE = 1.0e4
DIFF_QK = HEAD_DIM // 2
CHUNK = 128
ROPE_THETA = 500000.0
ROT_HALF = HEAD_DIM // 8
DIFF_ROT_HALF = DIFF_QK // 8
LN_EPS = 1e-5
NEG_INF = -1e30
TINY = 1e-30
NEG_BIG = -3.0e38

LANES = 128
VMEM_LIMIT_BYTES = 56 * 1024 * 1024

TOKEN_TILE = 256
Q_TILE = 256
KV_TILE = 256

F32 = jnp.float32
BF16 = jnp.bfloat16

RM = dict(qa=0, qa_sw=512, qb=1024, qd=1280, qd_sw=1536, u=1792, v=2048, gate=2304, cmp=2432,
          sel=2560, sel_sw=2688, win=2816, win_sw=2944, diff=3072, kd_sw=3584)
RM_PROMPT_COLS = 2560
RM_SAMPLE_COLS = 3840
TR = dict(cmp=0, sel=128, sel_sw=256, win=384, win_sw=512, sb=640, diff=1152, kd_sw=1664)
TR_ROWS = 1920


def _dot(a, b):
    return jnp.dot(a, b, preferred_element_type=F32)


def _dot_nt(a, b):
    return lax.dot_general(a, b, (((1,), (1,)), ((), ())), preferred_element_type=F32)


def _iota(shape, dim):
    return lax.broadcasted_iota(jnp.int32, shape, dim)


def _gelu(x):
    return 0.5 * x * (1.0 + jnp.tanh(0.7978845608028654 * (x + 0.044715 * (x * x * x))))


def _sigmoid(x):
    return 1.0 / (1.0 + jnp.exp(-x))


def _layer_norm(x, g, b):
    mu = jnp.mean(x, axis=-1, keepdims=True)
    xc = x - mu
    var = jnp.mean(xc * xc, axis=-1, keepdims=True)
    return xc * lax.rsqrt(var + LN_EPS) * g + b


def _softplus(z):
    return jnp.maximum(z, 0.0) + jnp.log(1.0 + jnp.exp(-jnp.abs(z)))


def _split_bf16(x):
    hi = x.astype(BF16)
    lo = (x - hi.astype(F32)).astype(BF16)
    return hi, lo


def _suffix_matrix(n):
    return (_iota((n, n), 0) >= _iota((n, n), 1)).astype(BF16)


def _params(sem):
    return pltpu.CompilerParams(dimension_semantics=sem, vmem_limit_bytes=VMEM_LIMIT_BYTES)


def _const_spec(shape):
    zeros = (0,) * len(shape)
    return pl.BlockSpec(shape, lambda *_: zeros, pipeline_mode=pl.Buffered(1))


def _proj_common(xb, wrm_ref, cq_ref, sq_ref, cd_ref, sd_ref, gb_ref, lng_ref, lnb_ref,
                 qaf_ref, qb_ref, qd_ref, g_ref):
    def rm(name, k=0, width=128):
        a = RM[name] + k * 128
        return _dot(xb, wrm_ref[:, a:a + width])

    cq, sq = cq_ref[...], sq_ref[...]
    for h in range(N_HEADS):
        q = rm("qa", h) * cq + rm("qa_sw", h) * sq
        qaf_ref[:, h * 128:(h + 1) * 128] = q.astype(BF16)
    qb_ref[...] = rm("qb", 0, 256).astype(BF16)
    cd, sd = cd_ref[...], sd_ref[...]
    for k in range(2):
        q = rm("qd", k) * cd + rm("qd_sw", k) * sd
        qd_ref[:, k * 128:(k + 1) * 128] = q.astype(BF16)
    g_ref[...] = _sigmoid(rm("gate") + gb_ref[...])
    u = _gelu(rm("u", 0, 256))
    vn = _layer_norm(_gelu(rm("v", 0, 256)), lng_ref[...], lnb_ref[...])
    return rm, u, vn


def _proj_feature_major(xb, wt_ref, cta_ref, sta_ref, ctd_ref, std_ref):
    def tr(name, k=0, height=128):
        a = TR[name] + k * 128
        return _dot_nt(wt_ref[a:a + height, :], xb)

    cta, sta = cta_ref[...], sta_ref[...]
    ctd, std = ctd_ref[...], std_ref[...]
    cmp_t = tr("cmp")
    sel_t = tr("sel") * cta + tr("sel_sw") * sta
    win_t = tr("win") * cta + tr("win_sw") * sta
    sb_t = tr("sb", 0, 512)
    kd_t = [tr("diff", k) * ctd + tr("kd_sw", k) * std for k in range(2)]
    vd_t = tr("diff", 2, 256)
    return cmp_t, sel_t, win_t, sb_t, kd_t, vd_t


def _proj_prompt_kernel(x_ref, wrm_ref, wt_ref, cq_ref, sq_ref, cd_ref, sd_ref,
                        cta_ref, sta_ref, ctd_ref, std_ref, gb_ref, lng_ref, lnb_ref, ws_ref, bsm_ref,
                        qaf_ref, qb_ref, qd_ref, g_ref, od_ref, cmprm_ref,
                        cmpt_ref, selt_ref, wint_ref, sbt_ref, difft_ref,
                        selbf_ref, winbf_ref, sbbf_ref, diffbf_ref, *, tm, tk):
    xb = x_ref[...].astype(BF16)
    rm, u, vn = _proj_common(xb, wrm_ref, cq_ref, sq_ref, cd_ref, sd_ref, gb_ref, lng_ref, lnb_ref,
                             qaf_ref, qb_ref, qd_ref, g_ref)
    cmprm_ref[...] = rm("cmp")

    tril = _iota((CHUNK, CHUNK), 1) <= _iota((CHUNK, CHUNK), 0)
    lane_group = _iota((CHUNK, GROUP_WIDTH), 1) >> 6
    wsg = [jnp.where(tril, ws_ref[g], 0.0).astype(BF16) for g in range(N_HEADS)]
    bsm = bsm_ref[...]
    for c in range(tm // CHUNK):
        rows = slice(c * CHUNK, (c + 1) * CHUNK)
        vnc = vn[rows, :]
        mixed = bsm
        for g in range(N_HEADS):
            mixed = mixed + _dot(wsg[g], jnp.where(lane_group == g, vnc, 0.0).astype(BF16))
        od_ref[rows, :] = (u[rows, :] * mixed).astype(BF16)

    cmp_t, sel_t, win_t, sb_t, kd_t, vd_t = _proj_feature_major(
        xb, wt_ref, cta_ref, sta_ref, ctd_ref, std_ref)
    cmpt_ref[...] = cmp_t
    selt_ref[...] = sel_t
    wint_ref[...] = win_t
    sbt_ref[...] = sb_t
    difft_ref[0:128, :] = kd_t[0]
    difft_ref[128:256, :] = kd_t[1]
    difft_ref[256:512, :] = vd_t
    for t in range(tm // tk):
        cols = slice(t * tk, (t + 1) * tk)
        selbf_ref[t] = sel_t[:, cols].astype(BF16)
        winbf_ref[t] = win_t[:, cols].astype(BF16)
        sbbf_ref[t] = sb_t[:, cols].astype(BF16)
        diffbf_ref[t, 0:128, :] = kd_t[0][:, cols].astype(BF16)
        diffbf_ref[t, 128:256, :] = kd_t[1][:, cols].astype(BF16)
        diffbf_ref[t, 256:512, :] = vd_t[:, cols].astype(BF16)


def _proj_sample_kernel(x_ref, wrm_ref, wt_ref, cq_ref, sq_ref, cd_ref, sd_ref,
                        cta_ref, sta_ref, ctd_ref, std_ref, gb_ref, lng_ref, lnb_ref, w00_ref, b0_ref,
                        qaf_ref, qb_ref, qd_ref, g_ref, od_ref, vn_ref,
                        selrm_ref, winrm_ref, diffrm_ref,
                        cmpt_ref, selt_ref, wint_ref, sbt_ref, difft_ref):
    xb = x_ref[...].astype(BF16)
    rm, u, vn = _proj_common(xb, wrm_ref, cq_ref, sq_ref, cd_ref, sd_ref, gb_ref, lng_ref, lnb_ref,
                             qaf_ref, qb_ref, qd_ref, g_ref)
    vn_ref[...] = vn
    od_ref[...] = (u * (vn * w00_ref[...] + b0_ref[...])).astype(BF16)

    cq, sq = cq_ref[...], sq_ref[...]
    cd, sd = cd_ref[...], sd_ref[...]
    selrm_ref[...] = rm("sel") * cq + rm("sel_sw") * sq
    winrm_ref[...] = rm("win") * cq + rm("win_sw") * sq
    for k in range(2):
        diffrm_ref[:, k * 128:(k + 1) * 128] = rm("diff", k) * cd + rm("kd_sw", k) * sd
    diffrm_ref[:, 256:512] = rm("diff", 2, 256)

    cmp_t, sel_t, win_t, sb_t, kd_t, vd_t = _proj_feature_major(
        xb, wt_ref, cta_ref, sta_ref, ctd_ref, std_ref)
    cmpt_ref[...] = cmp_t
    selt_ref[...] = sel_t
    wint_ref[...] = win_t
    sbt_ref[...] = sb_t
    difft_ref[0:128, :] = kd_t[0]
    difft_ref[128:256, :] = kd_t[1]
    difft_ref[256:512, :] = vd_t


def _proj_prompt(x, wrm, wt, tabs, gb, lng, lnb, ws, bsm):
    B, S, _ = x.shape
    tm, tk = TOKEN_TILE, KV_TILE
    nk = S // tk
    row = lambda n: pl.BlockSpec((None, tm, n), lambda b, i: (b, i, 0))
    feat = lambda n: pl.BlockSpec((None, n, tm), lambda b, i: (b, 0, i))
    tile = lambda n: pl.BlockSpec((None, tm // tk, n, tk), lambda b, i: (b, i, 0, 0))
    tab_rm = pl.BlockSpec((tm, 128), lambda b, i: (i, 0))
    tab_t = pl.BlockSpec((128, tm), lambda b, i: (0, i))
    in_specs = [row(D_MODEL), _const_spec(wrm.shape), _const_spec(wt.shape),
                tab_rm, tab_rm, tab_rm, tab_rm, tab_t, tab_t, tab_t, tab_t,
                _const_spec(gb.shape), _const_spec(lng.shape), _const_spec(lnb.shape),
                _const_spec(ws.shape), _const_spec(bsm.shape)]
    sd = jax.ShapeDtypeStruct
    out_shape = [sd((B, S, 512), BF16), sd((B, S, 256), BF16), sd((B, S, 256), BF16),
                 sd((B, S, 128), F32), sd((B, S, 256), BF16), sd((B, S, 128), F32),
                 sd((B, 128, S), F32), sd((B, 128, S), F32), sd((B, 128, S), F32),
                 sd((B, 512, S), F32), sd((B, 512, S), F32),
                 sd((B, nk, 128, tk), BF16), sd((B, nk, 128, tk), BF16),
                 sd((B, nk, 512, tk), BF16), sd((B, nk, 512, tk), BF16)]
    out_specs = [row(512), row(256), row(256), row(128), row(256), row(128),
                 feat(128), feat(128), feat(128), feat(512), feat(512),
                 tile(128), tile(128), tile(512), tile(512)]
    return pl.pallas_call(
        functools.partial(_proj_prompt_kernel, tm=tm, tk=tk),
        grid=(B, S // tm), in_specs=in_specs, out_specs=out_specs, out_shape=out_shape,
        compiler_params=_params(("parallel", "parallel")), name="proj_prompt",
    )(x, wrm, wt, *tabs, gb, lng, lnb, ws, bsm)


def _proj_sample(x, wrm, wt, tabs, gb, lng, lnb, w00, b0):
    nb = x.shape[0]
    full = lambda a: pl.BlockSpec(a.shape, lambda i: (0,) * a.ndim)
    ins = (x, wrm, wt, *tabs, gb, lng, lnb, w00, b0)
    sd = jax.ShapeDtypeStruct
    out_shape = [sd((nb, 512), BF16), sd((nb, 256), BF16), sd((nb, 256), BF16), sd((nb, 128), F32),
                 sd((nb, 256), BF16), sd((nb, 256), F32),
                 sd((nb, 128), F32), sd((nb, 128), F32), sd((nb, 512), F32),
                 sd((128, nb), F32), sd((128, nb), F32), sd((128, nb), F32),
                 sd((512, nb), F32), sd((512, nb), F32)]
    return pl.pallas_call(
        _proj_sample_kernel, grid=(1,), in_specs=[full(a) for a in ins],
        out_specs=[pl.BlockSpec(s.shape, lambda i: (0, 0)) for s in out_shape], out_shape=out_shape,
        compiler_params=_params(("arbitrary",)), name="proj_sample",
    )(*ins)


def _compress_kernel(x_ref, pe_ref, w1_ref, w2_ref, o_ref):
    h = _dot((x_ref[...] + pe_ref[...]).astype(BF16), w1_ref[...])
    o_ref[...] = _dot(_gelu(h).astype(BF16), w2_ref[...]).astype(o_ref.dtype)


def _compress(x, pe, w1, w2, out_dtype):
    n, f = x.shape
    tb = math.gcd(n, 256)
    return pl.pallas_call(
        _compress_kernel, grid=(n // tb,),
        in_specs=[pl.BlockSpec((tb, f), lambda i: (i, 0)), _const_spec(pe.shape),
                  _const_spec(w1.shape), _const_spec(w2.shape)],
        out_specs=pl.BlockSpec((tb, 128), lambda i: (i, 0)),
        out_shape=jax.ShapeDtypeStruct((n, 128), out_dtype),
        compiler_params=_params(("parallel",)), name="compress",
    )(x, pe, w1, w2)


def _select_topk(score, blk_f, n_blocks, k):
    sel = jnp.zeros(score.shape, F32)
    work = score
    for _ in range(k):
        mx = jnp.max(work, axis=0, keepdims=True)
        idx = jnp.min(jnp.where(work == mx, blk_f, float(n_blocks)), axis=0, keepdims=True)
        pick = blk_f == idx
        sel = jnp.where(pick, 1.0, sel)
        work = jnp.where(pick, NEG_BIG, work)
    return sel


def _select_topk_row(score, k):
    n = score.shape[1]
    mine = jnp.broadcast_to(score, (n, n))
    other = mine.T
    beats = (other > mine) | ((other == mine) & (_iota((n, n), 0) < _iota((n, n), 1)))
    rank = jnp.sum(jnp.where(beats, 1.0, 0.0), axis=0, keepdims=True)
    return jnp.where(rank < k, 1.0, 0.0)


def _masked_softmax(logits, mask):
    lm = jnp.where(mask, logits, NEG_INF)
    mx = jnp.max(lm, axis=1, keepdims=True)
    e = jnp.where(mask, jnp.exp(lm - mx), 0.0)
    return e / jnp.maximum(jnp.sum(e, axis=1, keepdims=True), TINY)


def _cmp_block_end(n_half):
    r = _iota((1, 2 * n_half), 1)
    return jnp.where(r < n_half, 64 * r + 31, 64 * (r - n_half) + 63)


def _nsa_prompt_kernel(qf_ref, g_ref, kvc_ref, sel_ref, win_ref, o_ref, qs_ref, acc_ref, m_ref, l_ref, bias_ref,
                       *, tq, tk, ns, topk):
    assert tq == tk
    i = pl.program_id(1)
    tpos = i * tq + _iota((tq, 1), 0)
    qh = [qf_ref[:, h * 128:(h + 1) * 128] for h in range(N_HEADS)]
    kvc = kvc_ref[...]

    maskc = _cmp_block_end(ns) <= tpos
    oc, impc = [], jnp.zeros((tq, 2 * ns), F32)
    for h in range(N_HEADS):
        pc = _masked_softmax(_dot_nt(qh[h], kvc), maskc)
        oc.append(_dot(pc.astype(BF16), kvc))
        impc = impc + pc
    imp = impc[:, :ns] + impc[:, ns:]

    blk = _iota((ns, tq), 0)
    tpos_row = i * tq + _iota((1, tq), 1)
    cur = tpos_row >> 6
    forced = (blk == 0) | (blk == cur) | (blk == cur - 1)
    valid = (blk << 6) <= tpos_row
    score = jnp.where(forced, FORCE_SCORE, jnp.where(valid, imp.T, -1.0))
    sel_bf = _select_topk(score, blk.astype(F32), ns, topk).T.astype(BF16)

    row, col = _iota((tq, tk), 0), _iota((tq, tk), 1)

    def chosen_in_tile(j):
        expand = (_iota((ns, tk), 0) == j * (tk // SEL_BLOCK) + (_iota((ns, tk), 1) >> 6))
        return _dot(sel_bf, expand.astype(BF16)) > 0.5

    def sel_bias_step(j, carry):
        bias_ref[j] = jnp.where(chosen_in_tile(j), 0.0, NEG_INF).astype(BF16)
        return carry

    lax.fori_loop(0, i, sel_bias_step, 0)
    bias_ref[i] = jnp.where(chosen_in_tile(i) & (col <= row), 0.0, NEG_INF).astype(BF16)

    for h in range(N_HEADS):
        qs_ref[h * tq:(h + 1) * tq, :] = qh[h]

    def attend(kv_ref, j_lo, j_hi, bias_of):
        def scores(j):
            kt = kv_ref[j]
            s = _dot(qs_ref[...], kt).reshape(N_HEADS, tq, tk) + bias_of(j)[None]
            return s.reshape(N_HEADS * tq, tk), kt

        def max_step(j, carry):
            s, _ = scores(j)
            m_ref[...] = jnp.maximum(m_ref[...], jnp.maximum(s[:, :128], s[:, 128:]))
            return carry

        def sum_step(j, carry):
            s, kt = scores(j)
            m = m_ref[...]
            p0, p1 = jnp.exp(s[:, :128] - m), jnp.exp(s[:, 128:] - m)
            l_ref[...] += p0 + p1
            acc_ref[...] += _dot_nt(jnp.concatenate([p0, p1], axis=1).astype(BF16), kt)
            return carry

        m_ref[...] = jnp.full(m_ref.shape, NEG_INF, F32)
        lax.fori_loop(j_lo, j_hi, max_step, 0)
        m_ref[...] = jnp.broadcast_to(jnp.max(m_ref[...], axis=1, keepdims=True), m_ref.shape)
        acc_ref[...] = jnp.zeros(acc_ref.shape, F32)
        l_ref[...] = jnp.zeros(l_ref.shape, F32)
        lax.fori_loop(j_lo, j_hi, sum_step, 0)
        out = acc_ref[...] / jnp.maximum(jnp.sum(l_ref[...], axis=1, keepdims=True), TINY)
        return [out[h * tq:(h + 1) * tq, :] for h in range(N_HEADS)]

    def win_bias(j):
        d = (i - j) * tk + row - col
        return jnp.where((d >= 0) & (d < WINDOW), 0.0, NEG_INF)

    osel = attend(sel_ref, 0, i + 1, lambda j: bias_ref[j].astype(F32))
    ow = attend(win_ref, jnp.maximum(i - WINDOW // tk, 0), i + 1, win_bias)

    g = g_ref[...]
    lane = _iota((tq, 128), 1)
    mixed = [g[:, 3 * h:3 * h + 1] * oc[h] + g[:, 3 * h + 1:3 * h + 2] * osel[h]
             + g[:, 3 * h + 2:3 * h + 3] * ow[h] for h in range(N_HEADS)]
    for k in range(2):
        even = pltpu.roll(mixed[2 * k], 64, 1)
        o_ref[:, k * 128:(k + 1) * 128] = jnp.where(lane < 64, even, mixed[2 * k + 1]).astype(BF16)


def _nsa_prompt(qaf, gates, kvc, selbf, winbf):
    B, S, _ = qaf.shape
    tq, tk = Q_TILE, KV_TILE
    nk = S // tk
    ns = S // SEL_BLOCK
    kern = functools.partial(_nsa_prompt_kernel, tq=tq, tk=tk, ns=ns, topk=min(SEL_TOPK, ns))
    row = lambda n: pl.BlockSpec((None, tq, n), lambda b, i: (b, i, 0))
    whole = lambda a: pl.BlockSpec((None,) + a.shape[1:], lambda b, i: (b,) + (0,) * (a.ndim - 1))
    return pl.pallas_call(
        kern, grid=(B, S // tq),
        in_specs=[row(512), row(128), whole(kvc), whole(selbf), whole(winbf)],
        out_specs=row(256), out_shape=jax.ShapeDtypeStruct((B, S, 256), BF16),
        scratch_shapes=[pltpu.VMEM((N_HEADS * tq, 128), BF16), pltpu.VMEM((N_HEADS * tq, 128), F32),
                        pltpu.VMEM((N_HEADS * tq, 128), F32), pltpu.VMEM((N_HEADS * tq, 128), F32),
                        pltpu.VMEM((nk, tq, tk), BF16)],
        compiler_params=_params(("parallel", "arbitrary")), name="nsa_prompt",
    )(qaf, gates, kvc, selbf, winbf)


def _sb_prompt_kernel(q_ref, kv_ref, o_ref, qs_ref, acc_ref, car_ref, *, tq, tk):
    assert tq == tk
    i = pl.program_id(1)
    rows = N_HEADS * tq
    q = q_ref[...].astype(F32)
    lane_head = _iota((1, GROUP_WIDTH), 1) >> 6
    for h in range(N_HEADS):
        qs_ref[h * tq:(h + 1) * tq, :] = jnp.where(lane_head == h, q, 0.0).astype(BF16)
    suffix = _suffix_matrix(tk)
    acc_ref[...] = jnp.zeros(acc_ref.shape, F32)
    car_ref[...] = jnp.zeros(car_ref.shape, F32)

    def step(j, strict):
        z = _dot(qs_ref[...], kv_ref[j, 0:256, :])
        log_keep = -_softplus(z)
        if strict is not None:
            log_keep = jnp.where(strict[None], log_keep.reshape(N_HEADS, tq, tk), 0.0).reshape(rows, tk)
        hi, lo = _split_bf16(log_keep)
        incl = _dot(hi, suffix) + _dot(lo, suffix)
        car = car_ref[...]
        a = jnp.exp(z + incl + jnp.concatenate([car, car], axis=1))
        if strict is not None:
            a = jnp.where(strict[None], a.reshape(N_HEADS, tq, tk), 0.0).reshape(rows, tk)
        ab = a.astype(BF16)
        for k in range(2):
            pair = slice(2 * k * tq, (2 * k + 2) * tq)
            acc_ref[pair, :] += _dot_nt(ab[pair, :], kv_ref[j, 256 + k * 128:384 + k * 128, :])
        car_ref[...] = car + jnp.broadcast_to(incl[:, 0:1], (rows, 128))

    step(i, _iota((tq, tk), 1) < _iota((tq, tk), 0))

    def below(jj, carry):
        step(i - 1 - jj, None)
        return carry

    lax.fori_loop(0, i, below, 0)
    lane = _iota((tq, 128), 1)
    for k in range(2):
        even, odd = acc_ref[2 * k * tq:(2 * k + 1) * tq, :], acc_ref[(2 * k + 1) * tq:(2 * k + 2) * tq, :]
        o_ref[:, k * 128:(k + 1) * 128] = jnp.where(lane < 64, even, odd).astype(BF16)


def _sb_prompt(qb, sbbf):
    B, S, _ = qb.shape
    tq, tk = Q_TILE, KV_TILE
    row = lambda n: pl.BlockSpec((None, tq, n), lambda b, i: (b, i, 0))
    return pl.pallas_call(
        functools.partial(_sb_prompt_kernel, tq=tq, tk=tk), grid=(B, S // tq),
        in_specs=[row(256), pl.BlockSpec((None,) + sbbf.shape[1:], lambda b, i: (b, 0, 0, 0))],
        out_specs=row(256), out_shape=jax.ShapeDtypeStruct((B, S, 256), BF16),
        scratch_shapes=[pltpu.VMEM((N_HEADS * tq, GROUP_WIDTH), BF16), pltpu.VMEM((N_HEADS * tq, 128), F32),
                        pltpu.VMEM((N_HEADS * tq, 128), F32)],
        compiler_params=_params(("parallel", "arbitrary")), name="sb_prompt",
    )(qb, sbbf)


def _diff_lambda(dl_ref, lam_init):
    lq = dl_ref[...]
    s1 = jnp.sum(lq[0:1, :] * lq[1:2, :], axis=1, keepdims=True)
    s2 = jnp.sum(lq[2:3, :] * lq[3:4, :], axis=1, keepdims=True)
    return jnp.exp(s1) - jnp.exp(s2) + lam_init


def _head_rms_norm(out, lane_head, ng, lam_init):
    sq = out * out
    rs = jnp.zeros(out.shape, F32)
    for h in range(N_HEADS):
        ms = jnp.sum(jnp.where(lane_head == h, sq, 0.0), axis=1, keepdims=True) * (1.0 / HEAD_DIM)
        rs = jnp.where(lane_head == h, lax.rsqrt(ms + LN_EPS), rs)
    return out * rs * ng * (1.0 - lam_init)


def _diff_prompt_kernel(li_ref, dl_ref, ng_ref, q_ref, kv_ref, o_ref, qs_ref, acc_ref, m_ref, l_ref, *, tq, tk):
    assert tq == tk
    i = pl.program_id(1)
    lam_init = li_ref[0]
    lam = _diff_lambda(dl_ref, lam_init)
    q = q_ref[...].astype(F32)
    lane = _iota((1, GROUP_WIDTH), 1)
    lane_head, lane_half = lane >> 6, (lane >> 5) & 1
    n_soft = 2 * N_HEADS
    rows = n_soft * tq
    for a in range(n_soft):
        qa = jnp.where((lane_head == a // 2) & (lane_half == a % 2), q, 0.0)
        qs_ref[a * tq:(a + 1) * tq, :] = qa.astype(BF16)
    causal_bias = jnp.where(_iota((tq, tk), 1) <= _iota((tq, tk), 0), 0.0, NEG_INF)

    def scores(j, bias):
        s = _dot(qs_ref[...], kv_ref[j, 0:256, :])
        if bias is not None:
            s = (s.reshape(n_soft, tq, tk) + bias[None]).reshape(rows, tk)
        return s

    def max_step(j, bias):
        s = scores(j, bias)
        m_ref[...] = jnp.maximum(m_ref[...], jnp.maximum(s[:, :128], s[:, 128:]))

    def sum_step(j, bias):
        s = scores(j, bias)
        m = m_ref[...]
        p0, p1 = jnp.exp(s[:, :128] - m), jnp.exp(s[:, 128:] - m)
        l_ref[...] += p0 + p1
        pb = jnp.concatenate([p0, p1], axis=1).astype(BF16)
        for k in range(2):
            part = slice(k * (rows // 2), (k + 1) * (rows // 2))
            acc_ref[part, :] += _dot_nt(pb[part, :], kv_ref[j, 256 + k * 128:384 + k * 128, :])

    def below(step):
        def body(j, carry):
            step(j, None)
            return carry
        lax.fori_loop(0, i, body, 0)

    m_ref[...] = jnp.full(m_ref.shape, NEG_INF, F32)
    below(max_step)
    max_step(i, causal_bias)
    m_ref[...] = jnp.broadcast_to(jnp.max(m_ref[...], axis=1, keepdims=True), m_ref.shape)
    acc_ref[...] = jnp.zeros(acc_ref.shape, F32)
    l_ref[...] = jnp.zeros(l_ref.shape, F32)
    below(sum_step)
    sum_step(i, causal_bias)

    lane128 = _iota((tq, 128), 1)
    ng = ng_ref[...]
    blk = lambda ref, a: ref[a * tq:(a + 1) * tq, :]
    for k in range(2):
        o = []
        for h in (2 * k, 2 * k + 1):
            l1 = jnp.maximum(jnp.sum(blk(l_ref, 2 * h), axis=1, keepdims=True), TINY)
            l2 = jnp.maximum(jnp.sum(blk(l_ref, 2 * h + 1), axis=1, keepdims=True), TINY)
            o.append(blk(acc_ref, 2 * h) / l1 - lam * (blk(acc_ref, 2 * h + 1) / l2))
        low = lane128 < HEAD_DIM
        out = jnp.where(low, o[0], o[1])
        sq = out * out
        ms_lo = jnp.sum(jnp.where(low, sq, 0.0), axis=1, keepdims=True)
        ms_hi = jnp.sum(jnp.where(low, 0.0, sq), axis=1, keepdims=True)
        rs = lax.rsqrt(jnp.where(low, ms_lo, ms_hi) * (1.0 / HEAD_DIM) + LN_EPS)
        o_ref[:, k * 128:(k + 1) * 128] = (out * rs * ng[:, k * 128:(k + 1) * 128] * (1.0 - lam_init)).astype(BF16)


def _diff_prompt(lam_init, dl, ng, qd, diffbf):
    B, S, _ = qd.shape
    tq, tk = Q_TILE, KV_TILE
    row = lambda n: pl.BlockSpec((None, tq, n), lambda b, i: (b, i, 0))
    return pl.pallas_call(
        functools.partial(_diff_prompt_kernel, tq=tq, tk=tk), grid=(B, S // tq),
        in_specs=[pl.BlockSpec(memory_space=pltpu.SMEM), _const_spec(dl.shape), _const_spec(ng.shape),
                  row(256), pl.BlockSpec((None,) + diffbf.shape[1:], lambda b, i: (b, 0, 0, 0))],
        out_specs=row(256), out_shape=jax.ShapeDtypeStruct((B, S, 256), BF16),
        scratch_shapes=[pltpu.VMEM((2 * N_HEADS * tq, GROUP_WIDTH), BF16), pltpu.VMEM((2 * N_HEADS * tq, 128), F32),
                        pltpu.VMEM((2 * N_HEADS * tq, 128), F32), pltpu.VMEM((2 * N_HEADS * tq, 128), F32)],
        compiler_params=_params(("parallel", "arbitrary")), name="diff_prompt",
    )(lam_init, dl, ng, qd, diffbf)


def _round_bf16(x):
    return x.astype(BF16).astype(F32)


def _softmax_with_self(s, ok, s_self):
    sm = jnp.where(ok, s, NEG_INF)
    mx = jnp.maximum(jnp.max(sm, axis=1, keepdims=True), s_self)
    e = jnp.where(ok, jnp.exp(sm - mx), 0.0)
    e_self = jnp.exp(s_self - mx)
    inv = 1.0 / (jnp.sum(e, axis=1, keepdims=True) + e_self)
    return e * inv, e_self * inv


def _nsa_sample_kernel(pt_ref, qh_ref, gs_ref, snew_ref, wnew_ref, win_ref, *rest,
                       n_pages, past, half, topk):
    del pt_ref
    cmp_refs, sel_refs = rest[:n_pages], rest[n_pages:2 * n_pages]
    o_ref, kvc_ref = rest[2 * n_pages], rest[2 * n_pages + 1]
    qh = qh_ref[...]
    qf = qh.astype(F32)

    kvc_ref[...] = jnp.zeros(kvc_ref.shape, F32)
    for p in range(n_pages):
        piece = cmp_refs[p][...]
        kvc_ref[2 * p:2 * p + 2, :] = piece[0:2, :]
        kvc_ref[half + 2 * p:half + 2 * p + 2, :] = piece[2:4, :]
    kvc = kvc_ref[...].astype(BF16)
    maskc = _cmp_block_end(half) <= past
    pc = _masked_softmax(_dot_nt(qh, kvc), maskc)
    oc = _dot(pc.astype(BF16), kvc)
    head_row = _iota((8, 1), 0) < N_HEADS
    impc = jnp.sum(jnp.where(head_row, pc, 0.0), axis=0, keepdims=True)
    imp = impc[:, :half] + impc[:, half:]

    n_sel = past // SEL_BLOCK + 1
    blk = _iota((1, half), 1)
    cur = past // SEL_BLOCK
    forced = (blk == 0) | (blk == cur) | (blk == cur - 1)
    valid = blk * SEL_BLOCK <= past
    score = jnp.where(blk < n_sel, jnp.where(forced, FORCE_SCORE, jnp.where(valid, imp, -1.0)), NEG_BIG)
    sel8 = jnp.broadcast_to(_select_topk_row(score, topk), (8, half)).astype(BF16)
    expand = _iota((half, past), 0) == (_iota((half, past), 1) >> 6)
    chosen = _dot(sel8, expand.astype(BF16)) > 0.5

    def branch(pages, new_row, ok):
        new_b = _round_bf16(new_row)
        s = jnp.concatenate([_dot(qh, t) for t in pages], axis=1)
        s_self = jnp.sum(qf * new_b, axis=1, keepdims=True)
        p, p_self = _softmax_with_self(s, ok, s_self)
        out = p_self * new_b
        off = 0
        for t in pages:
            n = t.shape[1]
            out = out + _dot_nt(p[:, off:off + n].astype(BF16), t)
            off += n
        return out

    osel = branch([r[...].astype(BF16) for r in sel_refs], snew_ref[...], chosen)
    w_past = win_ref.shape[1]
    in_window = _iota((1, w_past), 1) >= (w_past - WINDOW + 1)
    ow = branch([win_ref[...].astype(BF16)], wnew_ref[...], in_window)
    g = gs_ref[...]
    o_ref[...] = g[:, 0:1] * oc + g[:, 1:2] * osel + g[:, 2:3] * ow


def _sb_sample_kernel(pt_ref, q_ref, *rest, n_pages, past, cb):
    del pt_ref
    pages, o_ref = rest[:n_pages], rest[n_pages]
    q8 = jnp.broadcast_to(q_ref[...].astype(F32), (8, GROUP_WIDTH))
    row_head = _iota((8, GROUP_WIDTH), 0) == (_iota((8, GROUP_WIDTH), 1) >> 6)
    qm = jnp.where(row_head, q8, 0.0).astype(BF16)
    z = jnp.concatenate([_dot(qm, pages[p][0].astype(BF16)) for p in range(n_pages)], axis=1)
    log_keep = -_softplus(z)
    suffix = _suffix_matrix(cb)
    carry = jnp.zeros((8, 1), F32)
    expo = [None] * (past // cb)
    for bi in reversed(range(past // cb)):
        cols = slice(bi * cb, (bi + 1) * cb)
        hi, lo = _split_bf16(log_keep[:, cols])
        incl = _dot(hi, suffix) + _dot(lo, suffix)
        expo[bi] = z[:, cols] + incl + carry
        carry = carry + incl[:, 0:1]
    a = jnp.exp(jnp.concatenate(expo, axis=1))
    out = jnp.zeros((8, GROUP_WIDTH), F32)
    for p in range(n_pages):
        out = out + _dot_nt(a[:, p * 128:(p + 1) * 128].astype(BF16), pages[p][1].astype(BF16))
    o_ref[...] = jnp.sum(jnp.where(row_head, out, 0.0), axis=0, keepdims=True).astype(BF16)


def _diff_sample_kernel(pt_ref, li_ref, dl_ref, ng_ref, q_ref, dnew_ref, *rest, n_pages):
    del pt_ref
    pages, o_ref = rest[:n_pages], rest[n_pages]
    lam_init = li_ref[0]
    lam = _diff_lambda(dl_ref, lam_init)
    q8 = jnp.broadcast_to(q_ref[...].astype(F32), (8, GROUP_WIDTH))
    lane = _iota((8, GROUP_WIDTH), 1)
    row_head = _iota((8, GROUP_WIDTH), 0) == (lane >> 6)
    lane_half = (lane >> 5) & 1
    new_k = _round_bf16(dnew_ref[:, 0:256])
    new_v = _round_bf16(dnew_ref[:, 256:512])
    kts = [pages[p][0].astype(BF16) for p in range(n_pages)]
    ps, ps_self = [], []
    for m in range(2):
        qm = jnp.where(row_head & (lane_half == m), q8, 0.0)
        s = jnp.concatenate([_dot(qm.astype(BF16), kt) for kt in kts], axis=1)
        s_self = jnp.sum(qm * new_k, axis=1, keepdims=True)
        p, p_self = _softmax_with_self(s, jnp.full(s.shape, True), s_self)
        ps.append(p)
        ps_self.append(p_self)
    p = ps[0] - lam * ps[1]
    out = (ps_self[0] - lam * ps_self[1]) * new_v
    for pg in range(n_pages):
        out = out + _dot_nt(p[:, pg * 128:(pg + 1) * 128].astype(BF16), pages[pg][1].astype(BF16))
    o4 = jnp.sum(jnp.where(row_head, out, 0.0), axis=0, keepdims=True)
    o_ref[...] = _head_rms_norm(o4, _iota((1, GROUP_WIDTH), 1) >> 6, ng_ref[...], lam_init).astype(BF16)


def _page_spec(block, layer, p):
    nz = (0,) * (len(block) - 2)
    return pl.BlockSpec(block, lambda b, pt: (layer, pt[b, p]) + nz)


def _nsa_sample(layer, page_table, qh, gs, snew, wnew, cmp_pool, sel_cache, win_cache):
    nb, n_pages = page_table.shape
    past = n_pages * 128
    half = max(128, -(-(past // SEL_BLOCK + 1) // 128) * 128)
    n_sel = past // SEL_BLOCK + 1
    kern = functools.partial(_nsa_sample_kernel, n_pages=n_pages, past=past, half=half,
                             topk=min(SEL_TOPK, n_sel))
    per_seq = lambda a: pl.BlockSpec((None,) + a.shape[1:], lambda b, pt: (b,) + (0,) * (a.ndim - 1))
    in_specs = [per_seq(qh), per_seq(gs), per_seq(snew), per_seq(wnew),
                pl.BlockSpec((None, None) + win_cache.shape[2:], lambda b, pt: (layer, b, 0, 0))]
    in_specs += [pl.BlockSpec((None, 4, 128), functools.partial(lambda b, pt, p: (pt[b, p], 0, 0), p=p))
                 for p in range(n_pages)]
    in_specs += [_page_spec((None, None, 128, 128), layer, p) for p in range(n_pages)]
    grid_spec = pltpu.PrefetchScalarGridSpec(
        num_scalar_prefetch=1, grid=(nb,), in_specs=in_specs,
        out_specs=pl.BlockSpec((None, 8, 128), lambda b, pt: (b, 0, 0)),
        scratch_shapes=[pltpu.VMEM((2 * half, 128), F32)])
    return pl.pallas_call(
        kern, grid_spec=grid_spec, out_shape=jax.ShapeDtypeStruct((nb, 8, 128), F32),
        compiler_params=_params(("arbitrary",)), name="nsa_sample",
    )(page_table, qh, gs, snew, wnew, win_cache, *([cmp_pool] * n_pages), *([sel_cache] * n_pages))


def _sb_sample(layer, page_table, qb, cache):
    nb, n_pages = page_table.shape
    past = n_pages * 128
    cb = math.gcd(past, 256)
    in_specs = [pl.BlockSpec((None, 1, 256), lambda b, pt: (b, 0, 0))]
    in_specs += [_page_spec((None, None, 2, 256, 128), layer, p) for p in range(n_pages)]
    grid_spec = pltpu.PrefetchScalarGridSpec(
        num_scalar_prefetch=1, grid=(nb,), in_specs=in_specs,
        out_specs=pl.BlockSpec((None, 1, 256), lambda b, pt: (b, 0, 0)))
    return pl.pallas_call(
        functools.partial(_sb_sample_kernel, n_pages=n_pages, past=past, cb=cb), grid_spec=grid_spec,
        out_shape=jax.ShapeDtypeStruct((nb, 1, 256), BF16),
        compiler_params=_params(("arbitrary",)), name="sb_sample",
    )(page_table, qb, *([cache] * n_pages))


def _diff_sample(layer, page_table, lam_init, dl, ng, qd, dnew, cache):
    nb, n_pages = page_table.shape
    const = lambda a: pl.BlockSpec(a.shape, lambda b, pt: (0,) * a.ndim)
    in_specs = [pl.BlockSpec(memory_space=pltpu.SMEM), const(dl), const(ng),
                pl.BlockSpec((None, 1, 256), lambda b, pt: (b, 0, 0)),
                pl.BlockSpec((None, 1, 512), lambda b, pt: (b, 0, 0))]
    in_specs += [_page_spec((None, None, 2, 256, 128), layer, p) for p in range(n_pages)]
    grid_spec = pltpu.PrefetchScalarGridSpec(
        num_scalar_prefetch=1, grid=(nb,), in_specs=in_specs,
        out_specs=pl.BlockSpec((None, 1, 256), lambda b, pt: (b, 0, 0)))
    return pl.pallas_call(
        functools.partial(_diff_sample_kernel, n_pages=n_pages), grid_spec=grid_spec,
        out_shape=jax.ShapeDtypeStruct((nb, 1, 256), BF16),
        compiler_params=_params(("arbitrary",)), name="diff_sample",
    )(page_table, lam_init, dl, ng, qd, dnew, *([cache] * n_pages))


def _tail_kernel(x_ref, oa_ref, ob_ref, oc_ref, od_ref, wo_ref, g1_ref, b1_ref,
                 wg_ref, wu_ref, wd_ref, g2_ref, b2_ref, y_ref, *, alpha):
    mix = (_dot(oa_ref[...], wo_ref[0:256, :]) + _dot(ob_ref[...], wo_ref[256:512, :])
           + _dot(oc_ref[...], wo_ref[512:768, :]) + _dot(od_ref[...], wo_ref[768:1024, :]))
    x1 = _layer_norm(alpha * x_ref[...] + mix, g1_ref[...], b1_ref[...])
    x1b = x1.astype(BF16)
    hg = _dot(x1b, wg_ref[...])
    hu = _dot(x1b, wu_ref[...])
    act = (hg * _sigmoid(hg) * hu).astype(BF16)
    y_ref[...] = _layer_norm(alpha * x1 + _dot(act, wd_ref[...]), g2_ref[...], b2_ref[...])


def _tail(x, oa, ob, oc, od, wo, g1, b1, wg, wu, wd, g2, b2, alpha):
    m = x.shape[0]
    tm = math.gcd(m, TOKEN_TILE)
    row = lambda n: pl.BlockSpec((tm, n), lambda i: (i, 0))
    consts = (wo, g1, b1, wg, wu, wd, g2, b2)
    return pl.pallas_call(
        functools.partial(_tail_kernel, alpha=alpha), grid=(m // tm,),
        in_specs=[row(D_MODEL), row(256), row(256), row(256), row(256)] + [_const_spec(c.shape) for c in consts],
        out_specs=row(D_MODEL), out_shape=jax.ShapeDtypeStruct((m, D_MODEL), F32),
        compiler_params=_params(("parallel",)), name="tail",
    )(x, oa, ob, oc, od, *consts)


def _rot_partner(w, group, half):
    k = w.shape[0]
    wr = w.reshape(k, -1, group)
    x1, x2 = wr[..., :half], wr[..., half:2 * half]
    return jnp.concatenate([-x2, x1, jnp.zeros_like(wr[..., 2 * half:])], axis=-1).reshape(k, -1)


def _pad_heads(w):
    k = w.shape[0]
    wr = w.reshape(k, -1, HEAD_DIM)
    return jnp.concatenate([wr, jnp.zeros_like(wr)], axis=-1).reshape(k, -1)


def _layer_weights(w, sample):
    off = [0]
    for n in (256, 64, 64, 64, 64, 64, 64, 12, 256, 256, 256, 256, 256, 256, 256, 256):
        off.append(off[-1] + n)
    col = lambda i: w[:, off[i]:off[i + 1]]
    qa, kc, vc, ks, vs, kw, vw, ga, qb, kb, vb, qd, kd, vd, u, v = [col(i) for i in range(16)]
    zeros64 = jnp.zeros((w.shape[0], 64), F32)
    qa = qa * (HEAD_DIM ** -0.5)
    qb = qb * (HEAD_DIM ** -0.5)
    qd = qd * (DIFF_QK ** -0.5)
    sel = jnp.concatenate([ks, vs], 1)
    sel_sw = jnp.concatenate([_rot_partner(ks, HEAD_DIM, ROT_HALF), zeros64], 1)
    win = jnp.concatenate([kw, vw], 1)
    win_sw = jnp.concatenate([_rot_partner(kw, HEAD_DIM, ROT_HALF), zeros64], 1)
    kd_sw = _rot_partner(kd, DIFF_QK, DIFF_ROT_HALF)
    rm = [_pad_heads(qa), _pad_heads(_rot_partner(qa, HEAD_DIM, ROT_HALF)), qb, qd,
          _rot_partner(qd, DIFF_QK, DIFF_ROT_HALF), u, v,
          jnp.concatenate([ga, jnp.zeros((w.shape[0], 128 - ga.shape[1]), F32)], 1),
          jnp.concatenate([kc, vc], 1)]
    if sample:
        rm += [sel, sel_sw, win, win_sw, kd, vd, kd_sw]
    tr = [jnp.concatenate([kc, vc], 1), sel, sel_sw, win, win_sw, kb, vb, kd, vd, kd_sw]
    return (jnp.concatenate(rm, 1).astype(BF16), jnp.concatenate(tr, 1).T.astype(BF16))


def _rope_tables(pos):
    pos = pos.astype(F32)[:, None]
    n = pos.shape[0]

    def table(half, plain, reps):
        freq = ROPE_THETA ** (-jnp.arange(half, dtype=F32) / half)
        ang = pos * freq[None, :]
        c = jnp.concatenate([jnp.cos(ang), jnp.cos(ang), jnp.ones((n, plain), F32)], 1)
        s = jnp.concatenate([jnp.sin(ang), jnp.sin(ang), jnp.zeros((n, plain), F32)], 1)
        return jnp.tile(c, (1, reps)), jnp.tile(s, (1, reps))

    ca, sa = table(ROT_HALF, 128 - 2 * ROT_HALF, 1)
    cd, sd = table(DIFF_ROT_HALF, DIFF_QK - 2 * DIFF_ROT_HALF, 4)
    return (ca, sa, cd, sd, ca.T, sa.T, cd.T, sd.T)


def _compress_weights(pe, w1, w2):
    z = jnp.zeros((CMP_BLOCK, HEAD_DIM, HEAD_DIM), F32)
    w1k = w1[0].reshape(CMP_BLOCK, HEAD_DIM, HEAD_DIM)
    w1v = w1[1].reshape(CMP_BLOCK, HEAD_DIM, HEAD_DIM)
    w1c = jnp.concatenate([jnp.concatenate([w1k, z], 2), jnp.concatenate([z, w1v], 2)], 1)
    z2 = jnp.zeros((HEAD_DIM, HEAD_DIM), F32)
    w2c = jnp.concatenate([jnp.concatenate([w2[0], z2], 1), jnp.concatenate([z2, w2[1]], 1)], 0)
    pe_flat = jnp.concatenate([pe[0], pe[1]], 1).reshape(1, CMP_BLOCK * 128)
    return pe_flat, w1c.reshape(CMP_BLOCK * 128, 128).astype(BF16), w2c.astype(BF16)


def _feature_major(x, tail_shape):
    lead = x.shape[:-2]
    n = x.shape[-1]
    y = x.reshape(lead + tail_shape + (n,))
    nd = y.ndim
    return jnp.transpose(y, tuple(range(len(lead))) + (nd - 1,) + tuple(range(len(lead), nd - 1)))


def kernel(x_prompt, x_sample, cache_nsa_cmp, cache_nsa_sel, cache_nsa_win, cache_sb, cache_diff,
           page_table, w_in, gate_b, cmp_pe, cmp_w1, cmp_w2, diff_lam, diff_norm_g, gmlp_ln_g,
           gmlp_ln_b, gmlp_ws, gmlp_bs, w_out, ln1_g, ln1_b, w_gate, w_up, w_down, ln2_g, ln2_b):
    B, S, _ = x_prompt.shape
    nb, T, _ = x_sample.shape
    depth = w_in.shape[0]
    n_pool = cache_sb.shape[1]
    n_pages = page_table.shape[1]
    past = n_pages * cache_sb.shape[2]
    w_past = cache_nsa_win.shape[2]
    assert T == 1 and cache_sb.shape[2] == 128 and S % Q_TILE == 0 and past % SEL_BLOCK == 0
    alpha = (2 * depth) ** 0.25

    to_pages = lambda c: jnp.transpose(c, (0, 1, 3, 4, 5, 2))
    sel_cache = to_pages(cache_nsa_sel).reshape(depth, n_pool, 128, 128)
    win_cache = to_pages(cache_nsa_win).reshape(depth, nb, 128, w_past)
    sb_cache = to_pages(cache_sb).reshape(depth, n_pool, 2, 256, 128)
    diff_cache = to_pages(cache_diff).reshape(depth, n_pool, 2, 256, 128)

    tabs_p = _rope_tables(jnp.arange(S, dtype=jnp.int32))
    tabs_s = _rope_tables(jnp.full((nb,), past, jnp.int32))
    row = lambda a: a.reshape(1, -1)

    xp, xs = x_prompt, x_sample.reshape(nb, D_MODEL)
    outs = [[] for _ in range(11)]
    for l in range(depth):
        lam_init = jnp.full((1,), 0.8 - 0.6 * math.exp(-0.3 * l), F32)
        wrm_p, wt = _layer_weights(w_in[l], sample=False)
        wrm_s, _ = _layer_weights(w_in[l], sample=True)
        gb = jnp.concatenate([gate_b[l], jnp.zeros((128 - gate_b.shape[1],), F32)]).reshape(1, 128)
        lng, lnb = row(gmlp_ln_g[l]), row(gmlp_ln_b[l])
        bsm = jnp.repeat(gmlp_bs[l].T, HEAD_DIM, axis=1)
        w00 = row(jnp.repeat(gmlp_ws[l][:, 0, 0], HEAD_DIM))
        b0 = row(jnp.repeat(gmlp_bs[l][:, 0], HEAD_DIM))
        pe_flat, w1c, w2c = _compress_weights(cmp_pe[l], cmp_w1[l], cmp_w2[l])
        ng = row(jnp.tile(diff_norm_g[l], N_HEADS))
        tail_w = (w_out[l].astype(BF16), row(ln1_g[l]), row(ln1_b[l]), w_gate[l].astype(BF16),
                  w_up[l].astype(BF16), w_down[l].astype(BF16), row(ln2_g[l]), row(ln2_b[l]))

        (qaf, qb, qd, gates, od, cmp_rm, cmp_t, sel_t, win_t, sb_t, diff_t,
         selbf, winbf, sbbf, diffbf) = _proj_prompt(xp, wrm_p, wt, tabs_p, gb, lng, lnb, gmlp_ws[l], bsm)
        blocks = cmp_rm.reshape(B, S // SEL_BLOCK, 2, CMP_BLOCK * 128)
        blocks = jnp.transpose(blocks, (0, 2, 1, 3)).reshape(B * (S // CMP_BLOCK), CMP_BLOCK * 128)
        kvc = _compress(blocks, pe_flat, w1c, w2c, BF16).reshape(B, S // CMP_BLOCK, 128)
        oa = _nsa_prompt(qaf, gates, kvc, selbf, winbf)
        ob = _sb_prompt(qb, sbbf)
        oc = _diff_prompt(lam_init, diff_lam[l], ng, qd, diffbf)
        flat = lambda a: a.reshape(B * S, a.shape[-1])
        xp = _tail(flat(xp), flat(oa), flat(ob), flat(oc), flat(od), *tail_w, alpha).reshape(B, S, D_MODEL)
        outs[0].append(_feature_major(cmp_t, (2, 1, HEAD_DIM)))
        outs[2].append(_feature_major(sel_t, (2, 1, HEAD_DIM)))
        outs[4].append(_feature_major(win_t[:, :, S - min(WINDOW, S):], (2, 1, HEAD_DIM)))
        outs[6].append(_feature_major(sb_t, (2, N_HEADS, HEAD_DIM)))
        outs[8].append(_feature_major(diff_t, (2, N_HEADS, HEAD_DIM)))

        (qaf, qb, qd, gates, od, vn, sel_rm, win_rm, diff_rm,
         cmp_t, sel_t, win_t, sb_t, diff_t) = _proj_sample(xs, wrm_s, wt, tabs_s, gb, lng, lnb, w00, b0)
        pool = cache_nsa_cmp[l].reshape(n_pool * 4, CMP_BLOCK * 128)
        cmp_pool = _compress(pool, pe_flat, w1c, w2c, F32).reshape(n_pool, 4, 128)[:, jnp.array([0, 2, 1, 3])]
        qh = jnp.pad(qaf.reshape(nb, N_HEADS, 128), ((0, 0), (0, 8 - N_HEADS), (0, 0)))
        gs = jnp.pad(gates[:, :3 * N_HEADS].reshape(nb, N_HEADS, 3), ((0, 0), (0, 8 - N_HEADS), (0, 125)))
        oa = _nsa_sample(l, page_table, qh, gs, sel_rm.reshape(nb, 1, 128), win_rm.reshape(nb, 1, 128),
                         cmp_pool, sel_cache, win_cache)
        oa = oa[:, :N_HEADS, HEAD_DIM:].reshape(nb, GROUP_WIDTH).astype(BF16)
        ob = _sb_sample(l, page_table, qb.reshape(nb, 1, 256), sb_cache).reshape(nb, 256)
        oc = _diff_sample(l, page_table, lam_init, diff_lam[l], ng, qd.reshape(nb, 1, 256),
                          diff_rm.reshape(nb, 1, 512), diff_cache).reshape(nb, 256)
        xs = _tail(xs, oa, ob, oc, od, *tail_w, alpha)
        new_row = lambda t, shape: _feature_major(t, shape)[:, None]
        outs[1].append(new_row(cmp_t, (2, 1, HEAD_DIM)))
        outs[3].append(new_row(sel_t, (2, 1, HEAD_DIM)))
        outs[5].append(jnp.concatenate([cache_nsa_win[l][:, T:], new_row(win_t, (2, 1, HEAD_DIM))], axis=1))
        outs[7].append(new_row(sb_t, (2, N_HEADS, HEAD_DIM)))
        outs[9].append(new_row(diff_t, (2, N_HEADS, HEAD_DIM)))
        outs[10].append(vn.reshape(nb, 1, GROUP_WIDTH))

    stacked = [jnp.stack(o) for o in outs]
    return (xp, xs.reshape(nb, 1, D_MODEL), *stacked)
```

```python
import functools
import math

import jax
import jax.numpy as jnp
from jax import lax
from jax.experimental import pallas as pl
from jax.experimental.pallas import tpu as pltpu

D_MODEL = 1024
HEAD_DIM = 64
N_HEADS = 4
GROUP_WIDTH = N_HEADS * HEAD_DIM
CMP_BLOCK = 32
SEL_BLOCK = 64
SEL_TOPK = 16
WINDOW = 512
FORCE_SCORE = 1.0e4
DIFF_QK = HEAD_DIM // 2
CHUNK = 128
ROPE_THETA = 500000.0
ROT_HALF = HEAD_DIM // 8
DIFF_ROT_HALF = DIFF_QK // 8
LN_EPS = 1e-5
NEG_INF = -1e30
TINY = 1e-30
NEG_BIG = -3.0e38
SB_UNDERFLOW = -110.0

LANES = 128
VMEM_LIMIT_BYTES = 56 * 1024 * 1024

TOKEN_TILE = 256
Q_TILE = 256
BIG_Q_TILE = 512
KV_TILE = 256

F32 = jnp.float32
BF16 = jnp.bfloat16

RM = dict(qa=0, qa_sw=512, qb=1024, qd=1280, qd_sw=1536, u=1792, v=2048, gate=2304, cmp=2432,
          sel=2560, sel_sw=2688, win=2816, win_sw=2944, diff=3072, kd_sw=3584)
RM_PROMPT_COLS = 2560
RM_SAMPLE_COLS = 3840
TR = dict(cmp=0, sel=128, sel_sw=256, win=384, win_sw=512, sb=640, diff=1152, kd_sw=1664)
TR_ROWS = 1920


def _dot(a, b):
    return jnp.dot(a, b, preferred_element_type=F32)


def _dot_nt(a, b):
    return lax.dot_general(a, b, (((1,), (1,)), ((), ())), preferred_element_type=F32)


def _iota(shape, dim):
    return lax.broadcasted_iota(jnp.int32, shape, dim)


def _gelu(x):
    return 0.5 * x * (1.0 + jnp.tanh(0.7978845608028654 * (x + 0.044715 * (x * x * x))))


def _sigmoid(x):
    return 1.0 / (1.0 + jnp.exp(-x))


def _layer_norm(x, g, b):
    mu = jnp.mean(x, axis=-1, keepdims=True)
    xc = x - mu
    var = jnp.mean(xc * xc, axis=-1, keepdims=True)
    return xc * lax.rsqrt(var + LN_EPS) * g + b


def _softplus(z):
    return jnp.maximum(z, 0.0) + jnp.log(1.0 + jnp.exp(-jnp.abs(z)))


def _split_bf16(x):
    hi = x.astype(BF16)
    lo = (x - hi.astype(F32)).astype(BF16)
    return hi, lo


def _suffix_matrix(n):
    return (_iota((n, n), 0) >= _iota((n, n), 1)).astype(BF16)


def _params(sem):
    return pltpu.CompilerParams(dimension_semantics=sem, vmem_limit_bytes=VMEM_LIMIT_BYTES)


def _const_spec(shape):
    zeros = (0,) * len(shape)
    return pl.BlockSpec(shape, lambda *_: zeros, pipeline_mode=pl.Buffered(1))


def _proj_common(xb, wrm_ref, cq_ref, sq_ref, cd_ref, sd_ref, gb_ref, lng_ref, lnb_ref,
                 qaf_ref, qb_ref, qd_ref, g_ref):
    def rm(name, k=0, width=128):
        a = RM[name] + k * 128
        return _dot(xb, wrm_ref[:, a:a + width])

    cq, sq = cq_ref[...], sq_ref[...]
    for h in range(N_HEADS):
        q = rm("qa", h) * cq + rm("qa_sw", h) * sq
        qaf_ref[:, h * 128:(h + 1) * 128] = q.astype(BF16)
    qb_ref[...] = rm("qb", 0, 256).astype(BF16)
    cd, sd = cd_ref[...], sd_ref[...]
    for k in range(2):
        q = rm("qd", k) * cd + rm("qd_sw", k) * sd
        qd_ref[:, k * 128:(k + 1) * 128] = q.astype(BF16)
    g_ref[...] = _sigmoid(rm("gate") + gb_ref[...])
    u = _gelu(rm("u", 0, 256))
    vn = _layer_norm(_gelu(rm("v", 0, 256)), lng_ref[...], lnb_ref[...])
    return rm, u, vn


def _proj_feature_major(xb, wt_ref, cta_ref, sta_ref, ctd_ref, std_ref):
    def tr(name, k=0, height=128):
        a = TR[name] + k * 128
        return _dot_nt(wt_ref[a:a + height, :], xb)

    cta, sta = cta_ref[...], sta_ref[...]
    ctd, std = ctd_ref[...], std_ref[...]
    cmp_t = tr("cmp")
    sel_t = tr("sel") * cta + tr("sel_sw") * sta
    win_t = tr("win") * cta + tr("win_sw") * sta
    sb_t = tr("sb", 0, 512)
    kd_t = [tr("diff", k) * ctd + tr("kd_sw", k) * std for k in range(2)]
    vd_t = tr("diff", 2, 256)
    return cmp_t, sel_t, win_t, sb_t, kd_t, vd_t


def _proj_prompt_kernel(x_ref, wrm_ref, wt_ref, cq_ref, sq_ref, cd_ref, sd_ref,
                        cta_ref, sta_ref, ctd_ref, std_ref, gb_ref, lng_ref, lnb_ref, ws_ref, bsm_ref,
                        qaf_ref, qb_ref, qd_ref, g_ref, od_ref, cmprm_ref,
                        cmpt_ref, selt_ref, wint_ref, sbt_ref, difft_ref,
                        selbf_ref, winbf_ref, sbbf_ref, diffbf_ref, *, tm, tk):
    xb = x_ref[...].astype(BF16)
    rm, u, vn = _proj_common(xb, wrm_ref, cq_ref, sq_ref, cd_ref, sd_ref, gb_ref, lng_ref, lnb_ref,
                             qaf_ref, qb_ref, qd_ref, g_ref)
    cmprm_ref[...] = rm("cmp")

    tril = _iota((CHUNK, CHUNK), 1) <= _iota((CHUNK, CHUNK), 0)
    lane_group = _iota((CHUNK, GROUP_WIDTH), 1) >> 6
    wsg = [jnp.where(tril, ws_ref[g], 0.0).astype(BF16) for g in range(N_HEADS)]
    bsm = bsm_ref[...]
    for c in range(tm // CHUNK):
        rows = slice(c * CHUNK, (c + 1) * CHUNK)
        vnc = vn[rows, :]
        mixed = bsm
        for g in range(N_HEADS):
            mixed = mixed + _dot(wsg[g], jnp.where(lane_group == g, vnc, 0.0).astype(BF16))
        od_ref[rows, :] = (u[rows, :] * mixed).astype(BF16)

    cmp_t, sel_t, win_t, sb_t, kd_t, vd_t = _proj_feature_major(
        xb, wt_ref, cta_ref, sta_ref, ctd_ref, std_ref)
    cmpt_ref[...] = cmp_t
    selt_ref[...] = sel_t
    wint_ref[...] = win_t
    sbt_ref[...] = sb_t
    difft_ref[0:128, :] = kd_t[0]
    difft_ref[128:256, :] = kd_t[1]
    difft_ref[256:512, :] = vd_t
    for t in range(tm // tk):
        cols = slice(t * tk, (t + 1) * tk)
        selbf_ref[t] = sel_t[:, cols].astype(BF16)
        winbf_ref[t] = win_t[:, cols].astype(BF16)
        sbbf_ref[t] = sb_t[:, cols].astype(BF16)
        diffbf_ref[t, 0:128, :] = kd_t[0][:, cols].astype(BF16)
        diffbf_ref[t, 128:256, :] = kd_t[1][:, cols].astype(BF16)
        diffbf_ref[t, 256:512, :] = vd_t[:, cols].astype(BF16)


def _proj_sample_kernel(x_ref, wrm_ref, wt_ref, cq_ref, sq_ref, cd_ref, sd_ref,
                        cta_ref, sta_ref, ctd_ref, std_ref, gb_ref, lng_ref, lnb_ref, w00_ref, b0_ref,
                        qaf_ref, qb_ref, qd_ref, g_ref, od_ref, vn_ref,
                        selrm_ref, winrm_ref, diffrm_ref,
                        cmpt_ref, selt_ref, wint_ref, sbt_ref, difft_ref):
    xb = x_ref[...].astype(BF16)
    rm, u, vn = _proj_common(xb, wrm_ref, cq_ref, sq_ref, cd_ref, sd_ref, gb_ref, lng_ref, lnb_ref,
                             qaf_ref, qb_ref, qd_ref, g_ref)
    vn_ref[...] = vn
    od_ref[...] = (u * (vn * w00_ref[...] + b0_ref[...])).astype(BF16)

    cq, sq = cq_ref[...], sq_ref[...]
    cd, sd = cd_ref[...], sd_ref[...]
    selrm_ref[...] = rm("sel") * cq + rm("sel_sw") * sq
    winrm_ref[...] = rm("win") * cq + rm("win_sw") * sq
    for k in range(2):
        diffrm_ref[:, k * 128:(k + 1) * 128] = rm("diff", k) * cd + rm("kd_sw", k) * sd
    diffrm_ref[:, 256:512] = rm("diff", 2, 256)

    cmp_t, sel_t, win_t, sb_t, kd_t, vd_t = _proj_feature_major(
        xb, wt_ref, cta_ref, sta_ref, ctd_ref, std_ref)
    cmpt_ref[...] = cmp_t
    selt_ref[...] = sel_t
    wint_ref[...] = win_t
    sbt_ref[...] = sb_t
    difft_ref[0:128, :] = kd_t[0]
    difft_ref[128:256, :] = kd_t[1]
    difft_ref[256:512, :] = vd_t


def _proj_prompt(x, wrm, wt, tabs, gb, lng, lnb, ws, bsm):
    B, S, _ = x.shape
    tm, tk = TOKEN_TILE, KV_TILE
    nk = S // tk
    row = lambda n: pl.BlockSpec((None, tm, n), lambda b, i: (b, i, 0))
    feat = lambda n: pl.BlockSpec((None, n, tm), lambda b, i: (b, 0, i))
    tile = lambda n: pl.BlockSpec((None, tm // tk, n, tk), lambda b, i: (b, i, 0, 0))
    tab_rm = pl.BlockSpec((tm, 128), lambda b, i: (i, 0))
    tab_t = pl.BlockSpec((128, tm), lambda b, i: (0, i))
    in_specs = [row(D_MODEL), _const_spec(wrm.shape), _const_spec(wt.shape),
                tab_rm, tab_rm, tab_rm, tab_rm, tab_t, tab_t, tab_t, tab_t,
                _const_spec(gb.shape), _const_spec(lng.shape), _const_spec(lnb.shape),
                _const_spec(ws.shape), _const_spec(bsm.shape)]
    sd = jax.ShapeDtypeStruct
    out_shape = [sd((B, S, 512), BF16), sd((B, S, 256), BF16), sd((B, S, 256), BF16),
                 sd((B, S, 128), F32), sd((B, S, 256), BF16), sd((B, S, 128), F32),
                 sd((B, 128, S), F32), sd((B, 128, S), F32), sd((B, 128, S), F32),
                 sd((B, 512, S), F32), sd((B, 512, S), F32),
                 sd((B, nk, 128, tk), BF16), sd((B, nk, 128, tk), BF16),
                 sd((B, nk, 512, tk), BF16), sd((B, nk, 512, tk), BF16)]
    out_specs = [row(512), row(256), row(256), row(128), row(256), row(128),
                 feat(128), feat(128), feat(128), feat(512), feat(512),
                 tile(128), tile(128), tile(512), tile(512)]
    return pl.pallas_call(
        functools.partial(_proj_prompt_kernel, tm=tm, tk=tk),
        grid=(B, S // tm), in_specs=in_specs, out_specs=out_specs, out_shape=out_shape,
        compiler_params=_params(("parallel", "parallel")), name="proj_prompt",
    )(x, wrm, wt, *tabs, gb, lng, lnb, ws, bsm)


def _proj_sample(x, wrm, wt, tabs, gb, lng, lnb, w00, b0):
    nb = x.shape[0]
    full = lambda a: pl.BlockSpec(a.shape, lambda i: (0,) * a.ndim)
    ins = (x, wrm, wt, *tabs, gb, lng, lnb, w00, b0)
    sd = jax.ShapeDtypeStruct
    out_shape = [sd((nb, 512), BF16), sd((nb, 256), BF16), sd((nb, 256), BF16), sd((nb, 128), F32),
                 sd((nb, 256), BF16), sd((nb, 256), F32),
                 sd((nb, 128), F32), sd((nb, 128), F32), sd((nb, 512), F32),
                 sd((128, nb), F32), sd((128, nb), F32), sd((128, nb), F32),
                 sd((512, nb), F32), sd((512, nb), F32)]
    return pl.pallas_call(
        _proj_sample_kernel, grid=(1,), in_specs=[full(a) for a in ins],
        out_specs=[pl.BlockSpec(s.shape, lambda i: (0, 0)) for s in out_shape], out_shape=out_shape,
        compiler_params=_params(("arbitrary",)), name="proj_sample",
    )(*ins)


def _compress_rows(rows_ref, first, stride, n_out, pe_ref, w1_ref, w2_ref):
    h = jnp.zeros((n_out, 128), F32)
    for r in range(CMP_BLOCK):
        x = rows_ref[pl.ds(first + r, n_out, stride=stride), :] + pe_ref[r:r + 1, :]
        h = h + _dot(x.astype(BF16), w1_ref[r])
    return _dot(_gelu(h).astype(BF16), w2_ref[...])


def _compress_prompt_kernel(x_ref, pe_ref, w1_ref, w2_ref, o_ref, *, ns):
    for parity in range(2):
        o_ref[parity * ns:(parity + 1) * ns, :] = _compress_rows(
            x_ref, parity * CMP_BLOCK, SEL_BLOCK, ns, pe_ref, w1_ref, w2_ref).astype(BF16)


def _compress_prompt(cmp_rm, pe, w1, w2):
    B, S, _ = cmp_rm.shape
    ns = S // SEL_BLOCK
    return pl.pallas_call(
        functools.partial(_compress_prompt_kernel, ns=ns), grid=(B,),
        in_specs=[pl.BlockSpec((None, S, 128), lambda b: (b, 0, 0)), _const_spec(pe.shape),
                  _const_spec(w1.shape), _const_spec(w2.shape)],
        out_specs=pl.BlockSpec((None, 2 * ns, 128), lambda b: (b, 0, 0)),
        out_shape=jax.ShapeDtypeStruct((B, 2 * ns, 128), BF16),
        compiler_params=_params(("parallel",)), name="compress_prompt",
    )(cmp_rm, pe, w1, w2)


def _compress_pool_kernel(x_ref, pe_ref, w1_ref, w2_ref, o_ref, rows_ref, *, pages):
    for p in range(pages):
        rows_ref[p * 128:(p + 1) * 128, :] = x_ref[p].T
    n_out = pages * (128 // CMP_BLOCK)
    o_ref[...] = _compress_rows(rows_ref, 0, CMP_BLOCK, n_out, pe_ref, w1_ref, w2_ref)


def _compress_pool(layer, cmp_cache, pe, w1, w2):
    n_pool = cmp_cache.shape[1]
    pages = math.gcd(n_pool, 64)
    n_out = pages * (128 // CMP_BLOCK)
    return pl.pallas_call(
        functools.partial(_compress_pool_kernel, pages=pages), grid=(n_pool // pages,),
        in_specs=[pl.BlockSpec((None, pages, 128, 128), lambda g: (layer, g, 0, 0)), _const_spec(pe.shape),
                  _const_spec(w1.shape), _const_spec(w2.shape)],
        out_specs=pl.BlockSpec((n_out, 128), lambda g: (g, 0)),
        out_shape=jax.ShapeDtypeStruct((n_pool * (128 // CMP_BLOCK), 128), F32),
        scratch_shapes=[pltpu.VMEM((pages * 128, 128), F32)],
        compiler_params=_params(("parallel",)), name="compress_pool",
    )(cmp_cache, pe, w1, w2)


def _select_topk(score, blk_f, n_blocks, k):
    sel = jnp.zeros(score.shape, F32)
    work = score
    for _ in range(k):
        mx = jnp.max(work, axis=0, keepdims=True)
        idx = jnp.min(jnp.where(work == mx, blk_f, float(n_blocks)), axis=0, keepdims=True)
        pick = blk_f == idx
        sel = jnp.where(pick, 1.0, sel)
        work = jnp.where(pick, NEG_BIG, work)
    return sel


def _select_topk_row(score, k):
    n = score.shape[1]
    mine = jnp.broadcast_to(score, (n, n))
    other = mine.T
    beats = (other > mine) | ((other == mine) & (_iota((n, n), 0) < _iota((n, n), 1)))
    rank = jnp.sum(jnp.where(beats, 1.0, 0.0), axis=0, keepdims=True)
    return jnp.where(rank < k, 1.0, 0.0)


def _masked_softmax(logits, mask):
    lm = jnp.where(mask, logits, NEG_INF)
    mx = jnp.max(lm, axis=1, keepdims=True)
    e = jnp.where(mask, jnp.exp(lm - mx), 0.0)
    return e / jnp.maximum(jnp.sum(e, axis=1, keepdims=True), TINY)


def _cmp_block_end(n_half):
    r = _iota((1, 2 * n_half), 1)
    return jnp.where(r < n_half, 64 * r + 31, 64 * (r - n_half) + 63)


def _nsa_prompt_kernel(qf_ref, g_ref, kvc_ref, sel_ref, win_ref, o_ref, qs_ref, acc_ref, m_ref, l_ref, bias_ref,
                       *, tq, sub, ns, topk):
    tk = tq
    i = pl.program_id(1)
    tpos = i * tq + _iota((tq, 1), 0)
    qh = [qf_ref[:, h * 128:(h + 1) * 128] for h in range(N_HEADS)]
    kvc = kvc_ref[...]

    maskc = _cmp_block_end(ns) <= tpos
    oc, impc = [], jnp.zeros((tq, 2 * ns), F32)
    for h in range(N_HEADS):
        pc = _masked_softmax(_dot_nt(qh[h], kvc), maskc)
        oc.append(_dot(pc.astype(BF16), kvc))
        impc = impc + pc
    imp = impc[:, :ns] + impc[:, ns:]

    blk = _iota((ns, tq), 0)
    tpos_row = i * tq + _iota((1, tq), 1)
    cur = tpos_row >> 6
    forced = (blk == 0) | (blk == cur) | (blk == cur - 1)
    valid = (blk << 6) <= tpos_row
    score = jnp.where(forced, FORCE_SCORE, jnp.where(valid, imp.T, -1.0))
    sel_bf = _select_topk(score, blk.astype(F32), ns, topk).T.astype(BF16)

    row, col = _iota((tq, tk), 0), _iota((tq, tk), 1)

    def chosen_in_tile(j):
        expand = (_iota((ns, tk), 0) == j * (tk // SEL_BLOCK) + (_iota((ns, tk), 1) >> 6))
        return _dot(sel_bf, expand.astype(BF16)) > 0.5

    def sel_bias_step(j, carry):
        bias_ref[j] = jnp.where(chosen_in_tile(j), 0.0, NEG_INF).astype(BF16)
        return carry

    lax.fori_loop(0, i, sel_bias_step, 0)
    bias_ref[i] = jnp.where(chosen_in_tile(i) & (col <= row), 0.0, NEG_INF).astype(BF16)

    for h in range(N_HEADS):
        qs_ref[h * tq:(h + 1) * tq, :] = qh[h]

    def attend(kv_ref, j_lo, j_hi, bias_of):
        def scores(j):
            kt = _key_rows(kv_ref, j, sub, 0, 128)
            s = _dot(qs_ref[...], kt).reshape(N_HEADS, tq, tk) + bias_of(j)[None]
            return s.reshape(N_HEADS * tq, tk), kt

        def max_step(j, carry):
            s, _ = scores(j)
            m_ref[...] = jnp.maximum(m_ref[...], _chunk_reduce(s, jnp.maximum))
            return carry

        def sum_step(j, carry):
            s, kt = scores(j)
            m = m_ref[...]
            p = [jnp.exp(c - m) for c in _lane_chunks(s)]
            l_ref[...] += functools.reduce(jnp.add, p)
            acc_ref[...] += _dot_nt(jnp.concatenate(p, axis=1).astype(BF16), kt)
            return carry

        m_ref[...] = jnp.full(m_ref.shape, NEG_INF, F32)
        lax.fori_loop(j_lo, j_hi, max_step, 0)
        m_ref[...] = jnp.broadcast_to(jnp.max(m_ref[...], axis=1, keepdims=True), m_ref.shape)
        acc_ref[...] = jnp.zeros(acc_ref.shape, F32)
        l_ref[...] = jnp.zeros(l_ref.shape, F32)
        lax.fori_loop(j_lo, j_hi, sum_step, 0)
        out = acc_ref[...] / jnp.maximum(jnp.sum(l_ref[...], axis=1, keepdims=True), TINY)
        return [out[h * tq:(h + 1) * tq, :] for h in range(N_HEADS)]

    def win_bias(j):
        d = (i - j) * tk + row - col
        return jnp.where((d >= 0) & (d < WINDOW), 0.0, NEG_INF)

    osel = attend(sel_ref, 0, i + 1, lambda j: bias_ref[j].astype(F32))
    ow = attend(win_ref, jnp.maximum(i - WINDOW // tk, 0), i + 1, win_bias)

    g = g_ref[...]
    lane = _iota((tq, 128), 1)
    mixed = [g[:, 3 * h:3 * h + 1] * oc[h] + g[:, 3 * h + 1:3 * h + 2] * osel[h]
             + g[:, 3 * h + 2:3 * h + 3] * ow[h] for h in range(N_HEADS)]
    for k in range(2):
        even = pltpu.roll(mixed[2 * k], 64, 1)
        o_ref[:, k * 128:(k + 1) * 128] = jnp.where(lane < 64, even, mixed[2 * k + 1]).astype(BF16)


def _nsa_prompt(qaf, gates, kvc, selbf, winbf):
    B, S, _ = qaf.shape
    tq = math.gcd(S, BIG_Q_TILE)
    ns = S // SEL_BLOCK
    kern = functools.partial(_nsa_prompt_kernel, tq=tq, sub=tq // KV_TILE, ns=ns, topk=min(SEL_TOPK, ns))
    row = lambda n: pl.BlockSpec((None, tq, n), lambda b, i: (b, i, 0))
    whole = lambda a: pl.BlockSpec((None,) + a.shape[1:], lambda b, i: (b,) + (0,) * (a.ndim - 1))
    return pl.pallas_call(
        kern, grid=(B, S // tq),
        in_specs=[row(512), row(128), whole(kvc), whole(selbf), whole(winbf)],
        out_specs=row(256), out_shape=jax.ShapeDtypeStruct((B, S, 256), BF16),
        scratch_shapes=[pltpu.VMEM((N_HEADS * tq, 128), BF16), pltpu.VMEM((N_HEADS * tq, 128), F32),
                        pltpu.VMEM((N_HEADS * tq, 128), F32), pltpu.VMEM((N_HEADS * tq, 128), F32),
                        pltpu.VMEM((S // tq, tq, tq), BF16)],
        compiler_params=_params(("parallel", "arbitrary")), name="nsa_prompt",
    )(qaf, gates, kvc, selbf, winbf)


def _sb_prompt_kernel(q_ref, kv_ref, o_ref, qs_ref, acc_ref, car_ref, *, tq, tk):
    assert tq == tk
    i = pl.program_id(1)
    rows = N_HEADS * tq
    q = q_ref[...].astype(F32)
    lane_head = _iota((1, GROUP_WIDTH), 1) >> 6
    for h in range(N_HEADS):
        qs_ref[h * tq:(h + 1) * tq, :] = jnp.where(lane_head == h, q, 0.0).astype(BF16)
    suffix = _suffix_matrix(tk)
    acc_ref[...] = jnp.zeros(acc_ref.shape, F32)
    car_ref[...] = jnp.zeros(car_ref.shape, F32)

    def step(j, strict):
        z = _dot(qs_ref[...], kv_ref[j, 0:256, :])
        log_keep = -_softplus(z)
        if strict is not None:
            keep = jnp.broadcast_to(strict[None], (N_HEADS, tq, tk)).reshape(rows, tk)
            log_keep = jnp.where(keep, log_keep, 0.0)
            z = jnp.where(keep, z, NEG_INF)
        hi, lo = _split_bf16(log_keep)
        incl = _dot(hi, suffix) + _dot(lo, suffix)
        car = car_ref[...]
        ab = jnp.exp(z + incl + jnp.concatenate([car, car], axis=1)).astype(BF16)
        for k in range(2):
            pair = slice(2 * k * tq, (2 * k + 2) * tq)
            acc_ref[pair, :] += _dot_nt(ab[pair, :], kv_ref[j, 256 + k * 128:384 + k * 128, :])
        car_ref[...] = car + jnp.broadcast_to(incl[:, 0:1], (rows, 128))

    def any_weight_left():
        return jnp.max(car_ref[...]) > SB_UNDERFLOW

    step(i, _iota((tq, tk), 1) < _iota((tq, tk), 0))

    def left_tile(state):
        j, _ = state
        step(j, None)
        return j - 1, any_weight_left()

    lax.while_loop(lambda state: (state[0] >= 0) & state[1], left_tile, (i - 1, any_weight_left()))
    lane = _iota((tq, 128), 1)
    for k in range(2):
        even, odd = acc_ref[2 * k * tq:(2 * k + 1) * tq, :], acc_ref[(2 * k + 1) * tq:(2 * k + 2) * tq, :]
        o_ref[:, k * 128:(k + 1) * 128] = jnp.where(lane < 64, even, odd).astype(BF16)


def _sb_prompt(qb, sbbf):
    B, S, _ = qb.shape
    tq, tk = Q_TILE, KV_TILE
    row = lambda n: pl.BlockSpec((None, tq, n), lambda b, i: (b, i, 0))
    return pl.pallas_call(
        functools.partial(_sb_prompt_kernel, tq=tq, tk=tk), grid=(B, S // tq),
        in_specs=[row(256), pl.BlockSpec((None,) + sbbf.shape[1:], lambda b, i: (b, 0, 0, 0))],
        out_specs=row(256), out_shape=jax.ShapeDtypeStruct((B, S, 256), BF16),
        scratch_shapes=[pltpu.VMEM((N_HEADS * tq, GROUP_WIDTH), BF16), pltpu.VMEM((N_HEADS * tq, 128), F32),
                        pltpu.VMEM((N_HEADS * tq, 128), F32)],
        compiler_params=_params(("parallel", "arbitrary")), name="sb_prompt",
    )(qb, sbbf)


def _diff_lambda(dl_ref, lam_init):
    lq = dl_ref[...]
    s1 = jnp.sum(lq[0:1, :] * lq[1:2, :], axis=1, keepdims=True)
    s2 = jnp.sum(lq[2:3, :] * lq[3:4, :], axis=1, keepdims=True)
    return jnp.exp(s1) - jnp.exp(s2) + lam_init


def _head_rms_norm(out, lane_head, ng, lam_init):
    sq = out * out
    rs = jnp.zeros(out.shape, F32)
    for h in range(N_HEADS):
        ms = jnp.sum(jnp.where(lane_head == h, sq, 0.0), axis=1, keepdims=True) * (1.0 / HEAD_DIM)
        rs = jnp.where(lane_head == h, lax.rsqrt(ms + LN_EPS), rs)
    return out * rs * ng * (1.0 - lam_init)


def _lane_chunks(x):
    return [x[:, c * LANES:(c + 1) * LANES] for c in range(x.shape[1] // LANES)]


def _chunk_reduce(x, op):
    chunks = _lane_chunks(x)
    out = chunks[0]
    for c in chunks[1:]:
        out = op(out, c)
    return out


def _key_rows(kv_ref, j, sub, r0, r1):
    return jnp.concatenate([kv_ref[j * sub + u, r0:r1, :] for u in range(sub)], axis=1)


def _diff_prompt_kernel(li_ref, dl_ref, ng_ref, q_ref, kv_ref, o_ref, qs_ref, acc_ref, m_ref, l_ref, *, tq, sub):
    tk = tq
    i = pl.program_id(1)
    lam_init = li_ref[0]
    lam = _diff_lambda(dl_ref, lam_init)
    q = q_ref[...].astype(F32)
    lane = _iota((1, GROUP_WIDTH), 1)
    lane_head, lane_half = lane >> 6, (lane >> 5) & 1
    n_soft = 2 * N_HEADS
    rows = n_soft * tq
    for a in range(n_soft):
        qa = jnp.where((lane_head == a // 2) & (lane_half == a % 2), q, 0.0)
        qs_ref[a * tq:(a + 1) * tq, :] = qa.astype(BF16)
    causal_bias = jnp.where(_iota((tq, tk), 1) <= _iota((tq, tk), 0), 0.0, NEG_INF)

    def scores(j, bias):
        s = _dot(qs_ref[...], _key_rows(kv_ref, j, sub, 0, 256))
        if bias is not None:
            s = (s.reshape(n_soft, tq, tk) + bias[None]).reshape(rows, tk)
        return s

    def max_step(j, bias):
        m_ref[...] = jnp.maximum(m_ref[...], _chunk_reduce(scores(j, bias), jnp.maximum))

    def sum_step(j, bias):
        s = scores(j, bias)
        m = m_ref[...]
        p = [jnp.exp(c - m) for c in _lane_chunks(s)]
        l_ref[...] += functools.reduce(jnp.add, p)
        pb = jnp.concatenate(p, axis=1).astype(BF16)
        for k in range(2):
            part = slice(k * (rows // 2), (k + 1) * (rows // 2))
            acc_ref[part, :] += _dot_nt(pb[part, :], _key_rows(kv_ref, j, sub, 256 + k * 128, 384 + k * 128))

    def below(step):
        def body(j, carry):
            step(j, None)
            return carry
        lax.fori_loop(0, i, body, 0)

    m_ref[...] = jnp.full(m_ref.shape, NEG_INF, F32)
    below(max_step)
    max_step(i, causal_bias)
    m_ref[...] = jnp.broadcast_to(jnp.max(m_ref[...], axis=1, keepdims=True), m_ref.shape)
    acc_ref[...] = jnp.zeros(acc_ref.shape, F32)
    l_ref[...] = jnp.zeros(l_ref.shape, F32)
    below(sum_step)
    sum_step(i, causal_bias)

    lane128 = _iota((tq, 128), 1)
    ng = ng_ref[...]
    blk = lambda ref, a: ref[a * tq:(a + 1) * tq, :]
    for k in range(2):
        o = []
        for h in (2 * k, 2 * k + 1):
            l1 = jnp.maximum(jnp.sum(blk(l_ref, 2 * h), axis=1, keepdims=True), TINY)
            l2 = jnp.maximum(jnp.sum(blk(l_ref, 2 * h + 1), axis=1, keepdims=True), TINY)
            o.append(blk(acc_ref, 2 * h) / l1 - lam * (blk(acc_ref, 2 * h + 1) / l2))
        low = lane128 < HEAD_DIM
        out = jnp.where(low, o[0], o[1])
        sq = out * out
        ms_lo = jnp.sum(jnp.where(low, sq, 0.0), axis=1, keepdims=True)
        ms_hi = jnp.sum(jnp.where(low, 0.0, sq), axis=1, keepdims=True)
        rs = lax.rsqrt(jnp.where(low, ms_lo, ms_hi) * (1.0 / HEAD_DIM) + LN_EPS)
        o_ref[:, k * 128:(k + 1) * 128] = (out * rs * ng[:, k * 128:(k + 1) * 128] * (1.0 - lam_init)).astype(BF16)


def _diff_prompt(lam_init, dl, ng, qd, diffbf):
    B, S, _ = qd.shape
    tq = math.gcd(S, BIG_Q_TILE)
    row = lambda n: pl.BlockSpec((None, tq, n), lambda b, i: (b, i, 0))
    return pl.pallas_call(
        functools.partial(_diff_prompt_kernel, tq=tq, sub=tq // KV_TILE), grid=(B, S // tq),
        in_specs=[pl.BlockSpec(memory_space=pltpu.SMEM), _const_spec(dl.shape), _const_spec(ng.shape),
                  row(256), pl.BlockSpec((None,) + diffbf.shape[1:], lambda b, i: (b, 0, 0, 0))],
        out_specs=row(256), out_shape=jax.ShapeDtypeStruct((B, S, 256), BF16),
        scratch_shapes=[pltpu.VMEM((2 * N_HEADS * tq, GROUP_WIDTH), BF16), pltpu.VMEM((2 * N_HEADS * tq, 128), F32),
                        pltpu.VMEM((2 * N_HEADS * tq, 128), F32), pltpu.VMEM((2 * N_HEADS * tq, 128), F32)],
        compiler_params=_params(("parallel", "arbitrary")), name="diff_prompt",
    )(lam_init, dl, ng, qd, diffbf)


def _round_bf16(x):
    return x.astype(BF16).astype(F32)


def _softmax_with_self(s, ok, s_self):
    sm = jnp.where(ok, s, NEG_INF)
    mx = jnp.maximum(jnp.max(sm, axis=1, keepdims=True), s_self)
    e = jnp.where(ok, jnp.exp(sm - mx), 0.0)
    e_self = jnp.exp(s_self - mx)
    inv = 1.0 / (jnp.sum(e, axis=1, keepdims=True) + e_self)
    return e * inv, e_self * inv


def _nsa_sample_kernel(pt_ref, qh_ref, gs_ref, snew_ref, wnew_ref, win_ref, *rest,
                       n_pages, past, half, topk):
    del pt_ref
    cmp_refs, sel_refs = rest[:n_pages], rest[n_pages:2 * n_pages]
    o_ref, kvc_ref = rest[2 * n_pages], rest[2 * n_pages + 1]
    qh = qh_ref[...]
    qf = qh.astype(F32)

    kvc_ref[...] = jnp.zeros(kvc_ref.shape, F32)
    for p in range(n_pages):
        piece = cmp_refs[p][...]
        kvc_ref[2 * p:2 * p + 2, :] = piece[0:2, :]
        kvc_ref[half + 2 * p:half + 2 * p + 2, :] = piece[2:4, :]
    kvc = kvc_ref[...].astype(BF16)
    maskc = _cmp_block_end(half) <= past
    pc = _masked_softmax(_dot_nt(qh, kvc), maskc)
    oc = _dot(pc.astype(BF16), kvc)
    head_row = _iota((8, 1), 0) < N_HEADS
    impc = jnp.sum(jnp.where(head_row, pc, 0.0), axis=0, keepdims=True)
    imp = impc[:, :half] + impc[:, half:]

    n_sel = past // SEL_BLOCK + 1
    blk = _iota((1, half), 1)
    cur = past // SEL_BLOCK
    forced = (blk == 0) | (blk == cur) | (blk == cur - 1)
    valid = blk * SEL_BLOCK <= past
    score = jnp.where(blk < n_sel, jnp.where(forced, FORCE_SCORE, jnp.where(valid, imp, -1.0)), NEG_BIG)
    sel8 = jnp.broadcast_to(_select_topk_row(score, topk), (8, half)).astype(BF16)
    expand = _iota((half, past), 0) == (_iota((half, past), 1) >> 6)
    chosen = _dot(sel8, expand.astype(BF16)) > 0.5

    def branch(pages, new_row, ok):
        new_b = _round_bf16(new_row)
        s = jnp.concatenate([_dot(qh, t) for t in pages], axis=1)
        s_self = jnp.sum(qf * new_b, axis=1, keepdims=True)
        p, p_self = _softmax_with_self(s, ok, s_self)
        out = p_self * new_b
        off = 0
        for t in pages:
            n = t.shape[1]
            out = out + _dot_nt(p[:, off:off + n].astype(BF16), t)
            off += n
        return out

    osel = branch([r[...].astype(BF16) for r in sel_refs], snew_ref[...], chosen)
    w_past = win_ref.shape[1]
    in_window = _iota((1, w_past), 1) >= (w_past - WINDOW + 1)
    ow = branch([win_ref[...].astype(BF16)], wnew_ref[...], in_window)
    g = gs_ref[...]
    o_ref[...] = g[:, 0:1] * oc + g[:, 1:2] * osel + g[:, 2:3] * ow


def _sb_sample_kernel(pt_ref, q_ref, *rest, n_pages, past, cb):
    del pt_ref
    pages, o_ref = rest[:n_pages], rest[n_pages]
    q8 = jnp.broadcast_to(q_ref[...].astype(F32), (8, GROUP_WIDTH))
    row_head = _iota((8, GROUP_WIDTH), 0) == (_iota((8, GROUP_WIDTH), 1) >> 6)
    qm = jnp.where(row_head, q8, 0.0).astype(BF16)
    z = jnp.concatenate([_dot(qm, pages[p][0].astype(BF16)) for p in range(n_pages)], axis=1)
    log_keep = -_softplus(z)
    suffix = _suffix_matrix(cb)
    carry = jnp.zeros((8, 1), F32)
    expo = [None] * (past // cb)
    for bi in reversed(range(past // cb)):
        cols = slice(bi * cb, (bi + 1) * cb)
        hi, lo = _split_bf16(log_keep[:, cols])
        incl = _dot(hi, suffix) + _dot(lo, suffix)
        expo[bi] = z[:, cols] + incl + carry
        carry = carry + incl[:, 0:1]
    a = jnp.exp(jnp.concatenate(expo, axis=1))
    out = jnp.zeros((8, GROUP_WIDTH), F32)
    for p in range(n_pages):
        out = out + _dot_nt(a[:, p * 128:(p + 1) * 128].astype(BF16), pages[p][1].astype(BF16))
    o_ref[...] = jnp.sum(jnp.where(row_head, out, 0.0), axis=0, keepdims=True).astype(BF16)


def _diff_sample_kernel(pt_ref, li_ref, dl_ref, ng_ref, q_ref, dnew_ref, *rest, n_pages):
    del pt_ref
    pages, o_ref = rest[:n_pages], rest[n_pages]
    lam_init = li_ref[0]
    lam = _diff_lambda(dl_ref, lam_init)
    q8 = jnp.broadcast_to(q_ref[...].astype(F32), (8, GROUP_WIDTH))
    lane = _iota((8, GROUP_WIDTH), 1)
    row_head = _iota((8, GROUP_WIDTH), 0) == (lane >> 6)
    lane_half = (lane >> 5) & 1
    new_k = _round_bf16(dnew_ref[:, 0:256])
    new_v = _round_bf16(dnew_ref[:, 256:512])
    kts = [pages[p][0].astype(BF16) for p in range(n_pages)]
    ps, ps_self = [], []
    for m in range(2):
        qm = jnp.where(row_head & (lane_half == m), q8, 0.0)
        s = jnp.concatenate([_dot(qm.astype(BF16), kt) for kt in kts], axis=1)
        s_self = jnp.sum(qm * new_k, axis=1, keepdims=True)
        p, p_self = _softmax_with_self(s, jnp.full(s.shape, True), s_self)
        ps.append(p)
        ps_self.append(p_self)
    p = ps[0] - lam * ps[1]
    out = (ps_self[0] - lam * ps_self[1]) * new_v
    for pg in range(n_pages):
        out = out + _dot_nt(p[:, pg * 128:(pg + 1) * 128].astype(BF16), pages[pg][1].astype(BF16))
    o4 = jnp.sum(jnp.where(row_head, out, 0.0), axis=0, keepdims=True)
    o_ref[...] = _head_rms_norm(o4, _iota((1, GROUP_WIDTH), 1) >> 6, ng_ref[...], lam_init).astype(BF16)


def _page_spec(block, layer, p):
    nz = (0,) * (len(block) - 2)
    return pl.BlockSpec(block, lambda b, pt: (layer, pt[b, p]) + nz)


def _sample_mix_kernel(pt_ref, qh_ref, gs_ref, snew_ref, wnew_ref, win_ref, qb_ref, li_ref, dl_ref, ng_ref,
                       qd_ref, dnew_ref, *rest, n_pages, past, half, topk, cb):
    n = n_pages
    cmp_refs, sel_refs, sb_refs, diff_refs = rest[:n], rest[n:2 * n], rest[2 * n:3 * n], rest[3 * n:4 * n]
    oa_ref, ob_ref, oc_ref, kvc_ref = rest[4 * n:4 * n + 4]
    _nsa_sample_kernel(pt_ref, qh_ref, gs_ref, snew_ref, wnew_ref, win_ref, *cmp_refs, *sel_refs, oa_ref, kvc_ref,
                       n_pages=n, past=past, half=half, topk=topk)
    _sb_sample_kernel(pt_ref, qb_ref, *sb_refs, ob_ref, n_pages=n, past=past, cb=cb)
    _diff_sample_kernel(pt_ref, li_ref, dl_ref, ng_ref, qd_ref, dnew_ref, *diff_refs, oc_ref, n_pages=n)


def _sample_mix(layer, page_table, qh, gs, snew, wnew, cmp_pool, sel_cache, win_cache,
                qb, sb_cache, lam_init, dl, ng, qd, dnew, diff_cache):
    nb, n_pages = page_table.shape
    past = n_pages * 128
    half = max(128, -(-(past // SEL_BLOCK + 1) // 128) * 128)
    n_sel = past // SEL_BLOCK + 1
    kern = functools.partial(_sample_mix_kernel, n_pages=n_pages, past=past, half=half,
                             topk=min(SEL_TOPK, n_sel), cb=math.gcd(past, 256))
    per_seq = lambda a: pl.BlockSpec((None,) + a.shape[1:], lambda b, pt: (b,) + (0,) * (a.ndim - 1))
    const = lambda a: pl.BlockSpec(a.shape, lambda b, pt: (0,) * a.ndim)
    in_specs = [per_seq(qh), per_seq(gs), per_seq(snew), per_seq(wnew),
                pl.BlockSpec((None, None) + win_cache.shape[2:], lambda b, pt: (layer, b, 0, 0)),
                per_seq(qb), pl.BlockSpec(memory_space=pltpu.SMEM), const(dl), const(ng),
                per_seq(qd), per_seq(dnew)]
    in_specs += [pl.BlockSpec((None, 4, 128), functools.partial(lambda b, pt, p: (pt[b, p], 0, 0), p=p))
                 for p in range(n_pages)]
    in_specs += [_page_spec((None, None, 128, 128), layer, p) for p in range(n_pages)]
    in_specs += [_page_spec((None, None, 2, 256, 128), layer, p) for p in range(n_pages)]
    in_specs += [_page_spec((None, None, 2, 256, 128), layer, p) for p in range(n_pages)]
    row_out = pl.BlockSpec((None, 1, 256), lambda b, pt: (b, 0, 0))
    grid_spec = pltpu.PrefetchScalarGridSpec(
        num_scalar_prefetch=1, grid=(nb,), in_specs=in_specs,
        out_specs=[pl.BlockSpec((None, 8, 128), lambda b, pt: (b, 0, 0)), row_out, row_out],
        scratch_shapes=[pltpu.VMEM((2 * half, 128), F32)])
    return pl.pallas_call(
        kern, grid_spec=grid_spec,
        out_shape=[jax.ShapeDtypeStruct((nb, 8, 128), F32), jax.ShapeDtypeStruct((nb, 1, 256), BF16),
                   jax.ShapeDtypeStruct((nb, 1, 256), BF16)],
        compiler_params=_params(("arbitrary",)), name="sample_mix",
    )(page_table, qh, gs, snew, wnew, win_cache, qb, lam_init, dl, ng, qd, dnew,
      *([cmp_pool] * n_pages), *([sel_cache] * n_pages), *([sb_cache] * n_pages), *([diff_cache] * n_pages))


def _tail_kernel(x_ref, oa_ref, ob_ref, oc_ref, od_ref, wo_ref, g1_ref, b1_ref,
                 wg_ref, wu_ref, wd_ref, g2_ref, b2_ref, y_ref, *, alpha):
    mix = (_dot(oa_ref[...], wo_ref[0:256, :]) + _dot(ob_ref[...], wo_ref[256:512, :])
           + _dot(oc_ref[...], wo_ref[512:768, :]) + _dot(od_ref[...], wo_ref[768:1024, :]))
    x1 = _layer_norm(alpha * x_ref[...] + mix, g1_ref[...], b1_ref[...])
    x1b = x1.astype(BF16)
    hg = _dot(x1b, wg_ref[...])
    hu = _dot(x1b, wu_ref[...])
    act = (hg * _sigmoid(hg) * hu).astype(BF16)
    y_ref[...] = _layer_norm(alpha * x1 + _dot(act, wd_ref[...]), g2_ref[...], b2_ref[...])


def _tail(x, oa, ob, oc, od, wo, g1, b1, wg, wu, wd, g2, b2, alpha):
    m = x.shape[0]
    tm = math.gcd(m, TOKEN_TILE)
    row = lambda n: pl.BlockSpec((tm, n), lambda i: (i, 0))
    consts = (wo, g1, b1, wg, wu, wd, g2, b2)
    return pl.pallas_call(
        functools.partial(_tail_kernel, alpha=alpha), grid=(m // tm,),
        in_specs=[row(D_MODEL), row(256), row(256), row(256), row(256)] + [_const_spec(c.shape) for c in consts],
        out_specs=row(D_MODEL), out_shape=jax.ShapeDtypeStruct((m, D_MODEL), F32),
        compiler_params=_params(("parallel",)), name="tail",
    )(x, oa, ob, oc, od, *consts)


def _rot_partner(w, group, half):
    k = w.shape[0]
    wr = w.reshape(k, -1, group)
    x1, x2 = wr[..., :half], wr[..., half:2 * half]
    return jnp.concatenate([-x2, x1, jnp.zeros_like(wr[..., 2 * half:])], axis=-1).reshape(k, -1)


def _pad_heads(w):
    k = w.shape[0]
    wr = w.reshape(k, -1, HEAD_DIM)
    return jnp.concatenate([wr, jnp.zeros_like(wr)], axis=-1).reshape(k, -1)


def _layer_weights(w, sample):
    off = [0]
    for n in (256, 64, 64, 64, 64, 64, 64, 12, 256, 256, 256, 256, 256, 256, 256, 256):
        off.append(off[-1] + n)
    col = lambda i: w[:, off[i]:off[i + 1]]
    qa, kc, vc, ks, vs, kw, vw, ga, qb, kb, vb, qd, kd, vd, u, v = [col(i) for i in range(16)]
    zeros64 = jnp.zeros((w.shape[0], 64), F32)
    qa = qa * (HEAD_DIM ** -0.5)
    qb = qb * (HEAD_DIM ** -0.5)
    qd = qd * (DIFF_QK ** -0.5)
    sel = jnp.concatenate([ks, vs], 1)
    sel_sw = jnp.concatenate([_rot_partner(ks, HEAD_DIM, ROT_HALF), zeros64], 1)
    win = jnp.concatenate([kw, vw], 1)
    win_sw = jnp.concatenate([_rot_partner(kw, HEAD_DIM, ROT_HALF), zeros64], 1)
    kd_sw = _rot_partner(kd, DIFF_QK, DIFF_ROT_HALF)
    rm = [_pad_heads(qa), _pad_heads(_rot_partner(qa, HEAD_DIM, ROT_HALF)), qb, qd,
          _rot_partner(qd, DIFF_QK, DIFF_ROT_HALF), u, v,
          jnp.concatenate([ga, jnp.zeros((w.shape[0], 128 - ga.shape[1]), F32)], 1),
          jnp.concatenate([kc, vc], 1)]
    if sample:
        rm += [sel, sel_sw, win, win_sw, kd, vd, kd_sw]
    tr = [jnp.concatenate([kc, vc], 1), sel, sel_sw, win, win_sw, kb, vb, kd, vd, kd_sw]
    return (jnp.concatenate(rm, 1).astype(BF16), jnp.concatenate(tr, 1).T.astype(BF16))


def _rope_tables(pos):
    pos = pos.astype(F32)[:, None]
    n = pos.shape[0]

    def table(half, plain, reps):
        freq = ROPE_THETA ** (-jnp.arange(half, dtype=F32) / half)
        ang = pos * freq[None, :]
        c = jnp.concatenate([jnp.cos(ang), jnp.cos(ang), jnp.ones((n, plain), F32)], 1)
        s = jnp.concatenate([jnp.sin(ang), jnp.sin(ang), jnp.zeros((n, plain), F32)], 1)
        return jnp.tile(c, (1, reps)), jnp.tile(s, (1, reps))

    ca, sa = table(ROT_HALF, 128 - 2 * ROT_HALF, 1)
    cd, sd = table(DIFF_ROT_HALF, DIFF_QK - 2 * DIFF_ROT_HALF, 4)
    return (ca, sa, cd, sd, ca.T, sa.T, cd.T, sd.T)


def _compress_weights(pe, w1, w2):
    z = jnp.zeros((CMP_BLOCK, HEAD_DIM, HEAD_DIM), F32)
    w1k = w1[0].reshape(CMP_BLOCK, HEAD_DIM, HEAD_DIM)
    w1v = w1[1].reshape(CMP_BLOCK, HEAD_DIM, HEAD_DIM)
    w1c = jnp.concatenate([jnp.concatenate([w1k, z], 2), jnp.concatenate([z, w1v], 2)], 1)
    z2 = jnp.zeros((HEAD_DIM, HEAD_DIM), F32)
    w2c = jnp.concatenate([jnp.concatenate([w2[0], z2], 1), jnp.concatenate([z2, w2[1]], 1)], 0)
    pe_rows = jnp.concatenate([pe[0], pe[1]], 1)
    return pe_rows, w1c.astype(BF16), w2c.astype(BF16)


def _feature_major(x, tail_shape):
    lead = x.shape[:-2]
    n = x.shape[-1]
    y = x.reshape(lead + tail_shape + (n,))
    nd = y.ndim
    return jnp.transpose(y, tuple(range(len(lead))) + (nd - 1,) + tuple(range(len(lead), nd - 1)))


def kernel(x_prompt, x_sample, cache_nsa_cmp, cache_nsa_sel, cache_nsa_win, cache_sb, cache_diff,
           page_table, w_in, gate_b, cmp_pe, cmp_w1, cmp_w2, diff_lam, diff_norm_g, gmlp_ln_g,
           gmlp_ln_b, gmlp_ws, gmlp_bs, w_out, ln1_g, ln1_b, w_gate, w_up, w_down, ln2_g, ln2_b):
    B, S, _ = x_prompt.shape
    nb, T, _ = x_sample.shape
    depth = w_in.shape[0]
    n_pool = cache_sb.shape[1]
    n_pages = page_table.shape[1]
    past = n_pages * cache_sb.shape[2]
    w_past = cache_nsa_win.shape[2]
    assert T == 1 and cache_sb.shape[2] == 128 and S % Q_TILE == 0 and past % SEL_BLOCK == 0
    alpha = (2 * depth) ** 0.25

    to_pages = lambda c: jnp.transpose(c, (0, 1, 3, 4, 5, 2))
    cmp_cache = to_pages(cache_nsa_cmp).reshape(depth, n_pool, 128, 128)
    sel_cache = to_pages(cache_nsa_sel).reshape(depth, n_pool, 128, 128)
    win_cache = to_pages(cache_nsa_win).reshape(depth, nb, 128, w_past)
    sb_cache = to_pages(cache_sb).reshape(depth, n_pool, 2, 256, 128)
    diff_cache = to_pages(cache_diff).reshape(depth, n_pool, 2, 256, 128)

    tabs_p = _rope_tables(jnp.arange(S, dtype=jnp.int32))
    tabs_s = _rope_tables(jnp.full((nb,), past, jnp.int32))
    row = lambda a: a.reshape(1, -1)

    xp, xs = x_prompt, x_sample.reshape(nb, D_MODEL)
    outs = [[] for _ in range(11)]
    for l in range(depth):
        lam_init = jnp.full((1,), 0.8 - 0.6 * math.exp(-0.3 * l), F32)
        wrm_p, wt = _layer_weights(w_in[l], sample=False)
        wrm_s, _ = _layer_weights(w_in[l], sample=True)
        gb = jnp.concatenate([gate_b[l], jnp.zeros((128 - gate_b.shape[1],), F32)]).reshape(1, 128)
        lng, lnb = row(gmlp_ln_g[l]), row(gmlp_ln_b[l])
        bsm = jnp.repeat(gmlp_bs[l].T, HEAD_DIM, axis=1)
        w00 = row(jnp.repeat(gmlp_ws[l][:, 0, 0], HEAD_DIM))
        b0 = row(jnp.repeat(gmlp_bs[l][:, 0], HEAD_DIM))
        pe_rows, w1c, w2c = _compress_weights(cmp_pe[l], cmp_w1[l], cmp_w2[l])
        ng = row(jnp.tile(diff_norm_g[l], N_HEADS))
        tail_w = (w_out[l].astype(BF16), row(ln1_g[l]), row(ln1_b[l]), w_gate[l].astype(BF16),
                  w_up[l].astype(BF16), w_down[l].astype(BF16), row(ln2_g[l]), row(ln2_b[l]))

        (qaf, qb, qd, gates, od, cmp_rm, cmp_t, sel_t, win_t, sb_t, diff_t,
         selbf, winbf, sbbf, diffbf) = _proj_prompt(xp, wrm_p, wt, tabs_p, gb, lng, lnb, gmlp_ws[l], bsm)
        kvc = _compress_prompt(cmp_rm, pe_rows, w1c, w2c)
        oa = _nsa_prompt(qaf, gates, kvc, selbf, winbf)
        ob = _sb_prompt(qb, sbbf)
        oc = _diff_prompt(lam_init, diff_lam[l], ng, qd, diffbf)
        flat = lambda a: a.reshape(B * S, a.shape[-1])
        xp = _tail(flat(xp), flat(oa), flat(ob), flat(oc), flat(od), *tail_w, alpha).reshape(B, S, D_MODEL)
        outs[0].append(_feature_major(cmp_t, (2, 1, HEAD_DIM)))
        outs[2].append(_feature_major(sel_t, (2, 1, HEAD_DIM)))
        outs[4].append(_feature_major(win_t[:, :, S - min(WINDOW, S):], (2, 1, HEAD_DIM)))
        outs[6].append(_feature_major(sb_t, (2, N_HEADS, HEAD_DIM)))
        outs[8].append(_feature_major(diff_t, (2, N_HEADS, HEAD_DIM)))

        (qaf, qb, qd, gates, od, vn, sel_rm, win_rm, diff_rm,
         cmp_t, sel_t, win_t, sb_t, diff_t) = _proj_sample(xs, wrm_s, wt, tabs_s, gb, lng, lnb, w00, b0)
        cmp_pool = _compress_pool(l, cmp_cache, pe_rows, w1c, w2c).reshape(n_pool, 4, 128)[:, jnp.array([0, 2, 1, 3])]
        qh = jnp.pad(qaf.reshape(nb, N_HEADS, 128), ((0, 0), (0, 8 - N_HEADS), (0, 0)))
        gs = jnp.pad(gates[:, :3 * N_HEADS].reshape(nb, N_HEADS, 3), ((0, 0), (0, 8 - N_HEADS), (0, 125)))
        oa, ob, oc = _sample_mix(
            l, page_table, qh, gs, sel_rm.reshape(nb, 1, 128), win_rm.reshape(nb, 1, 128), cmp_pool, sel_cache,
            win_cache, qb.reshape(nb, 1, 256), sb_cache, lam_init, diff_lam[l], ng, qd.reshape(nb, 1, 256),
            diff_rm.reshape(nb, 1, 512), diff_cache)
        oa = oa[:, :N_HEADS, HEAD_DIM:].reshape(nb, GROUP_WIDTH).astype(BF16)
        ob, oc = ob.reshape(nb, 256), oc.reshape(nb, 256)
        xs = _tail(xs, oa, ob, oc, od, *tail_w, alpha)
        new_row = lambda t, shape: _feature_major(t, shape)[:, None]
        outs[1].append(new_row(cmp_t, (2, 1, HEAD_DIM)))
        outs[3].append(new_row(sel_t, (2, 1, HEAD_DIM)))
        outs[5].append(jnp.concatenate([cache_nsa_win[l][:, T:], new_row(win_t, (2, 1, HEAD_DIM))], axis=1))
        outs[7].append(new_row(sb_t, (2, N_HEADS, HEAD_DIM)))
        outs[9].append(new_row(diff_t, (2, N_HEADS, HEAD_DIM)))
        outs[10].append(vn.reshape(nb, 1, GROUP_WIDTH))

    stacked = [jnp.stack(o) for o in outs]
    return (xp, xs.reshape(nb, 1, D_MODEL), *stacked)
```

```python
import functools
import math

import jax
import jax.numpy as jnp
from jax import lax
from jax.experimental import pallas as pl
from jax.experimental.pallas import tpu as pltpu

D_MODEL = 1024
HEAD_DIM = 64
N_HEADS = 4
GROUP_WIDTH = N_HEADS * HEAD_DIM
CMP_BLOCK = 32
SEL_BLOCK = 64
SEL_TOPK = 16
WINDOW = 512
FORCE_SCORE = 1.0e4
DIFF_QK = HEAD_DIM // 2
CHUNK = 128
ROPE_THETA = 500000.0
ROT_HALF = HEAD_DIM // 8
DIFF_ROT_HALF = DIFF_QK // 8
LN_EPS = 1e-5
NEG_INF = -1e30
TINY = 1e-30
NEG_BIG = -3.0e38
SB_UNDERFLOW = -110.0
EXP_HEADROOM = 50.0

LANES = 128
VMEM_LIMIT_BYTES = 56 * 1024 * 1024

TOKEN_TILE = 256
Q_TILE = 256
BIG_Q_TILE = 512
KV_TILE = 256

F32 = jnp.float32
BF16 = jnp.bfloat16

RM = dict(qa=0, qa_sw=512, qb=1024, qd=1280, qd_sw=1536, u=1792, v=2048, gate=2304, cmp=2432,
          sel=2560, sel_sw=2688, win=2816, win_sw=2944, diff=3072, kd_sw=3584)
RM_PROMPT_COLS = 2560
RM_SAMPLE_COLS = 3840
TR = dict(cmp=0, sel=128, sel_sw=256, win=384, win_sw=512, sb=640, diff=1152, kd_sw=1664)
TR_ROWS = 1920


def _dot(a, b):
    return jnp.dot(a, b, preferred_element_type=F32)


def _dot_nt(a, b):
    return lax.dot_general(a, b, (((1,), (1,)), ((), ())), preferred_element_type=F32)


def _iota(shape, dim):
    return lax.broadcasted_iota(jnp.int32, shape, dim)


def _gelu(x):
    return 0.5 * x * (1.0 + jnp.tanh(0.7978845608028654 * (x + 0.044715 * (x * x * x))))


def _sigmoid(x):
    return 1.0 / (1.0 + jnp.exp(-x))


def _layer_norm(x, g, b):
    mu = jnp.mean(x, axis=-1, keepdims=True)
    xc = x - mu
    var = jnp.mean(xc * xc, axis=-1, keepdims=True)
    return xc * lax.rsqrt(var + LN_EPS) * g + b


def _softplus(z):
    return jnp.maximum(z, 0.0) + jnp.log(1.0 + jnp.exp(-jnp.abs(z)))


def _split_bf16(x):
    hi = x.astype(BF16)
    lo = (x - hi.astype(F32)).astype(BF16)
    return hi, lo


def _suffix_matrix(n):
    return (_iota((n, n), 0) >= _iota((n, n), 1)).astype(BF16)


def _params(sem):
    return pltpu.CompilerParams(dimension_semantics=sem, vmem_limit_bytes=VMEM_LIMIT_BYTES)


def _const_spec(shape):
    zeros = (0,) * len(shape)
    return pl.BlockSpec(shape, lambda *_: zeros, pipeline_mode=pl.Buffered(1))


def _proj_common(xb, wrm_ref, cq_ref, sq_ref, cd_ref, sd_ref, gb_ref, lng_ref, lnb_ref,
                 qaf_ref, qb_ref, qd_ref, g_ref):
    def rm(name, k=0, width=128):
        a = RM[name] + k * 128
        return _dot(xb, wrm_ref[:, a:a + width])

    cq, sq = cq_ref[...], sq_ref[...]
    for h in range(N_HEADS):
        q = rm("qa", h) * cq + rm("qa_sw", h) * sq
        qaf_ref[:, h * 128:(h + 1) * 128] = q.astype(BF16)
    qb_ref[...] = rm("qb", 0, 256).astype(BF16)
    cd, sd = cd_ref[...], sd_ref[...]
    for k in range(2):
        q = rm("qd", k) * cd + rm("qd_sw", k) * sd
        qd_ref[:, k * 128:(k + 1) * 128] = q.astype(BF16)
    g_ref[...] = _sigmoid(rm("gate") + gb_ref[...])
    u = _gelu(rm("u", 0, 256))
    vn = _layer_norm(_gelu(rm("v", 0, 256)), lng_ref[...], lnb_ref[...])
    return rm, u, vn


def _proj_feature_major(xb, wt_ref, cta_ref, sta_ref, ctd_ref, std_ref):
    def tr(name, k=0, height=128):
        a = TR[name] + k * 128
        return _dot_nt(wt_ref[a:a + height, :], xb)

    cta, sta = cta_ref[...], sta_ref[...]
    ctd, std = ctd_ref[...], std_ref[...]
    cmp_t = tr("cmp")
    sel_t = tr("sel") * cta + tr("sel_sw") * sta
    win_t = tr("win") * cta + tr("win_sw") * sta
    sb_t = tr("sb", 0, 512)
    kd_t = [tr("diff", k) * ctd + tr("kd_sw", k) * std for k in range(2)]
    vd_t = tr("diff", 2, 256)
    return cmp_t, sel_t, win_t, sb_t, kd_t, vd_t


def _proj_prompt_kernel(x_ref, wrm_ref, wt_ref, cq_ref, sq_ref, cd_ref, sd_ref,
                        cta_ref, sta_ref, ctd_ref, std_ref, gb_ref, lng_ref, lnb_ref, ws_ref, bsm_ref,
                        qaf_ref, qb_ref, qd_ref, g_ref, od_ref, cmprm_ref,
                        cmpt_ref, selt_ref, wint_ref, sbt_ref, difft_ref,
                        selbf_ref, winbf_ref, sbbf_ref, diffbf_ref, *, tm, tk):
    xb = x_ref[...].astype(BF16)
    rm, u, vn = _proj_common(xb, wrm_ref, cq_ref, sq_ref, cd_ref, sd_ref, gb_ref, lng_ref, lnb_ref,
                             qaf_ref, qb_ref, qd_ref, g_ref)
    cmprm_ref[...] = rm("cmp")

    tril = _iota((CHUNK, CHUNK), 1) <= _iota((CHUNK, CHUNK), 0)
    lane_group = _iota((CHUNK, GROUP_WIDTH), 1) >> 6
    wsg = [jnp.where(tril, ws_ref[g], 0.0).astype(BF16) for g in range(N_HEADS)]
    bsm = bsm_ref[...]
    for c in range(tm // CHUNK):
        rows = slice(c * CHUNK, (c + 1) * CHUNK)
        vnc = vn[rows, :]
        mixed = bsm
        for g in range(N_HEADS):
            mixed = mixed + _dot(wsg[g], jnp.where(lane_group == g, vnc, 0.0).astype(BF16))
        od_ref[rows, :] = (u[rows, :] * mixed).astype(BF16)

    cmp_t, sel_t, win_t, sb_t, kd_t, vd_t = _proj_feature_major(
        xb, wt_ref, cta_ref, sta_ref, ctd_ref, std_ref)
    cmpt_ref[...] = cmp_t
    selt_ref[...] = sel_t
    wint_ref[...] = win_t
    sbt_ref[...] = sb_t
    difft_ref[0:128, :] = kd_t[0]
    difft_ref[128:256, :] = kd_t[1]
    difft_ref[256:512, :] = vd_t
    for t in range(tm // tk):
        cols = slice(t * tk, (t + 1) * tk)
        selbf_ref[t] = sel_t[:, cols].astype(BF16)
        winbf_ref[t] = win_t[:, cols].astype(BF16)
        sbbf_ref[t] = sb_t[:, cols].astype(BF16)
        diffbf_ref[t, 0:128, :] = kd_t[0][:, cols].astype(BF16)
        diffbf_ref[t, 128:256, :] = kd_t[1][:, cols].astype(BF16)
        diffbf_ref[t, 256:512, :] = vd_t[:, cols].astype(BF16)


def _proj_sample_kernel(x_ref, wrm_ref, wt_ref, cq_ref, sq_ref, cd_ref, sd_ref,
                        cta_ref, sta_ref, ctd_ref, std_ref, gb_ref, lng_ref, lnb_ref, w00_ref, b0_ref,
                        qaf_ref, qb_ref, qd_ref, g_ref, od_ref, vn_ref,
                        selrm_ref, winrm_ref, diffrm_ref,
                        cmpt_ref, selt_ref, wint_ref, sbt_ref, difft_ref):
    xb = x_ref[...].astype(BF16)
    rm, u, vn = _proj_common(xb, wrm_ref, cq_ref, sq_ref, cd_ref, sd_ref, gb_ref, lng_ref, lnb_ref,
                             qaf_ref, qb_ref, qd_ref, g_ref)
    vn_ref[...] = vn
    od_ref[...] = (u * (vn * w00_ref[...] + b0_ref[...])).astype(BF16)

    cq, sq = cq_ref[...], sq_ref[...]
    cd, sd = cd_ref[...], sd_ref[...]
    selrm_ref[...] = rm("sel") * cq + rm("sel_sw") * sq
    winrm_ref[...] = rm("win") * cq + rm("win_sw") * sq
    for k in range(2):
        diffrm_ref[:, k * 128:(k + 1) * 128] = rm("diff", k) * cd + rm("kd_sw", k) * sd
    diffrm_ref[:, 256:512] = rm("diff", 2, 256)

    cmp_t, sel_t, win_t, sb_t, kd_t, vd_t = _proj_feature_major(
        xb, wt_ref, cta_ref, sta_ref, ctd_ref, std_ref)
    cmpt_ref[...] = cmp_t
    selt_ref[...] = sel_t
    wint_ref[...] = win_t
    sbt_ref[...] = sb_t
    difft_ref[0:128, :] = kd_t[0]
    difft_ref[128:256, :] = kd_t[1]
    difft_ref[256:512, :] = vd_t


def _proj_prompt(x, wrm, wt, tabs, gb, lng, lnb, ws, bsm):
    B, S, _ = x.shape
    tm, tk = TOKEN_TILE, KV_TILE
    nk = S // tk
    row = lambda n: pl.BlockSpec((None, tm, n), lambda b, i: (b, i, 0))
    feat = lambda n: pl.BlockSpec((None, n, tm), lambda b, i: (b, 0, i))
    tile = lambda n: pl.BlockSpec((None, tm // tk, n, tk), lambda b, i: (b, i, 0, 0))
    tab_rm = pl.BlockSpec((tm, 128), lambda b, i: (i, 0))
    tab_t = pl.BlockSpec((128, tm), lambda b, i: (0, i))
    in_specs = [row(D_MODEL), _const_spec(wrm.shape), _const_spec(wt.shape),
                tab_rm, tab_rm, tab_rm, tab_rm, tab_t, tab_t, tab_t, tab_t,
                _const_spec(gb.shape), _const_spec(lng.shape), _const_spec(lnb.shape),
                _const_spec(ws.shape), _const_spec(bsm.shape)]
    sd = jax.ShapeDtypeStruct
    out_shape = [sd((B, S, 512), BF16), sd((B, S, 256), BF16), sd((B, S, 256), BF16),
                 sd((B, S, 128), F32), sd((B, S, 256), BF16), sd((B, S, 128), F32),
                 sd((B, 128, S), F32), sd((B, 128, S), F32), sd((B, 128, S), F32),
                 sd((B, 512, S), F32), sd((B, 512, S), F32),
                 sd((B, nk, 128, tk), BF16), sd((B, nk, 128, tk), BF16),
                 sd((B, nk, 512, tk), BF16), sd((B, nk, 512, tk), BF16)]
    out_specs = [row(512), row(256), row(256), row(128), row(256), row(128),
                 feat(128), feat(128), feat(128), feat(512), feat(512),
                 tile(128), tile(128), tile(512), tile(512)]
    return pl.pallas_call(
        functools.partial(_proj_prompt_kernel, tm=tm, tk=tk),
        grid=(B, S // tm), in_specs=in_specs, out_specs=out_specs, out_shape=out_shape,
        compiler_params=_params(("parallel", "parallel")), name="proj_prompt",
    )(x, wrm, wt, *tabs, gb, lng, lnb, ws, bsm)


def _proj_sample(x, wrm, wt, tabs, gb, lng, lnb, w00, b0):
    nb = x.shape[0]
    full = lambda a: pl.BlockSpec(a.shape, lambda i: (0,) * a.ndim)
    ins = (x, wrm, wt, *tabs, gb, lng, lnb, w00, b0)
    sd = jax.ShapeDtypeStruct
    out_shape = [sd((nb, 512), BF16), sd((nb, 256), BF16), sd((nb, 256), BF16), sd((nb, 128), F32),
                 sd((nb, 256), BF16), sd((nb, 256), F32),
                 sd((nb, 128), F32), sd((nb, 128), F32), sd((nb, 512), F32),
                 sd((128, nb), F32), sd((128, nb), F32), sd((128, nb), F32),
                 sd((512, nb), F32), sd((512, nb), F32)]
    return pl.pallas_call(
        _proj_sample_kernel, grid=(1,), in_specs=[full(a) for a in ins],
        out_specs=[pl.BlockSpec(s.shape, lambda i: (0, 0)) for s in out_shape], out_shape=out_shape,
        compiler_params=_params(("arbitrary",)), name="proj_sample",
    )(*ins)


def _compress_rows(rows_ref, first, stride, n_out, pe_ref, w1_ref, w2_ref):
    h = jnp.zeros((n_out, 128), F32)
    for r in range(CMP_BLOCK):
        x = rows_ref[pl.ds(first + r, n_out, stride=stride), :] + pe_ref[r:r + 1, :]
        h = h + _dot(x.astype(BF16), w1_ref[r])
    return _dot(_gelu(h).astype(BF16), w2_ref[...])


def _compress_prompt_kernel(x_ref, pe_ref, w1_ref, w2_ref, o_ref, *, ns):
    for parity in range(2):
        o_ref[parity * ns:(parity + 1) * ns, :] = _compress_rows(
            x_ref, parity * CMP_BLOCK, SEL_BLOCK, ns, pe_ref, w1_ref, w2_ref).astype(BF16)


def _compress_prompt(cmp_rm, pe, w1, w2):
    B, S, _ = cmp_rm.shape
    ns = S // SEL_BLOCK
    return pl.pallas_call(
        functools.partial(_compress_prompt_kernel, ns=ns), grid=(B,),
        in_specs=[pl.BlockSpec((None, S, 128), lambda b: (b, 0, 0)), _const_spec(pe.shape),
                  _const_spec(w1.shape), _const_spec(w2.shape)],
        out_specs=pl.BlockSpec((None, 2 * ns, 128), lambda b: (b, 0, 0)),
        out_shape=jax.ShapeDtypeStruct((B, 2 * ns, 128), BF16),
        compiler_params=_params(("parallel",)), name="compress_prompt",
    )(cmp_rm, pe, w1, w2)


def _compress_pool_kernel(x_ref, pe_ref, w1_ref, w2_ref, o_ref, rows_ref, *, pages):
    for p in range(pages):
        rows_ref[p * 128:(p + 1) * 128, :] = x_ref[p].T
    n_out = pages * (128 // CMP_BLOCK)
    o_ref[...] = _compress_rows(rows_ref, 0, CMP_BLOCK, n_out, pe_ref, w1_ref, w2_ref)


def _compress_pool(layer, cmp_cache, pe, w1, w2):
    n_pool = cmp_cache.shape[1]
    pages = math.gcd(n_pool, 64)
    n_out = pages * (128 // CMP_BLOCK)
    return pl.pallas_call(
        functools.partial(_compress_pool_kernel, pages=pages), grid=(n_pool // pages,),
        in_specs=[pl.BlockSpec((None, pages, 128, 128), lambda g: (layer, g, 0, 0)), _const_spec(pe.shape),
                  _const_spec(w1.shape), _const_spec(w2.shape)],
        out_specs=pl.BlockSpec((n_out, 128), lambda g: (g, 0)),
        out_shape=jax.ShapeDtypeStruct((n_pool * (128 // CMP_BLOCK), 128), F32),
        scratch_shapes=[pltpu.VMEM((pages * 128, 128), F32)],
        compiler_params=_params(("parallel",)), name="compress_pool",
    )(cmp_cache, pe, w1, w2)


def _select_topk(score, blk_f, n_blocks, k):
    sel = jnp.zeros(score.shape, F32)
    work = score
    for _ in range(k):
        mx = jnp.max(work, axis=0, keepdims=True)
        idx = jnp.min(jnp.where(work == mx, blk_f, float(n_blocks)), axis=0, keepdims=True)
        pick = blk_f == idx
        sel = jnp.where(pick, 1.0, sel)
        work = jnp.where(pick, NEG_BIG, work)
    return sel


def _select_topk_row(score, k):
    n = score.shape[1]
    mine = jnp.broadcast_to(score, (n, n))
    other = mine.T
    beats = (other > mine) | ((other == mine) & (_iota((n, n), 0) < _iota((n, n), 1)))
    rank = jnp.sum(jnp.where(beats, 1.0, 0.0), axis=0, keepdims=True)
    return jnp.where(rank < k, 1.0, 0.0)


def _masked_softmax(logits, mask):
    lm = jnp.where(mask, logits, NEG_INF)
    mx = jnp.max(lm, axis=1, keepdims=True)
    e = jnp.where(mask, jnp.exp(lm - mx), 0.0)
    return e / jnp.maximum(jnp.sum(e, axis=1, keepdims=True), TINY)


def _cmp_block_end(n_half):
    r = _iota((1, 2 * n_half), 1)
    return jnp.where(r < n_half, 64 * r + 31, 64 * (r - n_half) + 63)


def _nsa_prompt_kernel(qf_ref, g_ref, kvc_ref, sel_ref, win_ref, o_ref, qs_ref, acc_ref, m_ref, l_ref, ex_ref,
                       bias_ref, *, tq, sub, ns, topk):
    tk = tq
    i = pl.program_id(1)
    tpos = i * tq + _iota((tq, 1), 0)
    qh = [qf_ref[:, h * 128:(h + 1) * 128] for h in range(N_HEADS)]
    kvc = kvc_ref[...]

    maskc = _cmp_block_end(ns) <= tpos
    oc, impc = [], jnp.zeros((tq, 2 * ns), F32)
    for h in range(N_HEADS):
        pc = _masked_softmax(_dot_nt(qh[h], kvc), maskc)
        oc.append(_dot(pc.astype(BF16), kvc))
        impc = impc + pc
    imp = impc[:, :ns] + impc[:, ns:]

    blk = _iota((ns, tq), 0)
    tpos_row = i * tq + _iota((1, tq), 1)
    cur = tpos_row >> 6
    forced = (blk == 0) | (blk == cur) | (blk == cur - 1)
    valid = (blk << 6) <= tpos_row
    score = jnp.where(forced, FORCE_SCORE, jnp.where(valid, imp.T, -1.0))
    sel_bf = _select_topk(score, blk.astype(F32), ns, topk).T.astype(BF16)

    row, col = _iota((tq, tk), 0), _iota((tq, tk), 1)

    def chosen_in_tile(j):
        expand = (_iota((ns, tk), 0) == j * (tk // SEL_BLOCK) + (_iota((ns, tk), 1) >> 6))
        return _dot(sel_bf, expand.astype(BF16)) > 0.5

    def sel_bias_step(j, carry):
        bias_ref[j] = jnp.where(chosen_in_tile(j), 0.0, NEG_INF).astype(BF16)
        return carry

    lax.fori_loop(0, i, sel_bias_step, 0)
    bias_ref[i] = jnp.where(chosen_in_tile(i) & (col <= row), 0.0, NEG_INF).astype(BF16)

    for h in range(N_HEADS):
        qs_ref[h * tq:(h + 1) * tq, :] = qh[h]

    def attend(kv_ref, j_lo, bias_of):
        def scores(j):
            kt = _key_rows(kv_ref, j, sub, 0, 128)
            s = _dot(qs_ref[...], kt).reshape(N_HEADS, tq, tk) + bias_of(j)[None]
            return s.reshape(N_HEADS * tq, tk), kt

        def max_step(j, carry):
            s, _ = scores(j)
            m_ref[...] = jnp.maximum(m_ref[...], _chunk_reduce(s, jnp.maximum))
            return carry

        def sum_step(j, carry, track=False):
            s, kt = scores(j)
            m = m_ref[...]
            d = [c - m for c in _lane_chunks(s)]
            if track:
                ex_ref[...] = jnp.maximum(ex_ref[...], functools.reduce(jnp.maximum, d))
            p = [jnp.exp(c) for c in d]
            l_ref[...] += functools.reduce(jnp.add, p)
            acc_ref[...] += _dot_nt(jnp.concatenate(p, axis=1).astype(BF16), kt)
            return carry

        def reset_sums():
            acc_ref[...] = jnp.zeros(acc_ref.shape, F32)
            l_ref[...] = jnp.zeros(l_ref.shape, F32)

        def row_max_to_lanes():
            m_ref[...] = jnp.broadcast_to(jnp.max(m_ref[...], axis=1, keepdims=True), m_ref.shape)

        m_ref[...] = jnp.full(m_ref.shape, NEG_INF, F32)
        max_step(i, 0)
        row_max_to_lanes()
        reset_sums()
        ex_ref[...] = jnp.zeros(ex_ref.shape, F32)
        lax.fori_loop(j_lo, i, functools.partial(sum_step, track=True), 0)
        sum_step(i, 0)

        @pl.when(jnp.max(ex_ref[...]) > EXP_HEADROOM)
        def _():
            m_ref[...] = jnp.full(m_ref.shape, NEG_INF, F32)
            lax.fori_loop(j_lo, i + 1, max_step, 0)
            row_max_to_lanes()
            reset_sums()
            lax.fori_loop(j_lo, i + 1, sum_step, 0)

        out = acc_ref[...] / jnp.maximum(jnp.sum(l_ref[...], axis=1, keepdims=True), TINY)
        return [out[h * tq:(h + 1) * tq, :] for h in range(N_HEADS)]

    def win_bias(j):
        d = (i - j) * tk + row - col
        return jnp.where((d >= 0) & (d < WINDOW), 0.0, NEG_INF)

    osel = attend(sel_ref, 0, lambda j: bias_ref[j].astype(F32))
    ow = attend(win_ref, jnp.maximum(i - WINDOW // tk, 0), win_bias)

    g = g_ref[...]
    lane = _iota((tq, 128), 1)
    mixed = [g[:, 3 * h:3 * h + 1] * oc[h] + g[:, 3 * h + 1:3 * h + 2] * osel[h]
             + g[:, 3 * h + 2:3 * h + 3] * ow[h] for h in range(N_HEADS)]
    for k in range(2):
        even = pltpu.roll(mixed[2 * k], 64, 1)
        o_ref[:, k * 128:(k + 1) * 128] = jnp.where(lane < 64, even, mixed[2 * k + 1]).astype(BF16)


def _nsa_prompt(qaf, gates, kvc, selbf, winbf):
    B, S, _ = qaf.shape
    tq = math.gcd(S, BIG_Q_TILE)
    ns = S // SEL_BLOCK
    kern = functools.partial(_nsa_prompt_kernel, tq=tq, sub=tq // KV_TILE, ns=ns, topk=min(SEL_TOPK, ns))
    row = lambda n: pl.BlockSpec((None, tq, n), lambda b, i: (b, i, 0))
    whole = lambda a: pl.BlockSpec((None,) + a.shape[1:], lambda b, i: (b,) + (0,) * (a.ndim - 1))
    return pl.pallas_call(
        kern, grid=(B, S // tq),
        in_specs=[row(512), row(128), whole(kvc), whole(selbf), whole(winbf)],
        out_specs=row(256), out_shape=jax.ShapeDtypeStruct((B, S, 256), BF16),
        scratch_shapes=[pltpu.VMEM((N_HEADS * tq, 128), BF16), pltpu.VMEM((N_HEADS * tq, 128), F32),
                        pltpu.VMEM((N_HEADS * tq, 128), F32), pltpu.VMEM((N_HEADS * tq, 128), F32),
                        pltpu.VMEM((N_HEADS * tq, 128), F32), pltpu.VMEM((S // tq, tq, tq), BF16)],
        compiler_params=_params(("parallel", "arbitrary")), name="nsa_prompt",
    )(qaf, gates, kvc, selbf, winbf)


def _sb_prompt_kernel(q_ref, kv_ref, o_ref, qs_ref, acc_ref, car_ref, *, tq, tk):
    assert tq == tk
    i = pl.program_id(1)
    rows = N_HEADS * tq
    q = q_ref[...].astype(F32)
    lane_head = _iota((1, GROUP_WIDTH), 1) >> 6
    for h in range(N_HEADS):
        qs_ref[h * tq:(h + 1) * tq, :] = jnp.where(lane_head == h, q, 0.0).astype(BF16)
    suffix = _suffix_matrix(tk)
    acc_ref[...] = jnp.zeros(acc_ref.shape, F32)
    car_ref[...] = jnp.zeros(car_ref.shape, F32)

    def step(j, strict):
        z = _dot(qs_ref[...], kv_ref[j, 0:256, :])
        log_keep = -_softplus(z)
        if strict is not None:
            keep = jnp.broadcast_to(strict[None], (N_HEADS, tq, tk)).reshape(rows, tk)
            log_keep = jnp.where(keep, log_keep, 0.0)
            z = jnp.where(keep, z, NEG_INF)
        hi, lo = _split_bf16(log_keep)
        incl = _dot(hi, suffix) + _dot(lo, suffix)
        car = car_ref[...]
        ab = jnp.exp(z + incl + jnp.concatenate([car, car], axis=1)).astype(BF16)
        for k in range(2):
            pair = slice(2 * k * tq, (2 * k + 2) * tq)
            acc_ref[pair, :] += _dot_nt(ab[pair, :], kv_ref[j, 256 + k * 128:384 + k * 128, :])
        car_ref[...] = car + jnp.broadcast_to(incl[:, 0:1], (rows, 128))

    def any_weight_left():
        return jnp.max(car_ref[...]) > SB_UNDERFLOW

    step(i, _iota((tq, tk), 1) < _iota((tq, tk), 0))

    def left_tile(state):
        j, _ = state
        step(j, None)
        return j - 1, any_weight_left()

    lax.while_loop(lambda state: (state[0] >= 0) & state[1], left_tile, (i - 1, any_weight_left()))
    lane = _iota((tq, 128), 1)
    for k in range(2):
        even, odd = acc_ref[2 * k * tq:(2 * k + 1) * tq, :], acc_ref[(2 * k + 1) * tq:(2 * k + 2) * tq, :]
        o_ref[:, k * 128:(k + 1) * 128] = jnp.where(lane < 64, even, odd).astype(BF16)


def _sb_prompt(qb, sbbf):
    B, S, _ = qb.shape
    tq, tk = Q_TILE, KV_TILE
    row = lambda n: pl.BlockSpec((None, tq, n), lambda b, i: (b, i, 0))
    return pl.pallas_call(
        functools.partial(_sb_prompt_kernel, tq=tq, tk=tk), grid=(B, S // tq),
        in_specs=[row(256), pl.BlockSpec((None,) + sbbf.shape[1:], lambda b, i: (b, 0, 0, 0))],
        out_specs=row(256), out_shape=jax.ShapeDtypeStruct((B, S, 256), BF16),
        scratch_shapes=[pltpu.VMEM((N_HEADS * tq, GROUP_WIDTH), BF16), pltpu.VMEM((N_HEADS * tq, 128), F32),
                        pltpu.VMEM((N_HEADS * tq, 128), F32)],
        compiler_params=_params(("parallel", "arbitrary")), name="sb_prompt",
    )(qb, sbbf)


def _diff_lambda(dl_ref, lam_init):
    lq = dl_ref[...]
    s1 = jnp.sum(lq[0:1, :] * lq[1:2, :], axis=1, keepdims=True)
    s2 = jnp.sum(lq[2:3, :] * lq[3:4, :], axis=1, keepdims=True)
    return jnp.exp(s1) - jnp.exp(s2) + lam_init


def _head_rms_norm(out, lane_head, ng, lam_init):
    sq = out * out
    rs = jnp.zeros(out.shape, F32)
    for h in range(N_HEADS):
        ms = jnp.sum(jnp.where(lane_head == h, sq, 0.0), axis=1, keepdims=True) * (1.0 / HEAD_DIM)
        rs = jnp.where(lane_head == h, lax.rsqrt(ms + LN_EPS), rs)
    return out * rs * ng * (1.0 - lam_init)


def _lane_chunks(x):
    return [x[:, c * LANES:(c + 1) * LANES] for c in range(x.shape[1] // LANES)]


def _chunk_reduce(x, op):
    chunks = _lane_chunks(x)
    out = chunks[0]
    for c in chunks[1:]:
        out = op(out, c)
    return out


def _key_rows(kv_ref, j, sub, r0, r1):
    return jnp.concatenate([kv_ref[j * sub + u, r0:r1, :] for u in range(sub)], axis=1)


def _diff_prompt_kernel(li_ref, dl_ref, ng_ref, q_ref, kv_ref, o_ref, qs_ref, acc_ref, m_ref, l_ref, ex_ref,
                        *, tq, sub):
    tk = tq
    i = pl.program_id(1)
    lam_init = li_ref[0]
    lam = _diff_lambda(dl_ref, lam_init)
    q = q_ref[...].astype(F32)
    lane = _iota((1, GROUP_WIDTH), 1)
    lane_head, lane_half = lane >> 6, (lane >> 5) & 1
    n_soft = 2 * N_HEADS
    rows = n_soft * tq
    for a in range(n_soft):
        qa = jnp.where((lane_head == a // 2) & (lane_half == a % 2), q, 0.0)
        qs_ref[a * tq:(a + 1) * tq, :] = qa.astype(BF16)
    causal_bias = jnp.where(_iota((tq, tk), 1) <= _iota((tq, tk), 0), 0.0, NEG_INF)

    def scores(j, bias):
        s = _dot(qs_ref[...], _key_rows(kv_ref, j, sub, 0, 256))
        if bias is not None:
            s = (s.reshape(n_soft, tq, tk) + bias[None]).reshape(rows, tk)
        return s

    def max_step(j, bias):
        m_ref[...] = jnp.maximum(m_ref[...], _chunk_reduce(scores(j, bias), jnp.maximum))

    def sum_step(j, bias, s=None, track=False):
        s = scores(j, bias) if s is None else s
        m = m_ref[...]
        d = [c - m for c in _lane_chunks(s)]
        if track:
            ex_ref[...] = jnp.maximum(ex_ref[...], functools.reduce(jnp.maximum, d))
        p = [jnp.exp(c) for c in d]
        l_ref[...] += functools.reduce(jnp.add, p)
        pb = jnp.concatenate(p, axis=1).astype(BF16)
        for k in range(2):
            part = slice(k * (rows // 2), (k + 1) * (rows // 2))
            acc_ref[part, :] += _dot_nt(pb[part, :], _key_rows(kv_ref, j, sub, 256 + k * 128, 384 + k * 128))

    def below(step):
        def body(j, carry):
            step(j, None)
            return carry
        lax.fori_loop(0, i, body, 0)

    def reset_sums():
        acc_ref[...] = jnp.zeros(acc_ref.shape, F32)
        l_ref[...] = jnp.zeros(l_ref.shape, F32)

    def row_max_to_lanes():
        m_ref[...] = jnp.broadcast_to(jnp.max(m_ref[...], axis=1, keepdims=True), m_ref.shape)

    s_diag = scores(i, causal_bias)
    m_ref[...] = _chunk_reduce(s_diag, jnp.maximum)
    row_max_to_lanes()
    reset_sums()
    ex_ref[...] = jnp.zeros(ex_ref.shape, F32)
    sum_step(i, None, s=s_diag)
    below(functools.partial(sum_step, track=True))

    @pl.when(jnp.max(ex_ref[...]) > EXP_HEADROOM)
    def _():
        m_ref[...] = jnp.full(m_ref.shape, NEG_INF, F32)
        below(max_step)
        max_step(i, causal_bias)
        row_max_to_lanes()
        reset_sums()
        below(sum_step)
        sum_step(i, causal_bias)

    lane128 = _iota((tq, 128), 1)
    ng = ng_ref[...]
    blk = lambda ref, a: ref[a * tq:(a + 1) * tq, :]
    for k in range(2):
        o = []
        for h in (2 * k, 2 * k + 1):
            l1 = jnp.maximum(jnp.sum(blk(l_ref, 2 * h), axis=1, keepdims=True), TINY)
            l2 = jnp.maximum(jnp.sum(blk(l_ref, 2 * h + 1), axis=1, keepdims=True), TINY)
            o.append(blk(acc_ref, 2 * h) / l1 - lam * (blk(acc_ref, 2 * h + 1) / l2))
        low = lane128 < HEAD_DIM
        out = jnp.where(low, o[0], o[1])
        sq = out * out
        ms_lo = jnp.sum(jnp.where(low, sq, 0.0), axis=1, keepdims=True)
        ms_hi = jnp.sum(jnp.where(low, 0.0, sq), axis=1, keepdims=True)
        rs = lax.rsqrt(jnp.where(low, ms_lo, ms_hi) * (1.0 / HEAD_DIM) + LN_EPS)
        o_ref[:, k * 128:(k + 1) * 128] = (out * rs * ng[:, k * 128:(k + 1) * 128] * (1.0 - lam_init)).astype(BF16)


def _diff_prompt(lam_init, dl, ng, qd, diffbf):
    B, S, _ = qd.shape
    tq = math.gcd(S, BIG_Q_TILE)
    row = lambda n: pl.BlockSpec((None, tq, n), lambda b, i: (b, i, 0))
    return pl.pallas_call(
        functools.partial(_diff_prompt_kernel, tq=tq, sub=tq // KV_TILE), grid=(B, S // tq),
        in_specs=[pl.BlockSpec(memory_space=pltpu.SMEM), _const_spec(dl.shape), _const_spec(ng.shape),
                  row(256), pl.BlockSpec((None,) + diffbf.shape[1:], lambda b, i: (b, 0, 0, 0))],
        out_specs=row(256), out_shape=jax.ShapeDtypeStruct((B, S, 256), BF16),
        scratch_shapes=[pltpu.VMEM((2 * N_HEADS * tq, GROUP_WIDTH), BF16), pltpu.VMEM((2 * N_HEADS * tq, 128), F32),
                        pltpu.VMEM((2 * N_HEADS * tq, 128), F32), pltpu.VMEM((2 * N_HEADS * tq, 128), F32),
                        pltpu.VMEM((2 * N_HEADS * tq, 128), F32)],
        compiler_params=_params(("parallel", "arbitrary")), name="diff_prompt",
    )(lam_init, dl, ng, qd, diffbf)


def _round_bf16(x):
    return x.astype(BF16).astype(F32)


def _softmax_with_self(s, ok, s_self):
    sm = jnp.where(ok, s, NEG_INF)
    mx = jnp.maximum(jnp.max(sm, axis=1, keepdims=True), s_self)
    e = jnp.where(ok, jnp.exp(sm - mx), 0.0)
    e_self = jnp.exp(s_self - mx)
    inv = 1.0 / (jnp.sum(e, axis=1, keepdims=True) + e_self)
    return e * inv, e_self * inv


def _nsa_sample_kernel(pt_ref, qh_ref, gs_ref, snew_ref, wnew_ref, win_ref, *rest,
                       n_pages, past, half, topk):
    del pt_ref
    cmp_refs, sel_refs = rest[:n_pages], rest[n_pages:2 * n_pages]
    o_ref, kvc_ref = rest[2 * n_pages], rest[2 * n_pages + 1]
    qh = qh_ref[...]
    qf = qh.astype(F32)

    kvc_ref[...] = jnp.zeros(kvc_ref.shape, F32)
    for p in range(n_pages):
        piece = cmp_refs[p][...]
        kvc_ref[2 * p:2 * p + 2, :] = piece[0:2, :]
        kvc_ref[half + 2 * p:half + 2 * p + 2, :] = piece[2:4, :]
    kvc = kvc_ref[...].astype(BF16)
    maskc = _cmp_block_end(half) <= past
    pc = _masked_softmax(_dot_nt(qh, kvc), maskc)
    oc = _dot(pc.astype(BF16), kvc)
    head_row = _iota((8, 1), 0) < N_HEADS
    impc = jnp.sum(jnp.where(head_row, pc, 0.0), axis=0, keepdims=True)
    imp = impc[:, :half] + impc[:, half:]

    n_sel = past // SEL_BLOCK + 1
    blk = _iota((1, half), 1)
    cur = past // SEL_BLOCK
    forced = (blk == 0) | (blk == cur) | (blk == cur - 1)
    valid = blk * SEL_BLOCK <= past
    score = jnp.where(blk < n_sel, jnp.where(forced, FORCE_SCORE, jnp.where(valid, imp, -1.0)), NEG_BIG)
    sel8 = jnp.broadcast_to(_select_topk_row(score, topk), (8, half)).astype(BF16)
    expand = _iota((half, past), 0) == (_iota((half, past), 1) >> 6)
    chosen = _dot(sel8, expand.astype(BF16)) > 0.5

    def branch(kv, new_row, ok):
        new_b = _round_bf16(new_row)
        s_self = jnp.sum(qf * new_b, axis=1, keepdims=True)
        p, p_self = _softmax_with_self(_dot(qh, kv), ok, s_self)
        return p_self * new_b + _dot_nt(p.astype(BF16), kv)

    osel = branch(jnp.concatenate([r[...].astype(BF16) for r in sel_refs], axis=1), snew_ref[...], chosen)
    w_past = win_ref.shape[1]
    in_window = _iota((1, w_past), 1) >= (w_past - WINDOW + 1)
    ow = branch(win_ref[...].astype(BF16), wnew_ref[...], in_window)
    g = gs_ref[...]
    o_ref[...] = g[:, 0:1] * oc + g[:, 1:2] * osel + g[:, 2:3] * ow


def _sb_sample_kernel(pt_ref, q_ref, *rest, n_pages, past, cb):
    del pt_ref
    pages, o_ref = rest[:n_pages], rest[n_pages]
    q8 = jnp.broadcast_to(q_ref[...].astype(F32), (8, GROUP_WIDTH))
    row_head = _iota((8, GROUP_WIDTH), 0) == (_iota((8, GROUP_WIDTH), 1) >> 6)
    qm = jnp.where(row_head, q8, 0.0).astype(BF16)
    kt = jnp.concatenate([pages[p][0].astype(BF16) for p in range(n_pages)], axis=1)
    vt = jnp.concatenate([pages[p][1].astype(BF16) for p in range(n_pages)], axis=1)
    z = _dot(qm, kt)
    log_keep = -_softplus(z)
    suffix = _suffix_matrix(cb)
    carry = jnp.zeros((8, 1), F32)
    expo = [None] * (past // cb)
    for bi in reversed(range(past // cb)):
        cols = slice(bi * cb, (bi + 1) * cb)
        hi, lo = _split_bf16(log_keep[:, cols])
        incl = _dot(hi, suffix) + _dot(lo, suffix)
        expo[bi] = z[:, cols] + incl + carry
        carry = carry + incl[:, 0:1]
    out = _dot_nt(jnp.exp(jnp.concatenate(expo, axis=1)).astype(BF16), vt)
    o_ref[...] = jnp.sum(jnp.where(row_head, out, 0.0), axis=0, keepdims=True).astype(BF16)


def _diff_sample_kernel(pt_ref, li_ref, dl_ref, ng_ref, q_ref, dnew_ref, *rest, n_pages):
    del pt_ref
    pages, o_ref = rest[:n_pages], rest[n_pages]
    lam_init = li_ref[0]
    lam = _diff_lambda(dl_ref, lam_init)
    q8 = jnp.broadcast_to(q_ref[...].astype(F32), (8, GROUP_WIDTH))
    lane = _iota((8, GROUP_WIDTH), 1)
    row_head = _iota((8, GROUP_WIDTH), 0) == (lane >> 6)
    lane_half = (lane >> 5) & 1
    new_k = _round_bf16(dnew_ref[:, 0:256])
    new_v = _round_bf16(dnew_ref[:, 256:512])
    kt = jnp.concatenate([pages[p][0].astype(BF16) for p in range(n_pages)], axis=1)
    vt = jnp.concatenate([pages[p][1].astype(BF16) for p in range(n_pages)], axis=1)
    ps, ps_self = [], []
    for m in range(2):
        qm = jnp.where(row_head & (lane_half == m), q8, 0.0)
        s = _dot(qm.astype(BF16), kt)
        s_self = jnp.sum(qm * new_k, axis=1, keepdims=True)
        p, p_self = _softmax_with_self(s, jnp.full(s.shape, True), s_self)
        ps.append(p)
        ps_self.append(p_self)
    out = (ps_self[0] - lam * ps_self[1]) * new_v + _dot_nt((ps[0] - lam * ps[1]).astype(BF16), vt)
    o4 = jnp.sum(jnp.where(row_head, out, 0.0), axis=0, keepdims=True)
    o_ref[...] = _head_rms_norm(o4, _iota((1, GROUP_WIDTH), 1) >> 6, ng_ref[...], lam_init).astype(BF16)


def _page_spec(block, layer, p):
    nz = (0,) * (len(block) - 2)
    return pl.BlockSpec(block, lambda b, pt: (layer, pt[b, p]) + nz)


def _sample_mix_kernel(pt_ref, qh_ref, gs_ref, snew_ref, wnew_ref, win_ref, qb_ref, li_ref, dl_ref, ng_ref,
                       qd_ref, dnew_ref, *rest, n_pages, past, half, topk, cb):
    n = n_pages
    cmp_refs, sel_refs, sb_refs, diff_refs = rest[:n], rest[n:2 * n], rest[2 * n:3 * n], rest[3 * n:4 * n]
    oa_ref, ob_ref, oc_ref, kvc_ref = rest[4 * n:4 * n + 4]
    _nsa_sample_kernel(pt_ref, qh_ref, gs_ref, snew_ref, wnew_ref, win_ref, *cmp_refs, *sel_refs, oa_ref, kvc_ref,
                       n_pages=n, past=past, half=half, topk=topk)
    _sb_sample_kernel(pt_ref, qb_ref, *sb_refs, ob_ref, n_pages=n, past=past, cb=cb)
    _diff_sample_kernel(pt_ref, li_ref, dl_ref, ng_ref, qd_ref, dnew_ref, *diff_refs, oc_ref, n_pages=n)


def _sample_mix(layer, page_table, qh, gs, snew, wnew, cmp_pool, sel_cache, win_cache,
                qb, sb_cache, lam_init, dl, ng, qd, dnew, diff_cache):
    nb, n_pages = page_table.shape
    past = n_pages * 128
    half = max(128, -(-(past // SEL_BLOCK + 1) // 128) * 128)
    n_sel = past // SEL_BLOCK + 1
    kern = functools.partial(_sample_mix_kernel, n_pages=n_pages, past=past, half=half,
                             topk=min(SEL_TOPK, n_sel), cb=math.gcd(past, 256))
    per_seq = lambda a: pl.BlockSpec((None,) + a.shape[1:], lambda b, pt: (b,) + (0,) * (a.ndim - 1))
    const = lambda a: pl.BlockSpec(a.shape, lambda b, pt: (0,) * a.ndim)
    in_specs = [per_seq(qh), per_seq(gs), per_seq(snew), per_seq(wnew),
                pl.BlockSpec((None, None) + win_cache.shape[2:], lambda b, pt: (layer, b, 0, 0)),
                per_seq(qb), pl.BlockSpec(memory_space=pltpu.SMEM), const(dl), const(ng),
                per_seq(qd), per_seq(dnew)]
    in_specs += [pl.BlockSpec((None, 4, 128), functools.partial(lambda b, pt, p: (pt[b, p], 0, 0), p=p))
                 for p in range(n_pages)]
    in_specs += [_page_spec((None, None, 128, 128), layer, p) for p in range(n_pages)]
    in_specs += [_page_spec((None, None, 2, 256, 128), layer, p) for p in range(n_pages)]
    in_specs += [_page_spec((None, None, 2, 256, 128), layer, p) for p in range(n_pages)]
    row_out = pl.BlockSpec((None, 1, 256), lambda b, pt: (b, 0, 0))
    grid_spec = pltpu.PrefetchScalarGridSpec(
        num_scalar_prefetch=1, grid=(nb,), in_specs=in_specs,
        out_specs=[pl.BlockSpec((None, 8, 128), lambda b, pt: (b, 0, 0)), row_out, row_out],
        scratch_shapes=[pltpu.VMEM((2 * half, 128), F32)])
    return pl.pallas_call(
        kern, grid_spec=grid_spec,
        out_shape=[jax.ShapeDtypeStruct((nb, 8, 128), F32), jax.ShapeDtypeStruct((nb, 1, 256), BF16),
                   jax.ShapeDtypeStruct((nb, 1, 256), BF16)],
        compiler_params=_params(("arbitrary",)), name="sample_mix",
    )(page_table, qh, gs, snew, wnew, win_cache, qb, lam_init, dl, ng, qd, dnew,
      *([cmp_pool] * n_pages), *([sel_cache] * n_pages), *([sb_cache] * n_pages), *([diff_cache] * n_pages))


def _tail_kernel(x_ref, oa_ref, ob_ref, oc_ref, od_ref, wo_ref, g1_ref, b1_ref,
                 wg_ref, wu_ref, wd_ref, g2_ref, b2_ref, y_ref, *, alpha):
    mix = (_dot(oa_ref[...], wo_ref[0:256, :]) + _dot(ob_ref[...], wo_ref[256:512, :])
           + _dot(oc_ref[...], wo_ref[512:768, :]) + _dot(od_ref[...], wo_ref[768:1024, :]))
    x1 = _layer_norm(alpha * x_ref[...] + mix, g1_ref[...], b1_ref[...])
    x1b = x1.astype(BF16)
    hg = _dot(x1b, wg_ref[...])
    hu = _dot(x1b, wu_ref[...])
    act = (hg * _sigmoid(hg) * hu).astype(BF16)
    y_ref[...] = _layer_norm(alpha * x1 + _dot(act, wd_ref[...]), g2_ref[...], b2_ref[...])


def _tail(x, oa, ob, oc, od, wo, g1, b1, wg, wu, wd, g2, b2, alpha):
    m = x.shape[0]
    tm = math.gcd(m, TOKEN_TILE)
    row = lambda n: pl.BlockSpec((tm, n), lambda i: (i, 0))
    consts = (wo, g1, b1, wg, wu, wd, g2, b2)
    return pl.pallas_call(
        functools.partial(_tail_kernel, alpha=alpha), grid=(m // tm,),
        in_specs=[row(D_MODEL), row(256), row(256), row(256), row(256)] + [_const_spec(c.shape) for c in consts],
        out_specs=row(D_MODEL), out_shape=jax.ShapeDtypeStruct((m, D_MODEL), F32),
        compiler_params=_params(("parallel",)), name="tail",
    )(x, oa, ob, oc, od, *consts)


def _rot_partner(w, group, half):
    k = w.shape[0]
    wr = w.reshape(k, -1, group)
    x1, x2 = wr[..., :half], wr[..., half:2 * half]
    return jnp.concatenate([-x2, x1, jnp.zeros_like(wr[..., 2 * half:])], axis=-1).reshape(k, -1)


def _pad_heads(w):
    k = w.shape[0]
    wr = w.reshape(k, -1, HEAD_DIM)
    return jnp.concatenate([wr, jnp.zeros_like(wr)], axis=-1).reshape(k, -1)


def _layer_weights(w, sample):
    off = [0]
    for n in (256, 64, 64, 64, 64, 64, 64, 12, 256, 256, 256, 256, 256, 256, 256, 256):
        off.append(off[-1] + n)
    col = lambda i: w[:, off[i]:off[i + 1]]
    qa, kc, vc, ks, vs, kw, vw, ga, qb, kb, vb, qd, kd, vd, u, v = [col(i) for i in range(16)]
    zeros64 = jnp.zeros((w.shape[0], 64), F32)
    qa = qa * (HEAD_DIM ** -0.5)
    qb = qb * (HEAD_DIM ** -0.5)
    qd = qd * (DIFF_QK ** -0.5)
    sel = jnp.concatenate([ks, vs], 1)
    sel_sw = jnp.concatenate([_rot_partner(ks, HEAD_DIM, ROT_HALF), zeros64], 1)
    win = jnp.concatenate([kw, vw], 1)
    win_sw = jnp.concatenate([_rot_partner(kw, HEAD_DIM, ROT_HALF), zeros64], 1)
    kd_sw = _rot_partner(kd, DIFF_QK, DIFF_ROT_HALF)
    rm = [_pad_heads(qa), _pad_heads(_rot_partner(qa, HEAD_DIM, ROT_HALF)), qb, qd,
          _rot_partner(qd, DIFF_QK, DIFF_ROT_HALF), u, v,
          jnp.concatenate([ga, jnp.zeros((w.shape[0], 128 - ga.shape[1]), F32)], 1),
          jnp.concatenate([kc, vc], 1)]
    if sample:
        rm += [sel, sel_sw, win, win_sw, kd, vd, kd_sw]
    tr = [jnp.concatenate([kc, vc], 1), sel, sel_sw, win, win_sw, kb, vb, kd, vd, kd_sw]
    return (jnp.concatenate(rm, 1).astype(BF16), jnp.concatenate(tr, 1).T.astype(BF16))


def _rope_tables(pos):
    pos = pos.astype(F32)[:, None]
    n = pos.shape[0]

    def table(half, plain, reps):
        freq = ROPE_THETA ** (-jnp.arange(half, dtype=F32) / half)
        ang = pos * freq[None, :]
        c = jnp.concatenate([jnp.cos(ang), jnp.cos(ang), jnp.ones((n, plain), F32)], 1)
        s = jnp.concatenate([jnp.sin(ang), jnp.sin(ang), jnp.zeros((n, plain), F32)], 1)
        return jnp.tile(c, (1, reps)), jnp.tile(s, (1, reps))

    ca, sa = table(ROT_HALF, 128 - 2 * ROT_HALF, 1)
    cd, sd = table(DIFF_ROT_HALF, DIFF_QK - 2 * DIFF_ROT_HALF, 4)
    return (ca, sa, cd, sd, ca.T, sa.T, cd.T, sd.T)


def _compress_weights(pe, w1, w2):
    z = jnp.zeros((CMP_BLOCK, HEAD_DIM, HEAD_DIM), F32)
    w1k = w1[0].reshape(CMP_BLOCK, HEAD_DIM, HEAD_DIM)
    w1v = w1[1].reshape(CMP_BLOCK, HEAD_DIM, HEAD_DIM)
    w1c = jnp.concatenate([jnp.concatenate([w1k, z], 2), jnp.concatenate([z, w1v], 2)], 1)
    z2 = jnp.zeros((HEAD_DIM, HEAD_DIM), F32)
    w2c = jnp.concatenate([jnp.concatenate([w2[0], z2], 1), jnp.concatenate([z2, w2[1]], 1)], 0)
    pe_rows = jnp.concatenate([pe[0], pe[1]], 1)
    return pe_rows, w1c.astype(BF16), w2c.astype(BF16)


def _feature_major(x, tail_shape):
    lead = x.shape[:-2]
    n = x.shape[-1]
    y = x.reshape(lead + tail_shape + (n,))
    nd = y.ndim
    return jnp.transpose(y, tuple(range(len(lead))) + (nd - 1,) + tuple(range(len(lead), nd - 1)))


def kernel(x_prompt, x_sample, cache_nsa_cmp, cache_nsa_sel, cache_nsa_win, cache_sb, cache_diff,
           page_table, w_in, gate_b, cmp_pe, cmp_w1, cmp_w2, diff_lam, diff_norm_g, gmlp_ln_g,
           gmlp_ln_b, gmlp_ws, gmlp_bs, w_out, ln1_g, ln1_b, w_gate, w_up, w_down, ln2_g, ln2_b):
    B, S, _ = x_prompt.shape
    nb, T, _ = x_sample.shape
    depth = w_in.shape[0]
    n_pool = cache_sb.shape[1]
    n_pages = page_table.shape[1]
    past = n_pages * cache_sb.shape[2]
    w_past = cache_nsa_win.shape[2]
    assert T == 1 and cache_sb.shape[2] == 128 and S % Q_TILE == 0 and past % SEL_BLOCK == 0
    alpha = (2 * depth) ** 0.25

    to_pages = lambda c: jnp.transpose(c, (0, 1, 3, 4, 5, 2))
    cmp_cache = to_pages(cache_nsa_cmp).reshape(depth, n_pool, 128, 128)
    sel_cache = to_pages(cache_nsa_sel).reshape(depth, n_pool, 128, 128)
    win_cache = to_pages(cache_nsa_win).reshape(depth, nb, 128, w_past)
    sb_cache = to_pages(cache_sb).reshape(depth, n_pool, 2, 256, 128)
    diff_cache = to_pages(cache_diff).reshape(depth, n_pool, 2, 256, 128)

    tabs_p = _rope_tables(jnp.arange(S, dtype=jnp.int32))
    tabs_s = _rope_tables(jnp.full((nb,), past, jnp.int32))
    row = lambda a: a.reshape(1, -1)

    xp, xs = x_prompt, x_sample.reshape(nb, D_MODEL)
    outs = [[] for _ in range(11)]
    for l in range(depth):
        lam_init = jnp.full((1,), 0.8 - 0.6 * math.exp(-0.3 * l), F32)
        wrm_p, wt = _layer_weights(w_in[l], sample=False)
        wrm_s, _ = _layer_weights(w_in[l], sample=True)
        gb = jnp.concatenate([gate_b[l], jnp.zeros((128 - gate_b.shape[1],), F32)]).reshape(1, 128)
        lng, lnb = row(gmlp_ln_g[l]), row(gmlp_ln_b[l])
        bsm = jnp.repeat(gmlp_bs[l].T, HEAD_DIM, axis=1)
        w00 = row(jnp.repeat(gmlp_ws[l][:, 0, 0], HEAD_DIM))
        b0 = row(jnp.repeat(gmlp_bs[l][:, 0], HEAD_DIM))
        pe_rows, w1c, w2c = _compress_weights(cmp_pe[l], cmp_w1[l], cmp_w2[l])
        ng = row(jnp.tile(diff_norm_g[l], N_HEADS))
        tail_w = (w_out[l].astype(BF16), row(ln1_g[l]), row(ln1_b[l]), w_gate[l].astype(BF16),
                  w_up[l].astype(BF16), w_down[l].astype(BF16), row(ln2_g[l]), row(ln2_b[l]))

        (qaf, qb, qd, gates, od, cmp_rm, cmp_t, sel_t, win_t, sb_t, diff_t,
         selbf, winbf, sbbf, diffbf) = _proj_prompt(xp, wrm_p, wt, tabs_p, gb, lng, lnb, gmlp_ws[l], bsm)
        kvc = _compress_prompt(cmp_rm, pe_rows, w1c, w2c)
        oa = _nsa_prompt(qaf, gates, kvc, selbf, winbf)
        ob = _sb_prompt(qb, sbbf)
        oc = _diff_prompt(lam_init, diff_lam[l], ng, qd, diffbf)
        flat = lambda a: a.reshape(B * S, a.shape[-1])
        xp = _tail(flat(xp), flat(oa), flat(ob), flat(oc), flat(od), *tail_w, alpha).reshape(B, S, D_MODEL)
        outs[0].append(_feature_major(cmp_t, (2, 1, HEAD_DIM)))
        outs[2].append(_feature_major(sel_t, (2, 1, HEAD_DIM)))
        outs[4].append(_feature_major(win_t[:, :, S - min(WINDOW, S):], (2, 1, HEAD_DIM)))
        outs[6].append(_feature_major(sb_t, (2, N_HEADS, HEAD_DIM)))
        outs[8].append(_feature_major(diff_t, (2, N_HEADS, HEAD_DIM)))

        (qaf, qb, qd, gates, od, vn, sel_rm, win_rm, diff_rm,
         cmp_t, sel_t, win_t, sb_t, diff_t) = _proj_sample(xs, wrm_s, wt, tabs_s, gb, lng, lnb, w00, b0)
        cmp_pool = _compress_pool(l, cmp_cache, pe_rows, w1c, w2c).reshape(n_pool, 4, 128)[:, jnp.array([0, 2, 1, 3])]
        qh = jnp.pad(qaf.reshape(nb, N_HEADS, 128), ((0, 0), (0, 8 - N_HEADS), (0, 0)))
        gs = jnp.pad(gates[:, :3 * N_HEADS].reshape(nb, N_HEADS, 3), ((0, 0), (0, 8 - N_HEADS), (0, 125)))
        oa, ob, oc = _sample_mix(
            l, page_table, qh, gs, sel_rm.reshape(nb, 1, 128), win_rm.reshape(nb, 1, 128), cmp_pool, sel_cache,
            win_cache, qb.reshape(nb, 1, 256), sb_cache, lam_init, diff_lam[l], ng, qd.reshape(nb, 1, 256),
            diff_rm.reshape(nb, 1, 512), diff_cache)
        oa = oa[:, :N_HEADS, HEAD_DIM:].reshape(nb, GROUP_WIDTH).astype(BF16)
        ob, oc = ob.reshape(nb, 256), oc.reshape(nb, 256)
        xs = _tail(xs, oa, ob, oc, od, *tail_w, alpha)
        new_row = lambda t, shape: _feature_major(t, shape)[:, None]
        outs[1].append(new_row(cmp_t, (2, 1, HEAD_DIM)))
        outs[3].append(new_row(sel_t, (2, 1, HEAD_DIM)))
        outs[5].append(jnp.concatenate([cache_nsa_win[l][:, T:], new_row(win_t, (2, 1, HEAD_DIM))], axis=1))
        outs[7].append(new_row(sb_t, (2, N_HEADS, HEAD_DIM)))
        outs[9].append(new_row(diff_t, (2, N_HEADS, HEAD_DIM)))
        outs[10].append(vn.reshape(nb, 1, GROUP_WIDTH))

    stacked = [jnp.stack(o) for o in outs]
    return (xp, xs.reshape(nb, 1, D_MODEL), *stacked)
```

```python
import functools
import math

import jax
import jax.numpy as jnp
from jax import lax
from jax.experimental import pallas as pl
from jax.experimental.pallas import tpu as pltpu

D_MODEL = 1024
HEAD_DIM = 64
N_HEADS = 4
GROUP_WIDTH = N_HEADS * HEAD_DIM
CMP_BLOCK = 32
SEL_BLOCK = 64
SEL_TOPK = 16
WINDOW = 512
FORCE_SCORE = 1.0e4
DIFF_QK = HEAD_DIM // 2
CHUNK = 128
ROPE_THETA = 500000.0
ROT_HALF = HEAD_DIM // 8
DIFF_ROT_HALF = DIFF_QK // 8
LN_EPS = 1e-5
NEG_INF = -1e30
TINY = 1e-30
NEG_BIG = -3.0e38
SB_UNDERFLOW = -110.0
EXP_HEADROOM = 50.0

LANES = 128
VMEM_LIMIT_BYTES = 56 * 1024 * 1024

TOKEN_TILE = 256
Q_TILE = 256
BIG_Q_TILE = 512
KV_TILE = 256

F32 = jnp.float32
BF16 = jnp.bfloat16

RM = dict(qa=0, qb=512, qd=768, u=1024, v=1280, gate=1536, cmp=1664, sel=1792, win=1920, diff=2048)
TR = dict(cmp=0, sel=128, win=256, sb=384, diff=896)


def _dot(a, b):
    return jnp.dot(a, b, preferred_element_type=F32)


def _dot_nt(a, b):
    return lax.dot_general(a, b, (((1,), (1,)), ((), ())), preferred_element_type=F32)


def _iota(shape, dim):
    return lax.broadcasted_iota(jnp.int32, shape, dim)


def _gelu(x):
    return 0.5 * x * (1.0 + jnp.tanh(0.7978845608028654 * (x + 0.044715 * (x * x * x))))


def _sigmoid(x):
    return 1.0 / (1.0 + jnp.exp(-x))


def _layer_norm(x, g, b):
    mu = jnp.mean(x, axis=-1, keepdims=True)
    xc = x - mu
    var = jnp.mean(xc * xc, axis=-1, keepdims=True)
    return xc * lax.rsqrt(var + LN_EPS) * g + b


def _softplus(z):
    return jnp.maximum(z, 0.0) + jnp.log(1.0 + jnp.exp(-jnp.abs(z)))


def _split_bf16(x):
    hi = x.astype(BF16)
    lo = (x - hi.astype(F32)).astype(BF16)
    return hi, lo


def _suffix_matrix(n):
    return (_iota((n, n), 0) >= _iota((n, n), 1)).astype(BF16)


def _rope(h, cos, sin, half, axis):
    n = h.shape[axis]
    above = pltpu.roll(h, n - half, axis)
    below = pltpu.roll(h, half, axis)
    first = (_iota(h.shape, axis) & (2 * half - 1)) < half
    return h * cos + jnp.where(first, -above, below) * sin


def _params(sem):
    return pltpu.CompilerParams(dimension_semantics=sem, vmem_limit_bytes=VMEM_LIMIT_BYTES)


def _const_spec(shape):
    zeros = (0,) * len(shape)
    return pl.BlockSpec(shape, lambda *_: zeros, pipeline_mode=pl.Buffered(1))


def _proj_common(xb, wrm_ref, cq_ref, sq_ref, cd_ref, sd_ref, gb_ref, lng_ref, lnb_ref,
                 qaf_ref, qb_ref, qd_ref, g_ref):
    def rm(name, k=0, width=128):
        a = RM[name] + k * 128
        return _dot(xb, wrm_ref[:, a:a + width])

    cq, sq = cq_ref[...], sq_ref[...]
    for h in range(N_HEADS):
        qaf_ref[:, h * 128:(h + 1) * 128] = _rope(rm("qa", h), cq, sq, ROT_HALF, 1).astype(BF16)
    qb_ref[...] = rm("qb", 0, 256).astype(BF16)
    cd, sd = cd_ref[...], sd_ref[...]
    for k in range(2):
        qd_ref[:, k * 128:(k + 1) * 128] = _rope(rm("qd", k), cd, sd, DIFF_ROT_HALF, 1).astype(BF16)
    g_ref[...] = _sigmoid(rm("gate") + gb_ref[...])
    u = _gelu(rm("u", 0, 256))
    vn = _layer_norm(_gelu(rm("v", 0, 256)), lng_ref[...], lnb_ref[...])
    return rm, u, vn


def _proj_feature_major(xb, wt_ref, cta_ref, sta_ref, ctd_ref, std_ref):
    def tr(name, k=0, height=128):
        a = TR[name] + k * 128
        return _dot_nt(wt_ref[a:a + height, :], xb)

    cta, sta = cta_ref[...], sta_ref[...]
    ctd, std = ctd_ref[...], std_ref[...]
    cmp_t = tr("cmp")
    sel_t = _rope(tr("sel"), cta, sta, ROT_HALF, 0)
    win_t = _rope(tr("win"), cta, sta, ROT_HALF, 0)
    sb_t = tr("sb", 0, 512)
    kd_t = [_rope(tr("diff", k), ctd, std, DIFF_ROT_HALF, 0) for k in range(2)]
    vd_t = tr("diff", 2, 256)
    return cmp_t, sel_t, win_t, sb_t, kd_t, vd_t


def _proj_prompt_kernel(x_ref, wrm_ref, wt_ref, cq_ref, sq_ref, cd_ref, sd_ref,
                        cta_ref, sta_ref, ctd_ref, std_ref, gb_ref, lng_ref, lnb_ref, ws_ref, bsm_ref,
                        qaf_ref, qb_ref, qd_ref, g_ref, od_ref, cmprm_ref,
                        cmpt_ref, selt_ref, wint_ref, sbt_ref, difft_ref,
                        selbf_ref, winbf_ref, sbbf_ref, diffbf_ref, *, tm, tk):
    xb = x_ref[...].astype(BF16)
    rm, u, vn = _proj_common(xb, wrm_ref, cq_ref, sq_ref, cd_ref, sd_ref, gb_ref, lng_ref, lnb_ref,
                             qaf_ref, qb_ref, qd_ref, g_ref)
    cmprm_ref[...] = rm("cmp")

    tril = _iota((CHUNK, CHUNK), 1) <= _iota((CHUNK, CHUNK), 0)
    lane_group = _iota((CHUNK, GROUP_WIDTH), 1) >> 6
    wsg = [jnp.where(tril, ws_ref[g], 0.0).astype(BF16) for g in range(N_HEADS)]
    bsm = bsm_ref[...]
    for c in range(tm // CHUNK):
        rows = slice(c * CHUNK, (c + 1) * CHUNK)
        vnc = vn[rows, :]
        mixed = bsm
        for g in range(N_HEADS):
            mixed = mixed + _dot(wsg[g], jnp.where(lane_group == g, vnc, 0.0).astype(BF16))
        od_ref[rows, :] = (u[rows, :] * mixed).astype(BF16)

    cmp_t, sel_t, win_t, sb_t, kd_t, vd_t = _proj_feature_major(
        xb, wt_ref, cta_ref, sta_ref, ctd_ref, std_ref)
    cmpt_ref[...] = cmp_t
    selt_ref[...] = sel_t
    wint_ref[...] = win_t
    sbt_ref[...] = sb_t
    difft_ref[0:128, :] = kd_t[0]
    difft_ref[128:256, :] = kd_t[1]
    difft_ref[256:512, :] = vd_t
    for t in range(tm // tk):
        cols = slice(t * tk, (t + 1) * tk)
        selbf_ref[t] = sel_t[:, cols].astype(BF16)
        winbf_ref[t] = win_t[:, cols].astype(BF16)
        sbbf_ref[t] = sb_t[:, cols].astype(BF16)
        diffbf_ref[t, 0:128, :] = kd_t[0][:, cols].astype(BF16)
        diffbf_ref[t, 128:256, :] = kd_t[1][:, cols].astype(BF16)
        diffbf_ref[t, 256:512, :] = vd_t[:, cols].astype(BF16)


def _proj_sample_kernel(x_ref, wrm_ref, wt_ref, cq_ref, sq_ref, cd_ref, sd_ref,
                        cta_ref, sta_ref, ctd_ref, std_ref, gb_ref, lng_ref, lnb_ref, w00_ref, b0_ref,
                        qaf_ref, qb_ref, qd_ref, g_ref, od_ref, vn_ref,
                        selrm_ref, winrm_ref, diffrm_ref,
                        cmpt_ref, selt_ref, wint_ref, sbt_ref, difft_ref):
    xb = x_ref[...].astype(BF16)
    rm, u, vn = _proj_common(xb, wrm_ref, cq_ref, sq_ref, cd_ref, sd_ref, gb_ref, lng_ref, lnb_ref,
                             qaf_ref, qb_ref, qd_ref, g_ref)
    vn_ref[...] = vn
    od_ref[...] = (u * (vn * w00_ref[...] + b0_ref[...])).astype(BF16)

    cq, sq = cq_ref[...], sq_ref[...]
    cd, sd = cd_ref[...], sd_ref[...]
    selrm_ref[...] = _rope(rm("sel"), cq, sq, ROT_HALF, 1)
    winrm_ref[...] = _rope(rm("win"), cq, sq, ROT_HALF, 1)
    for k in range(2):
        diffrm_ref[:, k * 128:(k + 1) * 128] = _rope(rm("diff", k), cd, sd, DIFF_ROT_HALF, 1)
    diffrm_ref[:, 256:512] = rm("diff", 2, 256)

    cmp_t, sel_t, win_t, sb_t, kd_t, vd_t = _proj_feature_major(
        xb, wt_ref, cta_ref, sta_ref, ctd_ref, std_ref)
    cmpt_ref[...] = cmp_t
    selt_ref[...] = sel_t
    wint_ref[...] = win_t
    sbt_ref[...] = sb_t
    difft_ref[0:128, :] = kd_t[0]
    difft_ref[128:256, :] = kd_t[1]
    difft_ref[256:512, :] = vd_t


def _proj_prompt(x, wrm, wt, tabs, gb, lng, lnb, ws, bsm):
    B, S, _ = x.shape
    tm, tk = TOKEN_TILE, KV_TILE
    nk = S // tk
    row = lambda n: pl.BlockSpec((None, tm, n), lambda b, i: (b, i, 0))
    feat = lambda n: pl.BlockSpec((None, n, tm), lambda b, i: (b, 0, i))
    tile = lambda n: pl.BlockSpec((None, tm // tk, n, tk), lambda b, i: (b, i, 0, 0))
    tab_rm = pl.BlockSpec((tm, 128), lambda b, i: (i, 0))
    tab_t = pl.BlockSpec((128, tm), lambda b, i: (0, i))
    in_specs = [row(D_MODEL), _const_spec(wrm.shape), _const_spec(wt.shape),
                tab_rm, tab_rm, tab_rm, tab_rm, tab_t, tab_t, tab_t, tab_t,
                _const_spec(gb.shape), _const_spec(lng.shape), _const_spec(lnb.shape),
                _const_spec(ws.shape), _const_spec(bsm.shape)]
    sd = jax.ShapeDtypeStruct
    out_shape = [sd((B, S, 512), BF16), sd((B, S, 256), BF16), sd((B, S, 256), BF16),
                 sd((B, S, 128), F32), sd((B, S, 256), BF16), sd((B, S, 128), F32),
                 sd((B, 128, S), F32), sd((B, 128, S), F32), sd((B, 128, S), F32),
                 sd((B, 512, S), F32), sd((B, 512, S), F32),
                 sd((B, nk, 128, tk), BF16), sd((B, nk, 128, tk), BF16),
                 sd((B, nk, 512, tk), BF16), sd((B, nk, 512, tk), BF16)]
    out_specs = [row(512), row(256), row(256), row(128), row(256), row(128),
                 feat(128), feat(128), feat(128), feat(512), feat(512),
                 tile(128), tile(128), tile(512), tile(512)]
    return pl.pallas_call(
        functools.partial(_proj_prompt_kernel, tm=tm, tk=tk),
        grid=(B, S // tm), in_specs=in_specs, out_specs=out_specs, out_shape=out_shape,
        compiler_params=_params(("parallel", "parallel")), name="proj_prompt",
    )(x, wrm, wt, *tabs, gb, lng, lnb, ws, bsm)


def _proj_sample(x, wrm, wt, tabs, gb, lng, lnb, w00, b0):
    nb = x.shape[0]
    full = lambda a: pl.BlockSpec(a.shape, lambda i: (0,) * a.ndim)
    ins = (x, wrm, wt, *tabs, gb, lng, lnb, w00, b0)
    sd = jax.ShapeDtypeStruct
    out_shape = [sd((nb, 512), BF16), sd((nb, 256), BF16), sd((nb, 256), BF16), sd((nb, 128), F32),
                 sd((nb, 256), BF16), sd((nb, 256), F32),
                 sd((nb, 128), F32), sd((nb, 128), F32), sd((nb, 512), F32),
                 sd((128, nb), F32), sd((128, nb), F32), sd((128, nb), F32),
                 sd((512, nb), F32), sd((512, nb), F32)]
    return pl.pallas_call(
        _proj_sample_kernel, grid=(1,), in_specs=[full(a) for a in ins],
        out_specs=[pl.BlockSpec(s.shape, lambda i: (0, 0)) for s in out_shape], out_shape=out_shape,
        compiler_params=_params(("arbitrary",)), name="proj_sample",
    )(*ins)


def _compress_rows(rows_ref, first, stride, n_out, pe_ref, w1_ref, w2_ref):
    h = jnp.zeros((n_out, 128), F32)
    for r in range(CMP_BLOCK):
        x = rows_ref[pl.ds(first + r, n_out, stride=stride), :] + pe_ref[r:r + 1, :]
        h = h + _dot(x.astype(BF16), w1_ref[r])
    return _dot(_gelu(h).astype(BF16), w2_ref[...])


def _compress_prompt_kernel(x_ref, pe_ref, w1_ref, w2_ref, o_ref, *, ns):
    for parity in range(2):
        o_ref[parity * ns:(parity + 1) * ns, :] = _compress_rows(
            x_ref, parity * CMP_BLOCK, SEL_BLOCK, ns, pe_ref, w1_ref, w2_ref).astype(BF16)


def _compress_prompt(cmp_rm, pe, w1, w2):
    B, S, _ = cmp_rm.shape
    ns = S // SEL_BLOCK
    return pl.pallas_call(
        functools.partial(_compress_prompt_kernel, ns=ns), grid=(B,),
        in_specs=[pl.BlockSpec((None, S, 128), lambda b: (b, 0, 0)), _const_spec(pe.shape),
                  _const_spec(w1.shape), _const_spec(w2.shape)],
        out_specs=pl.BlockSpec((None, 2 * ns, 128), lambda b: (b, 0, 0)),
        out_shape=jax.ShapeDtypeStruct((B, 2 * ns, 128), BF16),
        compiler_params=_params(("parallel",)), name="compress_prompt",
    )(cmp_rm, pe, w1, w2)


def _compress_pool_kernel(x_ref, pe_ref, w1_ref, w2_ref, o_ref, rows_ref, *, pages):
    for p in range(pages):
        rows_ref[p * 128:(p + 1) * 128, :] = x_ref[p].T
    n_out = pages * (128 // CMP_BLOCK)
    o_ref[...] = _compress_rows(rows_ref, 0, CMP_BLOCK, n_out, pe_ref, w1_ref, w2_ref)


def _compress_pool(layer, cmp_cache, pe, w1, w2):
    n_pool = cmp_cache.shape[1]
    pages = math.gcd(n_pool, 64)
    n_out = pages * (128 // CMP_BLOCK)
    return pl.pallas_call(
        functools.partial(_compress_pool_kernel, pages=pages), grid=(n_pool // pages,),
        in_specs=[pl.BlockSpec((None, pages, 128, 128), lambda g: (layer, g, 0, 0)), _const_spec(pe.shape),
                  _const_spec(w1.shape), _const_spec(w2.shape)],
        out_specs=pl.BlockSpec((n_out, 128), lambda g: (g, 0)),
        out_shape=jax.ShapeDtypeStruct((n_pool * (128 // CMP_BLOCK), 128), F32),
        scratch_shapes=[pltpu.VMEM((pages * 128, 128), F32)],
        compiler_params=_params(("parallel",)), name="compress_pool",
    )(cmp_cache, pe, w1, w2)


def _select_topk(score, blk_f, n_blocks, k):
    sel = jnp.zeros(score.shape, F32)
    work = score
    for _ in range(k):
        mx = jnp.max(work, axis=0, keepdims=True)
        idx = jnp.min(jnp.where(work == mx, blk_f, float(n_blocks)), axis=0, keepdims=True)
        pick = blk_f == idx
        sel = jnp.where(pick, 1.0, sel)
        work = jnp.where(pick, NEG_BIG, work)
    return sel


def _select_topk_row(score, k):
    n = score.shape[1]
    mine = jnp.broadcast_to(score, (n, n))
    other = mine.T
    beats = (other > mine) | ((other == mine) & (_iota((n, n), 0) < _iota((n, n), 1)))
    rank = jnp.sum(jnp.where(beats, 1.0, 0.0), axis=0, keepdims=True)
    return jnp.where(rank < k, 1.0, 0.0)


def _masked_softmax(logits, mask):
    lm = jnp.where(mask, logits, NEG_INF)
    mx = jnp.max(lm, axis=1, keepdims=True)
    e = jnp.where(mask, jnp.exp(lm - mx), 0.0)
    return e / jnp.maximum(jnp.sum(e, axis=1, keepdims=True), TINY)


def _cmp_block_end(n_half):
    r = _iota((1, 2 * n_half), 1)
    return jnp.where(r < n_half, 64 * r + 31, 64 * (r - n_half) + 63)


def _nsa_prompt_kernel(qf_ref, g_ref, kvc_ref, sel_ref, win_ref, o_ref, qs_ref, acc_ref, m_ref, l_ref, ex_ref,
                       bias_ref, *, tq, sub, ns, topk):
    tk = tq
    i = pl.program_id(1)
    tpos = i * tq + _iota((tq, 1), 0)
    qh = [qf_ref[:, h * 128:(h + 1) * 128] for h in range(N_HEADS)]
    kvc = kvc_ref[...]

    for h in range(N_HEADS):
        qs_ref[h * tq:(h + 1) * tq, :] = qh[h]

    maskc = jnp.broadcast_to((_cmp_block_end(ns) <= tpos)[None], (N_HEADS, tq, 2 * ns)).reshape(N_HEADS * tq, 2 * ns)
    pc = _masked_softmax(_dot_nt(qs_ref[...], kvc), maskc)
    oc_all = _dot(pc.astype(BF16), kvc)
    oc = [oc_all[h * tq:(h + 1) * tq, :] for h in range(N_HEADS)]
    impc = functools.reduce(jnp.add, [pc[h * tq:(h + 1) * tq, :] for h in range(N_HEADS)])
    imp = impc[:, :ns] + impc[:, ns:]

    blk = _iota((ns, tq), 0)
    tpos_row = i * tq + _iota((1, tq), 1)
    cur = tpos_row >> 6
    forced = (blk == 0) | (blk == cur) | (blk == cur - 1)
    valid = (blk << 6) <= tpos_row
    score = jnp.where(forced, FORCE_SCORE, jnp.where(valid, imp.T, -1.0))
    sel_bf = _select_topk(score, blk.astype(F32), ns, topk).T.astype(BF16)

    row, col = _iota((tq, tk), 0), _iota((tq, tk), 1)

    def chosen_in_tile(j):
        expand = (_iota((ns, tk), 0) == j * (tk // SEL_BLOCK) + (_iota((ns, tk), 1) >> 6))
        return _dot(sel_bf, expand.astype(BF16)) > 0.5

    def sel_bias_step(j, carry):
        bias_ref[j] = jnp.where(chosen_in_tile(j), 0.0, NEG_INF).astype(BF16)
        return carry

    lax.fori_loop(0, i, sel_bias_step, 0)
    bias_ref[i] = jnp.where(chosen_in_tile(i) & (col <= row), 0.0, NEG_INF).astype(BF16)

    def attend(kv_ref, j_lo, bias_of):
        def scores(j):
            kt = _key_rows(kv_ref, j, sub, 0, 128)
            s = _dot(qs_ref[...], kt).reshape(N_HEADS, tq, tk) + bias_of(j)[None]
            return s.reshape(N_HEADS * tq, tk), kt

        def max_step(j, carry):
            s, _ = scores(j)
            m_ref[...] = jnp.maximum(m_ref[...], _chunk_reduce(s, jnp.maximum))
            return carry

        def sum_step(j, carry, track=False):
            s, kt = scores(j)
            m = m_ref[...]
            d = [c - m for c in _lane_chunks(s)]
            if track:
                ex_ref[...] = jnp.maximum(ex_ref[...], functools.reduce(jnp.maximum, d))
            p = [jnp.exp(c) for c in d]
            l_ref[...] += functools.reduce(jnp.add, p)
            acc_ref[...] += _dot_nt(jnp.concatenate(p, axis=1).astype(BF16), kt)
            return carry

        def reset_sums():
            acc_ref[...] = jnp.zeros(acc_ref.shape, F32)
            l_ref[...] = jnp.zeros(l_ref.shape, F32)

        def row_max_to_lanes():
            m_ref[...] = jnp.broadcast_to(jnp.max(m_ref[...], axis=1, keepdims=True), m_ref.shape)

        m_ref[...] = jnp.full(m_ref.shape, NEG_INF, F32)
        max_step(i, 0)
        row_max_to_lanes()
        reset_sums()
        ex_ref[...] = jnp.zeros(ex_ref.shape, F32)
        lax.fori_loop(j_lo, i, functools.partial(sum_step, track=True), 0)
        sum_step(i, 0)

        @pl.when(jnp.max(ex_ref[...]) > EXP_HEADROOM)
        def _():
            m_ref[...] = jnp.full(m_ref.shape, NEG_INF, F32)
            lax.fori_loop(j_lo, i + 1, max_step, 0)
            row_max_to_lanes()
            reset_sums()
            lax.fori_loop(j_lo, i + 1, sum_step, 0)

        out = acc_ref[...] / jnp.maximum(jnp.sum(l_ref[...], axis=1, keepdims=True), TINY)
        return [out[h * tq:(h + 1) * tq, :] for h in range(N_HEADS)]

    def win_bias(j):
        d = (i - j) * tk + row - col
        return jnp.where((d >= 0) & (d < WINDOW), 0.0, NEG_INF)

    osel = attend(sel_ref, 0, lambda j: bias_ref[j].astype(F32))
    ow = attend(win_ref, jnp.maximum(i - WINDOW // tk, 0), win_bias)

    g = g_ref[...]
    lane = _iota((tq, 128), 1)
    mixed = [g[:, 3 * h:3 * h + 1] * oc[h] + g[:, 3 * h + 1:3 * h + 2] * osel[h]
             + g[:, 3 * h + 2:3 * h + 3] * ow[h] for h in range(N_HEADS)]
    for k in range(2):
        even = pltpu.roll(mixed[2 * k], 64, 1)
        o_ref[:, k * 128:(k + 1) * 128] = jnp.where(lane < 64, even, mixed[2 * k + 1]).astype(BF16)


def _nsa_prompt(qaf, gates, kvc, selbf, winbf):
    B, S, _ = qaf.shape
    tq = math.gcd(S, BIG_Q_TILE)
    ns = S // SEL_BLOCK
    kern = functools.partial(_nsa_prompt_kernel, tq=tq, sub=tq // KV_TILE, ns=ns, topk=min(SEL_TOPK, ns))
    row = lambda n: pl.BlockSpec((None, tq, n), lambda b, i: (b, i, 0))
    whole = lambda a: pl.BlockSpec((None,) + a.shape[1:], lambda b, i: (b,) + (0,) * (a.ndim - 1))
    return pl.pallas_call(
        kern, grid=(B, S // tq),
        in_specs=[row(512), row(128), whole(kvc), whole(selbf), whole(winbf)],
        out_specs=row(256), out_shape=jax.ShapeDtypeStruct((B, S, 256), BF16),
        scratch_shapes=[pltpu.VMEM((N_HEADS * tq, 128), BF16), pltpu.VMEM((N_HEADS * tq, 128), F32),
                        pltpu.VMEM((N_HEADS * tq, 128), F32), pltpu.VMEM((N_HEADS * tq, 128), F32),
                        pltpu.VMEM((N_HEADS * tq, 128), F32), pltpu.VMEM((S // tq, tq, tq), BF16)],
        compiler_params=_params(("parallel", "arbitrary")), name="nsa_prompt",
    )(qaf, gates, kvc, selbf, winbf)


def _sb_prompt_kernel(q_ref, kv_ref, o_ref, qs_ref, acc_ref, car_ref, *, tq, tk):
    assert tq == tk
    i = pl.program_id(1)
    rows = N_HEADS * tq
    q = q_ref[...].astype(F32)
    lane_head = _iota((1, GROUP_WIDTH), 1) >> 6
    for h in range(N_HEADS):
        qs_ref[h * tq:(h + 1) * tq, :] = jnp.where(lane_head == h, q, 0.0).astype(BF16)
    suffix = _suffix_matrix(tk)
    acc_ref[...] = jnp.zeros(acc_ref.shape, F32)
    car_ref[...] = jnp.zeros(car_ref.shape, F32)

    def step(j, strict):
        z = _dot(qs_ref[...], kv_ref[j, 0:256, :])
        log_keep = -_softplus(z)
        if strict is not None:
            keep = jnp.broadcast_to(strict[None], (N_HEADS, tq, tk)).reshape(rows, tk)
            log_keep = jnp.where(keep, log_keep, 0.0)
            z = jnp.where(keep, z, NEG_INF)
        hi, lo = _split_bf16(log_keep)
        incl = _dot(hi, suffix) + _dot(lo, suffix)
        car = car_ref[...]
        ab = jnp.exp(z + incl + jnp.concatenate([car, car], axis=1)).astype(BF16)
        for k in range(2):
            pair = slice(2 * k * tq, (2 * k + 2) * tq)
            acc_ref[pair, :] += _dot_nt(ab[pair, :], kv_ref[j, 256 + k * 128:384 + k * 128, :])
        car_ref[...] = car + jnp.broadcast_to(incl[:, 0:1], (rows, 128))

    def any_weight_left():
        return jnp.max(car_ref[...]) > SB_UNDERFLOW

    step(i, _iota((tq, tk), 1) < _iota((tq, tk), 0))

    def left_tile(state):
        j, _ = state
        step(j, None)
        return j - 1, any_weight_left()

    lax.while_loop(lambda state: (state[0] >= 0) & state[1], left_tile, (i - 1, any_weight_left()))
    lane = _iota((tq, 128), 1)
    for k in range(2):
        even, odd = acc_ref[2 * k * tq:(2 * k + 1) * tq, :], acc_ref[(2 * k + 1) * tq:(2 * k + 2) * tq, :]
        o_ref[:, k * 128:(k + 1) * 128] = jnp.where(lane < 64, even, odd).astype(BF16)


def _sb_prompt(qb, sbbf):
    B, S, _ = qb.shape
    tq, tk = Q_TILE, KV_TILE
    row = lambda n: pl.BlockSpec((None, tq, n), lambda b, i: (b, i, 0))
    return pl.pallas_call(
        functools.partial(_sb_prompt_kernel, tq=tq, tk=tk), grid=(B, S // tq),
        in_specs=[row(256), pl.BlockSpec((None,) + sbbf.shape[1:], lambda b, i: (b, 0, 0, 0))],
        out_specs=row(256), out_shape=jax.ShapeDtypeStruct((B, S, 256), BF16),
        scratch_shapes=[pltpu.VMEM((N_HEADS * tq, GROUP_WIDTH), BF16), pltpu.VMEM((N_HEADS * tq, 128), F32),
                        pltpu.VMEM((N_HEADS * tq, 128), F32)],
        compiler_params=_params(("parallel", "arbitrary")), name="sb_prompt",
    )(qb, sbbf)


def _diff_lambda(dl_ref, lam_init):
    lq = dl_ref[...]
    s1 = jnp.sum(lq[0:1, :] * lq[1:2, :], axis=1, keepdims=True)
    s2 = jnp.sum(lq[2:3, :] * lq[3:4, :], axis=1, keepdims=True)
    return jnp.exp(s1) - jnp.exp(s2) + lam_init


def _head_rms_norm(out, lane_head, ng, lam_init):
    sq = out * out
    rs = jnp.zeros(out.shape, F32)
    for h in range(N_HEADS):
        ms = jnp.sum(jnp.where(lane_head == h, sq, 0.0), axis=1, keepdims=True) * (1.0 / HEAD_DIM)
        rs = jnp.where(lane_head == h, lax.rsqrt(ms + LN_EPS), rs)
    return out * rs * ng * (1.0 - lam_init)


def _lane_chunks(x):
    return [x[:, c * LANES:(c + 1) * LANES] for c in range(x.shape[1] // LANES)]


def _chunk_reduce(x, op):
    chunks = _lane_chunks(x)
    out = chunks[0]
    for c in chunks[1:]:
        out = op(out, c)
    return out


def _key_rows(kv_ref, j, sub, r0, r1):
    return jnp.concatenate([kv_ref[j * sub + u, r0:r1, :] for u in range(sub)], axis=1)


def _diff_prompt_kernel(li_ref, dl_ref, ng_ref, q_ref, kv_ref, o_ref, qs_ref, acc_ref, m_ref, l_ref, ex_ref,
                        *, tq, sub):
    tk = tq
    i = pl.program_id(1)
    lam_init = li_ref[0]
    lam = _diff_lambda(dl_ref, lam_init)
    q = q_ref[...].astype(F32)
    lane = _iota((1, GROUP_WIDTH), 1)
    lane_head, lane_half = lane >> 6, (lane >> 5) & 1
    n_soft = 2 * N_HEADS
    rows = n_soft * tq
    for a in range(n_soft):
        qa = jnp.where((lane_head == a // 2) & (lane_half == a % 2), q, 0.0)
        qs_ref[a * tq:(a + 1) * tq, :] = qa.astype(BF16)
    causal_bias = jnp.where(_iota((tq, tk), 1) <= _iota((tq, tk), 0), 0.0, NEG_INF)

    def scores(j, bias):
        s = _dot(qs_ref[...], _key_rows(kv_ref, j, sub, 0, 256))
        if bias is not None:
            s = (s.reshape(n_soft, tq, tk) + bias[None]).reshape(rows, tk)
        return s

    def max_step(j, bias):
        m_ref[...] = jnp.maximum(m_ref[...], _chunk_reduce(scores(j, bias), jnp.maximum))

    def sum_step(j, bias, s=None, track=False):
        s = scores(j, bias) if s is None else s
        m = m_ref[...]
        d = [c - m for c in _lane_chunks(s)]
        if track:
            ex_ref[...] = jnp.maximum(ex_ref[...], functools.reduce(jnp.maximum, d))
        p = [jnp.exp(c) for c in d]
        l_ref[...] += functools.reduce(jnp.add, p)
        pb = jnp.concatenate(p, axis=1).astype(BF16)
        for k in range(2):
            part = slice(k * (rows // 2), (k + 1) * (rows // 2))
            acc_ref[part, :] += _dot_nt(pb[part, :], _key_rows(kv_ref, j, sub, 256 + k * 128, 384 + k * 128))

    def below(step):
        def body(j, carry):
            step(j, None)
            return carry
        lax.fori_loop(0, i, body, 0)

    def reset_sums():
        acc_ref[...] = jnp.zeros(acc_ref.shape, F32)
        l_ref[...] = jnp.zeros(l_ref.shape, F32)

    def row_max_to_lanes():
        m_ref[...] = jnp.broadcast_to(jnp.max(m_ref[...], axis=1, keepdims=True), m_ref.shape)

    s_diag = scores(i, causal_bias)
    m_ref[...] = _chunk_reduce(s_diag, jnp.maximum)
    row_max_to_lanes()
    reset_sums()
    ex_ref[...] = jnp.zeros(ex_ref.shape, F32)
    sum_step(i, None, s=s_diag)
    below(functools.partial(sum_step, track=True))

    @pl.when(jnp.max(ex_ref[...]) > EXP_HEADROOM)
    def _():
        m_ref[...] = jnp.full(m_ref.shape, NEG_INF, F32)
        below(max_step)
        max_step(i, causal_bias)
        row_max_to_lanes()
        reset_sums()
        below(sum_step)
        sum_step(i, causal_bias)

    lane128 = _iota((tq, 128), 1)
    ng = ng_ref[...]
    blk = lambda ref, a: ref[a * tq:(a + 1) * tq, :]
    for k in range(2):
        o = []
        for h in (2 * k, 2 * k + 1):
            l1 = jnp.maximum(jnp.sum(blk(l_ref, 2 * h), axis=1, keepdims=True), TINY)
            l2 = jnp.maximum(jnp.sum(blk(l_ref, 2 * h + 1), axis=1, keepdims=True), TINY)
            o.append(blk(acc_ref, 2 * h) / l1 - lam * (blk(acc_ref, 2 * h + 1) / l2))
        low = lane128 < HEAD_DIM
        out = jnp.where(low, o[0], o[1])
        sq = out * out
        ms_lo = jnp.sum(jnp.where(low, sq, 0.0), axis=1, keepdims=True)
        ms_hi = jnp.sum(jnp.where(low, 0.0, sq), axis=1, keepdims=True)
        rs = lax.rsqrt(jnp.where(low, ms_lo, ms_hi) * (1.0 / HEAD_DIM) + LN_EPS)
        o_ref[:, k * 128:(k + 1) * 128] = (out * rs * ng[:, k * 128:(k + 1) * 128] * (1.0 - lam_init)).astype(BF16)


def _diff_prompt(lam_init, dl, ng, qd, diffbf):
    B, S, _ = qd.shape
    tq = math.gcd(S, BIG_Q_TILE)
    row = lambda n: pl.BlockSpec((None, tq, n), lambda b, i: (b, i, 0))
    return pl.pallas_call(
        functools.partial(_diff_prompt_kernel, tq=tq, sub=tq // KV_TILE), grid=(B, S // tq),
        in_specs=[pl.BlockSpec(memory_space=pltpu.SMEM), _const_spec(dl.shape), _const_spec(ng.shape),
                  row(256), pl.BlockSpec((None,) + diffbf.shape[1:], lambda b, i: (b, 0, 0, 0))],
        out_specs=row(256), out_shape=jax.ShapeDtypeStruct((B, S, 256), BF16),
        scratch_shapes=[pltpu.VMEM((2 * N_HEADS * tq, GROUP_WIDTH), BF16), pltpu.VMEM((2 * N_HEADS * tq, 128), F32),
                        pltpu.VMEM((2 * N_HEADS * tq, 128), F32), pltpu.VMEM((2 * N_HEADS * tq, 128), F32),
                        pltpu.VMEM((2 * N_HEADS * tq, 128), F32)],
        compiler_params=_params(("parallel", "arbitrary")), name="diff_prompt",
    )(lam_init, dl, ng, qd, diffbf)


def _round_bf16(x):
    return x.astype(BF16).astype(F32)


def _softmax_with_self(s, ok, s_self):
    sm = jnp.where(ok, s, NEG_INF)
    mx = jnp.maximum(jnp.max(sm, axis=1, keepdims=True), s_self)
    e = jnp.where(ok, jnp.exp(sm - mx), 0.0)
    e_self = jnp.exp(s_self - mx)
    inv = 1.0 / (jnp.sum(e, axis=1, keepdims=True) + e_self)
    return e * inv, e_self * inv


def _nsa_sample_kernel(pt_ref, qh_ref, gs_ref, snew_ref, wnew_ref, win_ref, *rest,
                       n_pages, past, half, topk):
    del pt_ref
    cmp_refs, sel_refs = rest[:n_pages], rest[n_pages:2 * n_pages]
    o_ref, kvc_ref = rest[2 * n_pages], rest[2 * n_pages + 1]
    qh = qh_ref[...]
    qf = qh.astype(F32)

    kvc_ref[...] = jnp.zeros(kvc_ref.shape, F32)
    for p in range(n_pages):
        piece = cmp_refs[p][...]
        kvc_ref[2 * p:2 * p + 2, :] = piece[0:2, :]
        kvc_ref[half + 2 * p:half + 2 * p + 2, :] = piece[2:4, :]
    kvc = kvc_ref[...].astype(BF16)
    maskc = _cmp_block_end(half) <= past
    pc = _masked_softmax(_dot_nt(qh, kvc), maskc)
    oc = _dot(pc.astype(BF16), kvc)
    head_row = _iota((8, 1), 0) < N_HEADS
    impc = jnp.sum(jnp.where(head_row, pc, 0.0), axis=0, keepdims=True)
    imp = impc[:, :half] + impc[:, half:]

    n_sel = past // SEL_BLOCK + 1
    blk = _iota((1, half), 1)
    cur = past // SEL_BLOCK
    forced = (blk == 0) | (blk == cur) | (blk == cur - 1)
    valid = blk * SEL_BLOCK <= past
    score = jnp.where(blk < n_sel, jnp.where(forced, FORCE_SCORE, jnp.where(valid, imp, -1.0)), NEG_BIG)
    sel8 = jnp.broadcast_to(_select_topk_row(score, topk), (8, half)).astype(BF16)
    expand = _iota((half, past), 0) == (_iota((half, past), 1) >> 6)
    chosen = _dot(sel8, expand.astype(BF16)) > 0.5

    def branch(kv, new_row, ok):
        new_b = _round_bf16(new_row)
        s_self = jnp.sum(qf * new_b, axis=1, keepdims=True)
        p, p_self = _softmax_with_self(_dot(qh, kv), ok, s_self)
        return p_self * new_b + _dot_nt(p.astype(BF16), kv)

    osel = branch(jnp.concatenate([r[...].astype(BF16) for r in sel_refs], axis=1), snew_ref[...], chosen)
    w_past = win_ref.shape[1]
    in_window = _iota((1, w_past), 1) >= (w_past - WINDOW + 1)
    ow = branch(win_ref[...].astype(BF16), wnew_ref[...], in_window)
    g = gs_ref[...]
    o_ref[...] = g[:, 0:1] * oc + g[:, 1:2] * osel + g[:, 2:3] * ow


def _sb_sample_kernel(pt_ref, q_ref, *rest, n_pages, past, cb):
    del pt_ref
    pages, o_ref = rest[:n_pages], rest[n_pages]
    q8 = jnp.broadcast_to(q_ref[...].astype(F32), (8, GROUP_WIDTH))
    row_head = _iota((8, GROUP_WIDTH), 0) == (_iota((8, GROUP_WIDTH), 1) >> 6)
    qm = jnp.where(row_head, q8, 0.0).astype(BF16)
    kt = jnp.concatenate([pages[p][0].astype(BF16) for p in range(n_pages)], axis=1)
    vt = jnp.concatenate([pages[p][1].astype(BF16) for p in range(n_pages)], axis=1)
    z = _dot(qm, kt)
    log_keep = -_softplus(z)
    n_blk = past // cb
    stack = lambda x: jnp.concatenate([x[:, b * cb:(b + 1) * cb] for b in range(n_blk)], axis=0)
    hi, lo = _split_bf16(stack(log_keep))
    suffix = _suffix_matrix(cb)
    incl = _dot(hi, suffix) + _dot(lo, suffix)
    carry = jnp.zeros((8, 1), F32)
    carries = [None] * n_blk
    for b in reversed(range(n_blk)):
        carries[b] = carry
        carry = carry + incl[b * 8:(b + 1) * 8, 0:1]
    a = jnp.exp(stack(z) + incl + jnp.concatenate(carries, axis=0))
    a = jnp.concatenate([a[b * 8:(b + 1) * 8, :] for b in range(n_blk)], axis=1)
    out = _dot_nt(a.astype(BF16), vt)
    o_ref[...] = jnp.sum(jnp.where(row_head, out, 0.0), axis=0, keepdims=True).astype(BF16)


def _diff_sample_kernel(pt_ref, li_ref, dl_ref, ng_ref, q_ref, dnew_ref, *rest, n_pages):
    del pt_ref
    pages, o_ref = rest[:n_pages], rest[n_pages]
    lam_init = li_ref[0]
    lam = _diff_lambda(dl_ref, lam_init)
    q8 = jnp.broadcast_to(q_ref[...].astype(F32), (8, GROUP_WIDTH))
    lane = _iota((8, GROUP_WIDTH), 1)
    row_head = _iota((8, GROUP_WIDTH), 0) == (lane >> 6)
    lane_half = (lane >> 5) & 1
    new_k = _round_bf16(dnew_ref[:, 0:256])
    new_v = _round_bf16(dnew_ref[:, 256:512])
    kt = jnp.concatenate([pages[p][0].astype(BF16) for p in range(n_pages)], axis=1)
    vt = jnp.concatenate([pages[p][1].astype(BF16) for p in range(n_pages)], axis=1)
    ps, ps_self = [], []
    for m in range(2):
        qm = jnp.where(row_head & (lane_half == m), q8, 0.0)
        s = _dot(qm.astype(BF16), kt)
        s_self = jnp.sum(qm * new_k, axis=1, keepdims=True)
        p, p_self = _softmax_with_self(s, jnp.full(s.shape, True), s_self)
        ps.append(p)
        ps_self.append(p_self)
    out = (ps_self[0] - lam * ps_self[1]) * new_v + _dot_nt((ps[0] - lam * ps[1]).astype(BF16), vt)
    o4 = jnp.sum(jnp.where(row_head, out, 0.0), axis=0, keepdims=True)
    o_ref[...] = _head_rms_norm(o4, _iota((1, GROUP_WIDTH), 1) >> 6, ng_ref[...], lam_init).astype(BF16)


def _page_spec(block, layer, p):
    nz = (0,) * (len(block) - 2)
    return pl.BlockSpec(block, lambda b, pt: (layer, pt[b, p]) + nz)


def _sample_mix_kernel(pt_ref, qh_ref, gs_ref, snew_ref, wnew_ref, win_ref, qb_ref, li_ref, dl_ref, ng_ref,
                       qd_ref, dnew_ref, *rest, n_pages, past, half, topk, cb):
    n = n_pages
    cmp_refs, sel_refs, sb_refs, diff_refs = rest[:n], rest[n:2 * n], rest[2 * n:3 * n], rest[3 * n:4 * n]
    oa_ref, ob_ref, oc_ref, kvc_ref = rest[4 * n:4 * n + 4]
    _nsa_sample_kernel(pt_ref, qh_ref, gs_ref, snew_ref, wnew_ref, win_ref, *cmp_refs, *sel_refs, oa_ref, kvc_ref,
                       n_pages=n, past=past, half=half, topk=topk)
    _sb_sample_kernel(pt_ref, qb_ref, *sb_refs, ob_ref, n_pages=n, past=past, cb=cb)
    _diff_sample_kernel(pt_ref, li_ref, dl_ref, ng_ref, qd_ref, dnew_ref, *diff_refs, oc_ref, n_pages=n)


def _sample_mix(layer, page_table, qh, gs, snew, wnew, cmp_pool, sel_cache, win_cache,
                qb, sb_cache, lam_init, dl, ng, qd, dnew, diff_cache):
    nb, n_pages = page_table.shape
    past = n_pages * 128
    half = max(128, -(-(past // SEL_BLOCK + 1) // 128) * 128)
    n_sel = past // SEL_BLOCK + 1
    kern = functools.partial(_sample_mix_kernel, n_pages=n_pages, past=past, half=half,
                             topk=min(SEL_TOPK, n_sel), cb=math.gcd(past, 256))
    per_seq = lambda a: pl.BlockSpec((None,) + a.shape[1:], lambda b, pt: (b,) + (0,) * (a.ndim - 1))
    const = lambda a: pl.BlockSpec(a.shape, lambda b, pt: (0,) * a.ndim)
    in_specs = [per_seq(qh), per_seq(gs), per_seq(snew), per_seq(wnew),
                pl.BlockSpec((None, None) + win_cache.shape[2:], lambda b, pt: (layer, b, 0, 0)),
                per_seq(qb), pl.BlockSpec(memory_space=pltpu.SMEM), const(dl), const(ng),
                per_seq(qd), per_seq(dnew)]
    in_specs += [pl.BlockSpec((None, 4, 128), functools.partial(lambda b, pt, p: (pt[b, p], 0, 0), p=p))
                 for p in range(n_pages)]
    in_specs += [_page_spec((None, None, 128, 128), layer, p) for p in range(n_pages)]
    in_specs += [_page_spec((None, None, 2, 256, 128), layer, p) for p in range(n_pages)]
    in_specs += [_page_spec((None, None, 2, 256, 128), layer, p) for p in range(n_pages)]
    row_out = pl.BlockSpec((None, 1, 256), lambda b, pt: (b, 0, 0))
    grid_spec = pltpu.PrefetchScalarGridSpec(
        num_scalar_prefetch=1, grid=(nb,), in_specs=in_specs,
        out_specs=[pl.BlockSpec((None, 8, 128), lambda b, pt: (b, 0, 0)), row_out, row_out],
        scratch_shapes=[pltpu.VMEM((2 * half, 128), F32)])
    return pl.pallas_call(
        kern, grid_spec=grid_spec,
        out_shape=[jax.ShapeDtypeStruct((nb, 8, 128), F32), jax.ShapeDtypeStruct((nb, 1, 256), BF16),
                   jax.ShapeDtypeStruct((nb, 1, 256), BF16)],
        compiler_params=_params(("arbitrary",)), name="sample_mix",
    )(page_table, qh, gs, snew, wnew, win_cache, qb, lam_init, dl, ng, qd, dnew,
      *([cmp_pool] * n_pages), *([sel_cache] * n_pages), *([sb_cache] * n_pages), *([diff_cache] * n_pages))


def _tail_kernel(x_ref, oa_ref, ob_ref, oc_ref, od_ref, wo_ref, g1_ref, b1_ref,
                 wg_ref, wu_ref, wd_ref, g2_ref, b2_ref, y_ref, *, alpha):
    mix = (_dot(oa_ref[...], wo_ref[0:256, :]) + _dot(ob_ref[...], wo_ref[256:512, :])
           + _dot(oc_ref[...], wo_ref[512:768, :]) + _dot(od_ref[...], wo_ref[768:1024, :]))
    x1 = _layer_norm(alpha * x_ref[...] + mix, g1_ref[...], b1_ref[...])
    x1b = x1.astype(BF16)
    hg = _dot(x1b, wg_ref[...])
    hu = _dot(x1b, wu_ref[...])
    act = (hg * _sigmoid(hg) * hu).astype(BF16)
    y_ref[...] = _layer_norm(alpha * x1 + _dot(act, wd_ref[...]), g2_ref[...], b2_ref[...])


def _tail(x, oa, ob, oc, od, wo, g1, b1, wg, wu, wd, g2, b2, alpha):
    m = x.shape[0]
    tm = math.gcd(m, TOKEN_TILE)
    row = lambda n: pl.BlockSpec((tm, n), lambda i: (i, 0))
    consts = (wo, g1, b1, wg, wu, wd, g2, b2)
    return pl.pallas_call(
        functools.partial(_tail_kernel, alpha=alpha), grid=(m // tm,),
        in_specs=[row(D_MODEL), row(256), row(256), row(256), row(256)] + [_const_spec(c.shape) for c in consts],
        out_specs=row(D_MODEL), out_shape=jax.ShapeDtypeStruct((m, D_MODEL), F32),
        compiler_params=_params(("parallel",)), name="tail",
    )(x, oa, ob, oc, od, *consts)


def _pad_heads(w):
    k = w.shape[0]
    wr = w.reshape(k, -1, HEAD_DIM)
    return jnp.concatenate([wr, jnp.zeros_like(wr)], axis=-1).reshape(k, -1)


def _layer_weights(w, sample):
    off = [0]
    for n in (256, 64, 64, 64, 64, 64, 64, 12, 256, 256, 256, 256, 256, 256, 256, 256):
        off.append(off[-1] + n)
    col = lambda i: w[:, off[i]:off[i + 1]]
    qa, kc, vc, ks, vs, kw, vw, ga, qb, kb, vb, qd, kd, vd, u, v = [col(i) for i in range(16)]
    qa = qa * (HEAD_DIM ** -0.5)
    qb = qb * (HEAD_DIM ** -0.5)
    qd = qd * (DIFF_QK ** -0.5)
    rm = [_pad_heads(qa), qb, qd, u, v,
          jnp.concatenate([ga, jnp.zeros((w.shape[0], 128 - ga.shape[1]), F32)], 1), kc, vc]
    if sample:
        rm += [ks, vs, kw, vw, kd, vd]
    tr = [kc, vc, ks, vs, kw, vw, kb, vb, kd, vd]
    return (jnp.concatenate(rm, 1).astype(BF16), jnp.concatenate(tr, 1).T.astype(BF16))


def _rope_tables(pos):
    pos = pos.astype(F32)[:, None]
    n = pos.shape[0]

    def table(half, plain, reps):
        freq = ROPE_THETA ** (-jnp.arange(half, dtype=F32) / half)
        ang = pos * freq[None, :]
        c = jnp.concatenate([jnp.cos(ang), jnp.cos(ang), jnp.ones((n, plain), F32)], 1)
        s = jnp.concatenate([jnp.sin(ang), jnp.sin(ang), jnp.zeros((n, plain), F32)], 1)
        return jnp.tile(c, (1, reps)), jnp.tile(s, (1, reps))

    ca, sa = table(ROT_HALF, 128 - 2 * ROT_HALF, 1)
    cd, sd = table(DIFF_ROT_HALF, DIFF_QK - 2 * DIFF_ROT_HALF, 4)
    return (ca, sa, cd, sd, ca.T, sa.T, cd.T, sd.T)


def _compress_weights(pe, w1, w2):
    z = jnp.zeros((CMP_BLOCK, HEAD_DIM, HEAD_DIM), F32)
    w1k = w1[0].reshape(CMP_BLOCK, HEAD_DIM, HEAD_DIM)
    w1v = w1[1].reshape(CMP_BLOCK, HEAD_DIM, HEAD_DIM)
    w1c = jnp.concatenate([jnp.concatenate([w1k, z], 2), jnp.concatenate([z, w1v], 2)], 1)
    z2 = jnp.zeros((HEAD_DIM, HEAD_DIM), F32)
    w2c = jnp.concatenate([jnp.concatenate([w2[0], z2], 1), jnp.concatenate([z2, w2[1]], 1)], 0)
    pe_rows = jnp.concatenate([pe[0], pe[1]], 1)
    return pe_rows, w1c.astype(BF16), w2c.astype(BF16)


def _feature_major(x, tail_shape):
    lead = x.shape[:-2]
    n = x.shape[-1]
    y = x.reshape(lead + tail_shape + (n,))
    nd = y.ndim
    return jnp.transpose(y, tuple(range(len(lead))) + (nd - 1,) + tuple(range(len(lead), nd - 1)))


def kernel(x_prompt, x_sample, cache_nsa_cmp, cache_nsa_sel, cache_nsa_win, cache_sb, cache_diff,
           page_table, w_in, gate_b, cmp_pe, cmp_w1, cmp_w2, diff_lam, diff_norm_g, gmlp_ln_g,
           gmlp_ln_b, gmlp_ws, gmlp_bs, w_out, ln1_g, ln1_b, w_gate, w_up, w_down, ln2_g, ln2_b):
    B, S, _ = x_prompt.shape
    nb, T, _ = x_sample.shape
    depth = w_in.shape[0]
    n_pool = cache_sb.shape[1]
    n_pages = page_table.shape[1]
    past = n_pages * cache_sb.shape[2]
    w_past = cache_nsa_win.shape[2]
    assert T == 1 and cache_sb.shape[2] == 128 and S % Q_TILE == 0 and past % SEL_BLOCK == 0
    alpha = (2 * depth) ** 0.25

    to_pages = lambda c: jnp.transpose(c, (0, 1, 3, 4, 5, 2))
    cmp_cache = to_pages(cache_nsa_cmp).reshape(depth, n_pool, 128, 128)
    sel_cache = to_pages(cache_nsa_sel).reshape(depth, n_pool, 128, 128)
    win_cache = to_pages(cache_nsa_win).reshape(depth, nb, 128, w_past)
    sb_cache = to_pages(cache_sb).reshape(depth, n_pool, 2, 256, 128)
    diff_cache = to_pages(cache_diff).reshape(depth, n_pool, 2, 256, 128)

    tabs_p = _rope_tables(jnp.arange(S, dtype=jnp.int32))
    tabs_s = _rope_tables(jnp.full((nb,), past, jnp.int32))
    row = lambda a: a.reshape(1, -1)

    xp, xs = x_prompt, x_sample.reshape(nb, D_MODEL)
    outs = [[] for _ in range(11)]
    for l in range(depth):
        lam_init = jnp.full((1,), 0.8 - 0.6 * math.exp(-0.3 * l), F32)
        wrm_p, wt = _layer_weights(w_in[l], sample=False)
        wrm_s, _ = _layer_weights(w_in[l], sample=True)
        gb = jnp.concatenate([gate_b[l], jnp.zeros((128 - gate_b.shape[1],), F32)]).reshape(1, 128)
        lng, lnb = row(gmlp_ln_g[l]), row(gmlp_ln_b[l])
        bsm = jnp.repeat(gmlp_bs[l].T, HEAD_DIM, axis=1)
        w00 = row(jnp.repeat(gmlp_ws[l][:, 0, 0], HEAD_DIM))
        b0 = row(jnp.repeat(gmlp_bs[l][:, 0], HEAD_DIM))
        pe_rows, w1c, w2c = _compress_weights(cmp_pe[l], cmp_w1[l], cmp_w2[l])
        ng = row(jnp.tile(diff_norm_g[l], N_HEADS))
        tail_w = (w_out[l].astype(BF16), row(ln1_g[l]), row(ln1_b[l]), w_gate[l].astype(BF16),
                  w_up[l].astype(BF16), w_down[l].astype(BF16), row(ln2_g[l]), row(ln2_b[l]))

        (qaf, qb, qd, gates, od, cmp_rm, cmp_t, sel_t, win_t, sb_t, diff_t,
         selbf, winbf, sbbf, diffbf) = _proj_prompt(xp, wrm_p, wt, tabs_p, gb, lng, lnb, gmlp_ws[l], bsm)
        kvc = _compress_prompt(cmp_rm, pe_rows, w1c, w2c)
        oa = _nsa_prompt(qaf, gates, kvc, selbf, winbf)
        ob = _sb_prompt(qb, sbbf)
        oc = _diff_prompt(lam_init, diff_lam[l], ng, qd, diffbf)
        flat = lambda a: a.reshape(B * S, a.shape[-1])
        xp = _tail(flat(xp), flat(oa), flat(ob), flat(oc), flat(od), *tail_w, alpha).reshape(B, S, D_MODEL)
        outs[0].append(_feature_major(cmp_t, (2, 1, HEAD_DIM)))
        outs[2].append(_feature_major(sel_t, (2, 1, HEAD_DIM)))
        outs[4].append(_feature_major(win_t[:, :, S - min(WINDOW, S):], (2, 1, HEAD_DIM)))
        outs[6].append(_feature_major(sb_t, (2, N_HEADS, HEAD_DIM)))
        outs[8].append(_feature_major(diff_t, (2, N_HEADS, HEAD_DIM)))

        (qaf, qb, qd, gates, od, vn, sel_rm, win_rm, diff_rm,
         cmp_t, sel_t, win_t, sb_t, diff_t) = _proj_sample(xs, wrm_s, wt, tabs_s, gb, lng, lnb, w00, b0)
        cmp_pool = _compress_pool(l, cmp_cache, pe_rows, w1c, w2c).reshape(n_pool, 4, 128)[:, jnp.array([0, 2, 1, 3])]
        qh = jnp.pad(qaf.reshape(nb, N_HEADS, 128), ((0, 0), (0, 8 - N_HEADS), (0, 0)))
        gs = jnp.pad(gates[:, :3 * N_HEADS].reshape(nb, N_HEADS, 3), ((0, 0), (0, 8 - N_HEADS), (0, 125)))
        oa, ob, oc = _sample_mix(
            l, page_table, qh, gs, sel_rm.reshape(nb, 1, 128), win_rm.reshape(nb, 1, 128), cmp_pool, sel_cache,
            win_cache, qb.reshape(nb, 1, 256), sb_cache, lam_init, diff_lam[l], ng, qd.reshape(nb, 1, 256),
            diff_rm.reshape(nb, 1, 512), diff_cache)
        oa = oa[:, :N_HEADS, HEAD_DIM:].reshape(nb, GROUP_WIDTH).astype(BF16)
        ob, oc = ob.reshape(nb, 256), oc.reshape(nb, 256)
        xs = _tail(xs, oa, ob, oc, od, *tail_w, alpha)
        new_row = lambda t, shape: _feature_major(t, shape)[:, None]
        outs[1].append(new_row(cmp_t, (2, 1, HEAD_DIM)))
        outs[3].append(new_row(sel_t, (2, 1, HEAD_DIM)))
        outs[5].append(jnp.concatenate([cache_nsa_win[l][:, T:], new_row(win_t, (2, 1, HEAD_DIM))], axis=1))
        outs[7].append(new_row(sb_t, (2, N_HEADS, HEAD_DIM)))
        outs[9].append(new_row(diff_t, (2, N_HEADS, HEAD_DIM)))
        outs[10].append(vn.reshape(nb, 1, GROUP_WIDTH))

    stacked = [jnp.stack(o) for o in outs]
    return (xp, xs.reshape(nb, 1, D_MODEL), *stacked)
```

```python
import functools
import math

import jax
import jax.numpy as jnp
from jax import lax
from jax.experimental import pallas as pl
from jax.experimental.pallas import tpu as pltpu

D_MODEL = 1024
HEAD_DIM = 64
N_HEADS = 4
GROUP_WIDTH = N_HEADS * HEAD_DIM
CMP_BLOCK = 32
SEL_BLOCK = 64
SEL_TOPK = 16
WINDOW = 512
FORCE_SCORE = 1.0e4
DIFF_QK = HEAD_DIM // 2
CHUNK = 128
ROPE_THETA = 500000.0
ROT_HALF = HEAD_DIM // 8
DIFF_ROT_HALF = DIFF_QK // 8
LN_EPS = 1e-5
NEG_INF = -1e30
TINY = 1e-30
NEG_BIG = -3.0e38
SB_UNDERFLOW = -110.0
EXP_HEADROOM = 50.0

LANES = 128
VMEM_LIMIT_BYTES = 56 * 1024 * 1024

TOKEN_TILE = 256
Q_TILE = 256
BIG_Q_TILE = 512
KV_TILE = 256

F32 = jnp.float32
BF16 = jnp.bfloat16

RM = dict(qa=0, qb=512, qd=768, u=1024, v=1280, gate=1536, cmp=1664)
TR = dict(cmp=0, sel=128, win=256, sb=384, diff=896)


def _dot(a, b):
    return jnp.dot(a, b, preferred_element_type=F32)


def _dot_nt(a, b):
    return lax.dot_general(a, b, (((1,), (1,)), ((), ())), preferred_element_type=F32)


def _iota(shape, dim):
    return lax.broadcasted_iota(jnp.int32, shape, dim)


def _gelu(x):
    return 0.5 * x * (1.0 + jnp.tanh(0.7978845608028654 * (x + 0.044715 * (x * x * x))))


def _sigmoid(x):
    return 1.0 / (1.0 + jnp.exp(-x))


def _layer_norm(x, g, b):
    mu = jnp.mean(x, axis=-1, keepdims=True)
    xc = x - mu
    var = jnp.mean(xc * xc, axis=-1, keepdims=True)
    return xc * lax.rsqrt(var + LN_EPS) * g + b


def _softplus(z):
    return jnp.maximum(z, 0.0) + jnp.log(1.0 + jnp.exp(-jnp.abs(z)))


def _split_bf16(x):
    hi = x.astype(BF16)
    lo = (x - hi.astype(F32)).astype(BF16)
    return hi, lo


def _suffix_matrix(n):
    return (_iota((n, n), 0) >= _iota((n, n), 1)).astype(BF16)


def _rope(h, cos, sin, half, axis):
    n = h.shape[axis]
    above = pltpu.roll(h, n - half, axis)
    below = pltpu.roll(h, half, axis)
    first = (_iota(h.shape, axis) & (2 * half - 1)) < half
    return h * cos + jnp.where(first, -above, below) * sin


def _params(sem):
    return pltpu.CompilerParams(dimension_semantics=sem, vmem_limit_bytes=VMEM_LIMIT_BYTES)


def _const_spec(shape):
    zeros = (0,) * len(shape)
    return pl.BlockSpec(shape, lambda *_: zeros, pipeline_mode=pl.Buffered(1))


def _proj_common(xb, wrm_ref, cq_ref, sq_ref, cd_ref, sd_ref, gb_ref, lng_ref, lnb_ref,
                 qaf_ref, qb_ref, qd_ref, g_ref):
    def rm(name, k=0, width=128):
        a = RM[name] + k * 128
        return _dot(xb, wrm_ref[:, a:a + width])

    cq, sq = cq_ref[...], sq_ref[...]
    for h in range(N_HEADS):
        qaf_ref[:, h * 128:(h + 1) * 128] = _rope(rm("qa", h), cq, sq, ROT_HALF, 1).astype(BF16)
    qb_ref[...] = rm("qb", 0, 256).astype(BF16)
    cd, sd = cd_ref[...], sd_ref[...]
    for k in range(2):
        qd_ref[:, k * 128:(k + 1) * 128] = _rope(rm("qd", k), cd, sd, DIFF_ROT_HALF, 1).astype(BF16)
    g_ref[...] = _sigmoid(rm("gate") + gb_ref[...])
    u = _gelu(rm("u", 0, 256))
    vn = _layer_norm(_gelu(rm("v", 0, 256)), lng_ref[...], lnb_ref[...])
    return rm, u, vn


def _proj_feature_major(xb, wt_ref, cta_ref, sta_ref, ctd_ref, std_ref):
    def tr(name, k=0, height=128):
        a = TR[name] + k * 128
        return _dot_nt(wt_ref[a:a + height, :], xb)

    cta, sta = cta_ref[...], sta_ref[...]
    ctd, std = ctd_ref[...], std_ref[...]
    cmp_t = tr("cmp")
    sel_t = _rope(tr("sel"), cta, sta, ROT_HALF, 0)
    win_t = _rope(tr("win"), cta, sta, ROT_HALF, 0)
    sb_t = tr("sb", 0, 512)
    kd_t = [_rope(tr("diff", k), ctd, std, DIFF_ROT_HALF, 0) for k in range(2)]
    vd_t = tr("diff", 2, 256)
    return cmp_t, sel_t, win_t, sb_t, kd_t, vd_t


def _proj_prompt_kernel(x_ref, wrm_ref, wt_ref, cq_ref, sq_ref, cd_ref, sd_ref,
                        cta_ref, sta_ref, ctd_ref, std_ref, gb_ref, lng_ref, lnb_ref, ws_ref, bsm_ref,
                        qaf_ref, qb_ref, qd_ref, g_ref, od_ref, cmprm_ref,
                        cmpt_ref, selt_ref, wint_ref, sbt_ref, difft_ref,
                        selbf_ref, winbf_ref, sbbf_ref, diffbf_ref, *, tm, tk):
    xb = x_ref[...].astype(BF16)
    rm, u, vn = _proj_common(xb, wrm_ref, cq_ref, sq_ref, cd_ref, sd_ref, gb_ref, lng_ref, lnb_ref,
                             qaf_ref, qb_ref, qd_ref, g_ref)
    cmprm_ref[...] = rm("cmp")

    tril = _iota((CHUNK, CHUNK), 1) <= _iota((CHUNK, CHUNK), 0)
    lane_group = _iota((CHUNK, GROUP_WIDTH), 1) >> 6
    wsg = [jnp.where(tril, ws_ref[g], 0.0).astype(BF16) for g in range(N_HEADS)]
    bsm = bsm_ref[...]
    for c in range(tm // CHUNK):
        rows = slice(c * CHUNK, (c + 1) * CHUNK)
        vnc = vn[rows, :]
        mixed = bsm
        for g in range(N_HEADS):
            mixed = mixed + _dot(wsg[g], jnp.where(lane_group == g, vnc, 0.0).astype(BF16))
        od_ref[rows, :] = (u[rows, :] * mixed).astype(BF16)

    cmp_t, sel_t, win_t, sb_t, kd_t, vd_t = _proj_feature_major(
        xb, wt_ref, cta_ref, sta_ref, ctd_ref, std_ref)
    cmpt_ref[...] = cmp_t
    selt_ref[...] = sel_t
    wint_ref[...] = win_t
    sbt_ref[...] = sb_t
    difft_ref[0:128, :] = kd_t[0]
    difft_ref[128:256, :] = kd_t[1]
    difft_ref[256:512, :] = vd_t
    for t in range(tm // tk):
        cols = slice(t * tk, (t + 1) * tk)
        selbf_ref[t] = sel_t[:, cols].astype(BF16)
        winbf_ref[t] = win_t[:, cols].astype(BF16)
        sbbf_ref[t] = sb_t[:, cols].astype(BF16)
        diffbf_ref[t, 0:128, :] = kd_t[0][:, cols].astype(BF16)
        diffbf_ref[t, 128:256, :] = kd_t[1][:, cols].astype(BF16)
        diffbf_ref[t, 256:512, :] = vd_t[:, cols].astype(BF16)


def _proj_sample_kernel(x_ref, wrm_ref, wt_ref, cq_ref, sq_ref, cd_ref, sd_ref,
                        cta_ref, sta_ref, ctd_ref, std_ref, gb_ref, lng_ref, lnb_ref, w00_ref, b0_ref,
                        qaf_ref, qb_ref, qd_ref, g_ref, od_ref, vn_ref,
                        selrm_ref, winrm_ref, diffrm_ref,
                        cmpt_ref, selt_ref, wint_ref, sbt_ref, difft_ref):
    xb = x_ref[...].astype(BF16)
    rm, u, vn = _proj_common(xb, wrm_ref, cq_ref, sq_ref, cd_ref, sd_ref, gb_ref, lng_ref, lnb_ref,
                             qaf_ref, qb_ref, qd_ref, g_ref)
    vn_ref[...] = vn
    od_ref[...] = (u * (vn * w00_ref[...] + b0_ref[...])).astype(BF16)

    def rows(name, k=0, width=128):
        a = TR[name] + k * 128
        return _dot_nt(xb, wt_ref[a:a + width, :])

    cq, sq = cq_ref[...], sq_ref[...]
    cd, sd = cd_ref[...], sd_ref[...]
    selrm_ref[...] = _rope(rows("sel"), cq, sq, ROT_HALF, 1)
    winrm_ref[...] = _rope(rows("win"), cq, sq, ROT_HALF, 1)
    for k in range(2):
        diffrm_ref[:, k * 128:(k + 1) * 128] = _rope(rows("diff", k), cd, sd, DIFF_ROT_HALF, 1)
    diffrm_ref[:, 256:512] = rows("diff", 2, 256)

    cmp_t, sel_t, win_t, sb_t, kd_t, vd_t = _proj_feature_major(
        xb, wt_ref, cta_ref, sta_ref, ctd_ref, std_ref)
    cmpt_ref[...] = cmp_t
    selt_ref[...] = sel_t
    wint_ref[...] = win_t
    sbt_ref[...] = sb_t
    difft_ref[0:128, :] = kd_t[0]
    difft_ref[128:256, :] = kd_t[1]
    difft_ref[256:512, :] = vd_t


def _proj_prompt(x, wrm, wt, tabs, gb, lng, lnb, ws, bsm):
    B, S, _ = x.shape
    tm, tk = TOKEN_TILE, KV_TILE
    nk = S // tk
    row = lambda n: pl.BlockSpec((None, tm, n), lambda b, i: (b, i, 0))
    feat = lambda n: pl.BlockSpec((None, n, tm), lambda b, i: (b, 0, i))
    tile = lambda n: pl.BlockSpec((None, tm // tk, n, tk), lambda b, i: (b, i, 0, 0))
    tab_rm = pl.BlockSpec((tm, 128), lambda b, i: (i, 0))
    tab_t = pl.BlockSpec((128, tm), lambda b, i: (0, i))
    in_specs = [row(D_MODEL), _const_spec(wrm.shape), _const_spec(wt.shape),
                tab_rm, tab_rm, tab_rm, tab_rm, tab_t, tab_t, tab_t, tab_t,
                _const_spec(gb.shape), _const_spec(lng.shape), _const_spec(lnb.shape),
                _const_spec(ws.shape), _const_spec(bsm.shape)]
    sd = jax.ShapeDtypeStruct
    out_shape = [sd((B, S, 512), BF16), sd((B, S, 256), BF16), sd((B, S, 256), BF16),
                 sd((B, S, 128), F32), sd((B, S, 256), BF16), sd((B, S, 128), F32),
                 sd((B, 128, S), F32), sd((B, 128, S), F32), sd((B, 128, S), F32),
                 sd((B, 512, S), F32), sd((B, 512, S), F32),
                 sd((B, nk, 128, tk), BF16), sd((B, nk, 128, tk), BF16),
                 sd((B, nk, 512, tk), BF16), sd((B, nk, 512, tk), BF16)]
    out_specs = [row(512), row(256), row(256), row(128), row(256), row(128),
                 feat(128), feat(128), feat(128), feat(512), feat(512),
                 tile(128), tile(128), tile(512), tile(512)]
    return pl.pallas_call(
        functools.partial(_proj_prompt_kernel, tm=tm, tk=tk),
        grid=(B, S // tm), in_specs=in_specs, out_specs=out_specs, out_shape=out_shape,
        compiler_params=_params(("parallel", "parallel")), name="proj_prompt",
    )(x, wrm, wt, *tabs, gb, lng, lnb, ws, bsm)


def _proj_sample(x, wrm, wt, tabs, gb, lng, lnb, w00, b0):
    nb = x.shape[0]
    full = lambda a: pl.BlockSpec(a.shape, lambda i: (0,) * a.ndim)
    ins = (x, wrm, wt, *tabs, gb, lng, lnb, w00, b0)
    sd = jax.ShapeDtypeStruct
    out_shape = [sd((nb, 512), BF16), sd((nb, 256), BF16), sd((nb, 256), BF16), sd((nb, 128), F32),
                 sd((nb, 256), BF16), sd((nb, 256), F32),
                 sd((nb, 128), F32), sd((nb, 128), F32), sd((nb, 512), F32),
                 sd((128, nb), F32), sd((128, nb), F32), sd((128, nb), F32),
                 sd((512, nb), F32), sd((512, nb), F32)]
    return pl.pallas_call(
        _proj_sample_kernel, grid=(1,), in_specs=[full(a) for a in ins],
        out_specs=[pl.BlockSpec(s.shape, lambda i: (0, 0)) for s in out_shape], out_shape=out_shape,
        compiler_params=_params(("arbitrary",)), name="proj_sample",
    )(*ins)


def _compress_rows(rows_ref, first, stride, n_out, pe_ref, w1_ref, w2_ref):
    h = jnp.zeros((n_out, 128), F32)
    for r in range(CMP_BLOCK):
        x = rows_ref[pl.ds(first + r, n_out, stride=stride), :] + pe_ref[r:r + 1, :]
        h = h + _dot(x.astype(BF16), w1_ref[r])
    return _dot(_gelu(h).astype(BF16), w2_ref[...])


def _compress_prompt_kernel(x_ref, pe_ref, w1_ref, w2_ref, o_ref, *, ns):
    for parity in range(2):
        o_ref[parity * ns:(parity + 1) * ns, :] = _compress_rows(
            x_ref, parity * CMP_BLOCK, SEL_BLOCK, ns, pe_ref, w1_ref, w2_ref).astype(BF16)


def _compress_prompt(cmp_rm, pe, w1, w2):
    B, S, _ = cmp_rm.shape
    ns = S // SEL_BLOCK
    return pl.pallas_call(
        functools.partial(_compress_prompt_kernel, ns=ns), grid=(B,),
        in_specs=[pl.BlockSpec((None, S, 128), lambda b: (b, 0, 0)), _const_spec(pe.shape),
                  _const_spec(w1.shape), _const_spec(w2.shape)],
        out_specs=pl.BlockSpec((None, 2 * ns, 128), lambda b: (b, 0, 0)),
        out_shape=jax.ShapeDtypeStruct((B, 2 * ns, 128), BF16),
        compiler_params=_params(("parallel",)), name="compress_prompt",
    )(cmp_rm, pe, w1, w2)


def _compress_pool_kernel(x_ref, pe_ref, w1_ref, w2_ref, o_ref, rows_ref, *, pages):
    for p in range(pages):
        rows_ref[p * 128:(p + 1) * 128, :] = x_ref[p].T
    n_out = pages * (128 // CMP_BLOCK)
    o_ref[...] = _compress_rows(rows_ref, 0, CMP_BLOCK, n_out, pe_ref, w1_ref, w2_ref)


def _compress_pool(layer, cmp_cache, pe, w1, w2):
    n_pool = cmp_cache.shape[1]
    pages = math.gcd(n_pool, 64)
    n_out = pages * (128 // CMP_BLOCK)
    return pl.pallas_call(
        functools.partial(_compress_pool_kernel, pages=pages), grid=(n_pool // pages,),
        in_specs=[pl.BlockSpec((None, pages, 128, 128), lambda g: (layer, g, 0, 0)), _const_spec(pe.shape),
                  _const_spec(w1.shape), _const_spec(w2.shape)],
        out_specs=pl.BlockSpec((n_out, 128), lambda g: (g, 0)),
        out_shape=jax.ShapeDtypeStruct((n_pool * (128 // CMP_BLOCK), 128), F32),
        scratch_shapes=[pltpu.VMEM((pages * 128, 128), F32)],
        compiler_params=_params(("parallel",)), name="compress_pool",
    )(cmp_cache, pe, w1, w2)


def _select_topk(score, blk_f, n_blocks, k):
    work = score
    for _ in range(k):
        mx = jnp.max(work, axis=0, keepdims=True)
        idx = jnp.min(jnp.where(work == mx, blk_f, float(n_blocks)), axis=0, keepdims=True)
        work = jnp.where(blk_f == idx, NEG_BIG, work)
    return jnp.where(work == NEG_BIG, 1.0, 0.0)


def _select_topk_row(score, k):
    n = score.shape[1]
    mine = jnp.broadcast_to(score, (n, n))
    other = mine.T
    beats = (other > mine) | ((other == mine) & (_iota((n, n), 0) < _iota((n, n), 1)))
    rank = jnp.sum(jnp.where(beats, 1.0, 0.0), axis=0, keepdims=True)
    return jnp.where(rank < k, 1.0, 0.0)


def _masked_softmax(logits, mask):
    lm = jnp.where(mask, logits, NEG_INF)
    mx = jnp.max(lm, axis=1, keepdims=True)
    e = jnp.where(mask, jnp.exp(lm - mx), 0.0)
    return e / jnp.maximum(jnp.sum(e, axis=1, keepdims=True), TINY)


def _cmp_block_end(n_half):
    r = _iota((1, 2 * n_half), 1)
    return jnp.where(r < n_half, 64 * r + 31, 64 * (r - n_half) + 63)


def _nsa_prompt_kernel(qf_ref, g_ref, kvc_ref, sel_ref, win_ref, o_ref, qs_ref, acc_ref, m_ref, l_ref, ex_ref,
                       bias_ref, *, tq, sub, ns, topk):
    tk = tq
    i = pl.program_id(1)
    tpos = i * tq + _iota((tq, 1), 0)
    qh = [qf_ref[:, h * 128:(h + 1) * 128] for h in range(N_HEADS)]
    kvc = kvc_ref[...]

    for h in range(N_HEADS):
        qs_ref[h * tq:(h + 1) * tq, :] = qh[h]

    maskc = jnp.broadcast_to((_cmp_block_end(ns) <= tpos)[None], (N_HEADS, tq, 2 * ns)).reshape(N_HEADS * tq, 2 * ns)
    pc = _masked_softmax(_dot_nt(qs_ref[...], kvc), maskc)
    oc_all = _dot(pc.astype(BF16), kvc)
    oc = [oc_all[h * tq:(h + 1) * tq, :] for h in range(N_HEADS)]
    impc = functools.reduce(jnp.add, [pc[h * tq:(h + 1) * tq, :] for h in range(N_HEADS)])
    imp = impc[:, :ns] + impc[:, ns:]

    blk = _iota((ns, tq), 0)
    tpos_row = i * tq + _iota((1, tq), 1)
    cur = tpos_row >> 6
    forced = (blk == 0) | (blk == cur) | (blk == cur - 1)
    valid = (blk << 6) <= tpos_row
    score = jnp.where(forced, FORCE_SCORE, jnp.where(valid, imp.T, -1.0))
    sel_bf = _select_topk(score, blk.astype(F32), ns, topk).T.astype(BF16)

    row, col = _iota((tq, tk), 0), _iota((tq, tk), 1)

    def chosen_in_tile(j):
        expand = (_iota((ns, tk), 0) == j * (tk // SEL_BLOCK) + (_iota((ns, tk), 1) >> 6))
        return _dot(sel_bf, expand.astype(BF16)) > 0.5

    def sel_bias_step(j, carry):
        bias_ref[j] = jnp.where(chosen_in_tile(j), 0.0, NEG_INF).astype(BF16)
        return carry

    lax.fori_loop(0, i, sel_bias_step, 0)
    bias_ref[i] = jnp.where(chosen_in_tile(i) & (col <= row), 0.0, NEG_INF).astype(BF16)

    def attend(kv_ref, j_lo, bias_of):
        def scores(j):
            kt = _key_rows(kv_ref, j, sub, 0, 128)
            s = _dot(qs_ref[...], kt).reshape(N_HEADS, tq, tk) + bias_of(j)[None]
            return s.reshape(N_HEADS * tq, tk), kt

        def max_step(j, carry):
            s, _ = scores(j)
            m_ref[...] = jnp.maximum(m_ref[...], _chunk_reduce(s, jnp.maximum))
            return carry

        def sum_step(j, carry, track=False, scored=None):
            s, kt = scores(j) if scored is None else scored
            m = m_ref[...]
            d = [c - m for c in _lane_chunks(s)]
            if track:
                ex_ref[...] = jnp.maximum(ex_ref[...], functools.reduce(jnp.maximum, d))
            p = [jnp.exp(c) for c in d]
            l_ref[...] += functools.reduce(jnp.add, p)
            acc_ref[...] += _dot_nt(jnp.concatenate(p, axis=1).astype(BF16), kt)
            return carry

        def reset_sums():
            acc_ref[...] = jnp.zeros(acc_ref.shape, F32)
            l_ref[...] = jnp.zeros(l_ref.shape, F32)

        def row_max_to_lanes():
            m_ref[...] = jnp.broadcast_to(jnp.max(m_ref[...], axis=1, keepdims=True), m_ref.shape)

        diag = scores(i)
        m_ref[...] = _chunk_reduce(diag[0], jnp.maximum)
        row_max_to_lanes()
        reset_sums()
        ex_ref[...] = jnp.zeros(ex_ref.shape, F32)
        sum_step(i, 0, scored=diag)
        lax.fori_loop(j_lo, i, functools.partial(sum_step, track=True), 0)

        @pl.when(jnp.max(ex_ref[...]) > EXP_HEADROOM)
        def _():
            m_ref[...] = jnp.full(m_ref.shape, NEG_INF, F32)
            lax.fori_loop(j_lo, i + 1, max_step, 0)
            row_max_to_lanes()
            reset_sums()
            lax.fori_loop(j_lo, i + 1, sum_step, 0)

        out = acc_ref[...] / jnp.maximum(jnp.sum(l_ref[...], axis=1, keepdims=True), TINY)
        return [out[h * tq:(h + 1) * tq, :] for h in range(N_HEADS)]

    def win_bias(j):
        d = (i - j) * tk + row - col
        return jnp.where((d >= 0) & (d < WINDOW), 0.0, NEG_INF)

    osel = attend(sel_ref, 0, lambda j: bias_ref[j].astype(F32))
    ow = attend(win_ref, jnp.maximum(i - WINDOW // tk, 0), win_bias)

    g = g_ref[...]
    lane = _iota((tq, 128), 1)
    mixed = [g[:, 3 * h:3 * h + 1] * oc[h] + g[:, 3 * h + 1:3 * h + 2] * osel[h]
             + g[:, 3 * h + 2:3 * h + 3] * ow[h] for h in range(N_HEADS)]
    for k in range(2):
        even = pltpu.roll(mixed[2 * k], 64, 1)
        o_ref[:, k * 128:(k + 1) * 128] = jnp.where(lane < 64, even, mixed[2 * k + 1]).astype(BF16)


def _nsa_prompt(qaf, gates, kvc, selbf, winbf):
    B, S, _ = qaf.shape
    tq = math.gcd(S, BIG_Q_TILE)
    ns = S // SEL_BLOCK
    kern = functools.partial(_nsa_prompt_kernel, tq=tq, sub=tq // KV_TILE, ns=ns, topk=min(SEL_TOPK, ns))
    row = lambda n: pl.BlockSpec((None, tq, n), lambda b, i: (b, i, 0))
    whole = lambda a: pl.BlockSpec((None,) + a.shape[1:], lambda b, i: (b,) + (0,) * (a.ndim - 1))
    return pl.pallas_call(
        kern, grid=(B, S // tq),
        in_specs=[row(512), row(128), whole(kvc), whole(selbf), whole(winbf)],
        out_specs=row(256), out_shape=jax.ShapeDtypeStruct((B, S, 256), BF16),
        scratch_shapes=[pltpu.VMEM((N_HEADS * tq, 128), BF16), pltpu.VMEM((N_HEADS * tq, 128), F32),
                        pltpu.VMEM((N_HEADS * tq, 128), F32), pltpu.VMEM((N_HEADS * tq, 128), F32),
                        pltpu.VMEM((N_HEADS * tq, 128), F32), pltpu.VMEM((S // tq, tq, tq), BF16)],
        compiler_params=_params(("parallel", "arbitrary")), name="nsa_prompt",
    )(qaf, gates, kvc, selbf, winbf)


def _sb_prompt_kernel(q_ref, kv_ref, o_ref, qs_ref, acc_ref, car_ref, *, tq, tk):
    assert tq == tk
    i = pl.program_id(1)
    rows = N_HEADS * tq
    q = q_ref[...].astype(F32)
    lane_head = _iota((1, GROUP_WIDTH), 1) >> 6
    for h in range(N_HEADS):
        qs_ref[h * tq:(h + 1) * tq, :] = jnp.where(lane_head == h, q, 0.0).astype(BF16)
    suffix = _suffix_matrix(tk)
    acc_ref[...] = jnp.zeros(acc_ref.shape, F32)
    car_ref[...] = jnp.zeros(car_ref.shape, F32)

    def step(j, strict):
        z = _dot(qs_ref[...], kv_ref[j, 0:256, :])
        log_keep = -_softplus(z)
        if strict is not None:
            keep = jnp.broadcast_to(strict[None], (N_HEADS, tq, tk)).reshape(rows, tk)
            log_keep = jnp.where(keep, log_keep, 0.0)
            z = jnp.where(keep, z, NEG_INF)
        hi, lo = _split_bf16(log_keep)
        incl = _dot(hi, suffix) + _dot(lo, suffix)
        car = car_ref[...]
        ab = jnp.exp(z + incl + jnp.concatenate([car, car], axis=1)).astype(BF16)
        for k in range(2):
            pair = slice(2 * k * tq, (2 * k + 2) * tq)
            acc_ref[pair, :] += _dot_nt(ab[pair, :], kv_ref[j, 256 + k * 128:384 + k * 128, :])
        car_ref[...] = car + jnp.broadcast_to(incl[:, 0:1], (rows, 128))

    def any_weight_left():
        return jnp.max(car_ref[...]) > SB_UNDERFLOW

    step(i, _iota((tq, tk), 1) < _iota((tq, tk), 0))

    def left_tile(state):
        j, _ = state
        step(j, None)
        return j - 1, any_weight_left()

    lax.while_loop(lambda state: (state[0] >= 0) & state[1], left_tile, (i - 1, any_weight_left()))
    lane = _iota((tq, 128), 1)
    for k in range(2):
        even, odd = acc_ref[2 * k * tq:(2 * k + 1) * tq, :], acc_ref[(2 * k + 1) * tq:(2 * k + 2) * tq, :]
        o_ref[:, k * 128:(k + 1) * 128] = jnp.where(lane < 64, even, odd).astype(BF16)


def _sb_prompt(qb, sbbf):
    B, S, _ = qb.shape
    tq, tk = Q_TILE, KV_TILE
    row = lambda n: pl.BlockSpec((None, tq, n), lambda b, i: (b, i, 0))
    return pl.pallas_call(
        functools.partial(_sb_prompt_kernel, tq=tq, tk=tk), grid=(B, S // tq),
        in_specs=[row(256), pl.BlockSpec((None,) + sbbf.shape[1:], lambda b, i: (b, 0, 0, 0))],
        out_specs=row(256), out_shape=jax.ShapeDtypeStruct((B, S, 256), BF16),
        scratch_shapes=[pltpu.VMEM((N_HEADS * tq, GROUP_WIDTH), BF16), pltpu.VMEM((N_HEADS * tq, 128), F32),
                        pltpu.VMEM((N_HEADS * tq, 128), F32)],
        compiler_params=_params(("parallel", "arbitrary")), name="sb_prompt",
    )(qb, sbbf)


def _diff_lambda(dl_ref, lam_init):
    lq = dl_ref[...]
    s1 = jnp.sum(lq[0:1, :] * lq[1:2, :], axis=1, keepdims=True)
    s2 = jnp.sum(lq[2:3, :] * lq[3:4, :], axis=1, keepdims=True)
    return jnp.exp(s1) - jnp.exp(s2) + lam_init


def _head_rms_norm(out, lane_head, ng, lam_init):
    sq = out * out
    rs = jnp.zeros(out.shape, F32)
    for h in range(N_HEADS):
        ms = jnp.sum(jnp.where(lane_head == h, sq, 0.0), axis=1, keepdims=True) * (1.0 / HEAD_DIM)
        rs = jnp.where(lane_head == h, lax.rsqrt(ms + LN_EPS), rs)
    return out * rs * ng * (1.0 - lam_init)


def _lane_chunks(x):
    return [x[:, c * LANES:(c + 1) * LANES] for c in range(x.shape[1] // LANES)]


def _chunk_reduce(x, op):
    chunks = _lane_chunks(x)
    out = chunks[0]
    for c in chunks[1:]:
        out = op(out, c)
    return out


def _key_rows(kv_ref, j, sub, r0, r1):
    return jnp.concatenate([kv_ref[j * sub + u, r0:r1, :] for u in range(sub)], axis=1)


def _diff_prompt_kernel(li_ref, dl_ref, ng_ref, q_ref, kv_ref, o_ref, qs_ref, acc_ref, m_ref, l_ref, ex_ref,
                        *, tq, sub):
    tk = tq
    i = pl.program_id(1)
    lam_init = li_ref[0]
    lam = _diff_lambda(dl_ref, lam_init)
    q = q_ref[...].astype(F32)
    lane = _iota((1, GROUP_WIDTH), 1)
    lane_head, lane_half = lane >> 6, (lane >> 5) & 1
    n_soft = 2 * N_HEADS
    rows = n_soft * tq
    for a in range(n_soft):
        qa = jnp.where((lane_head == a // 2) & (lane_half == a % 2), q, 0.0)
        qs_ref[a * tq:(a + 1) * tq, :] = qa.astype(BF16)
    causal_bias = jnp.where(_iota((tq, tk), 1) <= _iota((tq, tk), 0), 0.0, NEG_INF)

    def scores(j, bias):
        s = _dot(qs_ref[...], _key_rows(kv_ref, j, sub, 0, 256))
        if bias is not None:
            s = (s.reshape(n_soft, tq, tk) + bias[None]).reshape(rows, tk)
        return s

    def max_step(j, bias):
        m_ref[...] = jnp.maximum(m_ref[...], _chunk_reduce(scores(j, bias), jnp.maximum))

    def sum_step(j, bias, s=None, track=False):
        s = scores(j, bias) if s is None else s
        m = m_ref[...]
        d = [c - m for c in _lane_chunks(s)]
        if track:
            ex_ref[...] = jnp.maximum(ex_ref[...], functools.reduce(jnp.maximum, d))
        p = [jnp.exp(c) for c in d]
        l_ref[...] += functools.reduce(jnp.add, p)
        pb = jnp.concatenate(p, axis=1).astype(BF16)
        for k in range(2):
            part = slice(k * (rows // 2), (k + 1) * (rows // 2))
            acc_ref[part, :] += _dot_nt(pb[part, :], _key_rows(kv_ref, j, sub, 256 + k * 128, 384 + k * 128))

    def below(step):
        def body(j, carry):
            step(j, None)
            return carry
        lax.fori_loop(0, i, body, 0)

    def reset_sums():
        acc_ref[...] = jnp.zeros(acc_ref.shape, F32)
        l_ref[...] = jnp.zeros(l_ref.shape, F32)

    def row_max_to_lanes():
        m_ref[...] = jnp.broadcast_to(jnp.max(m_ref[...], axis=1, keepdims=True), m_ref.shape)

    s_diag = scores(i, causal_bias)
    m_ref[...] = _chunk_reduce(s_diag, jnp.maximum)
    row_max_to_lanes()
    reset_sums()
    ex_ref[...] = jnp.zeros(ex_ref.shape, F32)
    sum_step(i, None, s=s_diag)
    below(functools.partial(sum_step, track=True))

    @pl.when(jnp.max(ex_ref[...]) > EXP_HEADROOM)
    def _():
        m_ref[...] = jnp.full(m_ref.shape, NEG_INF, F32)
        below(max_step)
        max_step(i, causal_bias)
        row_max_to_lanes()
        reset_sums()
        below(sum_step)
        sum_step(i, causal_bias)

    lane128 = _iota((tq, 128), 1)
    ng = ng_ref[...]
    blk = lambda ref, a: ref[a * tq:(a + 1) * tq, :]
    for k in range(2):
        o = []
        for h in (2 * k, 2 * k + 1):
            l1 = jnp.maximum(jnp.sum(blk(l_ref, 2 * h), axis=1, keepdims=True), TINY)
            l2 = jnp.maximum(jnp.sum(blk(l_ref, 2 * h + 1), axis=1, keepdims=True), TINY)
            o.append(blk(acc_ref, 2 * h) / l1 - lam * (blk(acc_ref, 2 * h + 1) / l2))
        low = lane128 < HEAD_DIM
        out = jnp.where(low, o[0], o[1])
        sq = out * out
        ms_lo = jnp.sum(jnp.where(low, sq, 0.0), axis=1, keepdims=True)
        ms_hi = jnp.sum(jnp.where(low, 0.0, sq), axis=1, keepdims=True)
        rs = lax.rsqrt(jnp.where(low, ms_lo, ms_hi) * (1.0 / HEAD_DIM) + LN_EPS)
        o_ref[:, k * 128:(k + 1) * 128] = (out * rs * ng[:, k * 128:(k + 1) * 128] * (1.0 - lam_init)).astype(BF16)


def _diff_prompt(lam_init, dl, ng, qd, diffbf):
    B, S, _ = qd.shape
    tq = math.gcd(S, BIG_Q_TILE)
    row = lambda n: pl.BlockSpec((None, tq, n), lambda b, i: (b, i, 0))
    return pl.pallas_call(
        functools.partial(_diff_prompt_kernel, tq=tq, sub=tq // KV_TILE), grid=(B, S // tq),
        in_specs=[pl.BlockSpec(memory_space=pltpu.SMEM), _const_spec(dl.shape), _const_spec(ng.shape),
                  row(256), pl.BlockSpec((None,) + diffbf.shape[1:], lambda b, i: (b, 0, 0, 0))],
        out_specs=row(256), out_shape=jax.ShapeDtypeStruct((B, S, 256), BF16),
        scratch_shapes=[pltpu.VMEM((2 * N_HEADS * tq, GROUP_WIDTH), BF16), pltpu.VMEM((2 * N_HEADS * tq, 128), F32),
                        pltpu.VMEM((2 * N_HEADS * tq, 128), F32), pltpu.VMEM((2 * N_HEADS * tq, 128), F32),
                        pltpu.VMEM((2 * N_HEADS * tq, 128), F32)],
        compiler_params=_params(("parallel", "arbitrary")), name="diff_prompt",
    )(lam_init, dl, ng, qd, diffbf)


def _round_bf16(x):
    return x.astype(BF16).astype(F32)


def _softmax_with_self(s, ok, s_self):
    sm = jnp.where(ok, s, NEG_INF)
    mx = jnp.maximum(jnp.max(sm, axis=1, keepdims=True), s_self)
    e = jnp.where(ok, jnp.exp(sm - mx), 0.0)
    e_self = jnp.exp(s_self - mx)
    inv = 1.0 / (jnp.sum(e, axis=1, keepdims=True) + e_self)
    return e * inv, e_self * inv


def _nsa_sample_kernel(pt_ref, qh_ref, gs_ref, snew_ref, wnew_ref, win_ref, *rest,
                       n_pages, past, half, topk):
    del pt_ref
    cmp_refs, sel_refs = rest[:n_pages], rest[n_pages:2 * n_pages]
    o_ref, kvc_ref = rest[2 * n_pages], rest[2 * n_pages + 1]
    qh = qh_ref[...]
    qf = qh.astype(F32)

    kvc_ref[...] = jnp.zeros(kvc_ref.shape, F32)
    for p in range(n_pages):
        piece = cmp_refs[p][...]
        kvc_ref[2 * p:2 * p + 2, :] = piece[0:2, :]
        kvc_ref[half + 2 * p:half + 2 * p + 2, :] = piece[2:4, :]
    kvc = kvc_ref[...].astype(BF16)
    maskc = _cmp_block_end(half) <= past
    pc = _masked_softmax(_dot_nt(qh, kvc), maskc)
    oc = _dot(pc.astype(BF16), kvc)
    head_row = _iota((8, 1), 0) < N_HEADS
    impc = jnp.sum(jnp.where(head_row, pc, 0.0), axis=0, keepdims=True)
    imp = impc[:, :half] + impc[:, half:]

    n_sel = past // SEL_BLOCK + 1
    blk = _iota((1, half), 1)
    cur = past // SEL_BLOCK
    forced = (blk == 0) | (blk == cur) | (blk == cur - 1)
    valid = blk * SEL_BLOCK <= past
    score = jnp.where(blk < n_sel, jnp.where(forced, FORCE_SCORE, jnp.where(valid, imp, -1.0)), NEG_BIG)
    sel8 = jnp.broadcast_to(_select_topk_row(score, topk), (8, half)).astype(BF16)
    expand = _iota((half, past), 0) == (_iota((half, past), 1) >> 6)
    chosen = _dot(sel8, expand.astype(BF16)) > 0.5

    def branch(kv, new_row, ok):
        new_b = _round_bf16(new_row)
        s_self = jnp.sum(qf * new_b, axis=1, keepdims=True)
        p, p_self = _softmax_with_self(_dot(qh, kv), ok, s_self)
        return p_self * new_b + _dot_nt(p.astype(BF16), kv)

    osel = branch(jnp.concatenate([r[...].astype(BF16) for r in sel_refs], axis=1), snew_ref[...], chosen)
    w_past = win_ref.shape[1]
    in_window = _iota((1, w_past), 1) >= (w_past - WINDOW + 1)
    ow = branch(win_ref[...].astype(BF16), wnew_ref[...], in_window)
    g = gs_ref[...]
    o_ref[...] = g[:, 0:1] * oc + g[:, 1:2] * osel + g[:, 2:3] * ow


def _sb_sample_kernel(pt_ref, q_ref, *rest, n_pages, past, cb):
    del pt_ref
    pages, o_ref = rest[:n_pages], rest[n_pages]
    q8 = jnp.broadcast_to(q_ref[...].astype(F32), (8, GROUP_WIDTH))
    row_head = _iota((8, GROUP_WIDTH), 0) == (_iota((8, GROUP_WIDTH), 1) >> 6)
    qm = jnp.where(row_head, q8, 0.0).astype(BF16)
    kt = jnp.concatenate([pages[p][0].astype(BF16) for p in range(n_pages)], axis=1)
    vt = jnp.concatenate([pages[p][1].astype(BF16) for p in range(n_pages)], axis=1)
    z = _dot(qm, kt)
    log_keep = -_softplus(z)
    n_blk = past // cb
    stack = lambda x: jnp.concatenate([x[:, b * cb:(b + 1) * cb] for b in range(n_blk)], axis=0)
    hi, lo = _split_bf16(stack(log_keep))
    suffix = _suffix_matrix(cb)
    incl = _dot(hi, suffix) + _dot(lo, suffix)
    carry = jnp.zeros((8, 1), F32)
    carries = [None] * n_blk
    for b in reversed(range(n_blk)):
        carries[b] = carry
        carry = carry + incl[b * 8:(b + 1) * 8, 0:1]
    a = jnp.exp(stack(z) + incl + jnp.concatenate(carries, axis=0))
    a = jnp.concatenate([a[b * 8:(b + 1) * 8, :] for b in range(n_blk)], axis=1)
    out = _dot_nt(a.astype(BF16), vt)
    o_ref[...] = jnp.sum(jnp.where(row_head, out, 0.0), axis=0, keepdims=True).astype(BF16)


def _diff_sample_kernel(pt_ref, li_ref, dl_ref, ng_ref, q_ref, dnew_ref, *rest, n_pages):
    del pt_ref
    pages, o_ref = rest[:n_pages], rest[n_pages]
    lam_init = li_ref[0]
    lam = _diff_lambda(dl_ref, lam_init)
    q8 = jnp.broadcast_to(q_ref[...].astype(F32), (8, GROUP_WIDTH))
    lane = _iota((8, GROUP_WIDTH), 1)
    row_head = _iota((8, GROUP_WIDTH), 0) == (lane >> 6)
    lane_half = (lane >> 5) & 1
    new_k = _round_bf16(dnew_ref[:, 0:256])
    new_v = _round_bf16(dnew_ref[:, 256:512])
    kt = jnp.concatenate([pages[p][0].astype(BF16) for p in range(n_pages)], axis=1)
    vt = jnp.concatenate([pages[p][1].astype(BF16) for p in range(n_pages)], axis=1)
    ps, ps_self = [], []
    for m in range(2):
        qm = jnp.where(row_head & (lane_half == m), q8, 0.0)
        s = _dot(qm.astype(BF16), kt)
        s_self = jnp.sum(qm * new_k, axis=1, keepdims=True)
        p, p_self = _softmax_with_self(s, jnp.full(s.shape, True), s_self)
        ps.append(p)
        ps_self.append(p_self)
    out = (ps_self[0] - lam * ps_self[1]) * new_v + _dot_nt((ps[0] - lam * ps[1]).astype(BF16), vt)
    o4 = jnp.sum(jnp.where(row_head, out, 0.0), axis=0, keepdims=True)
    o_ref[...] = _head_rms_norm(o4, _iota((1, GROUP_WIDTH), 1) >> 6, ng_ref[...], lam_init).astype(BF16)


def _page_spec(block, layer, p):
    nz = (0,) * (len(block) - 2)
    return pl.BlockSpec(block, lambda b, pt: (layer, pt[b, p]) + nz)


def _sample_mix_kernel(pt_ref, qh_ref, gs_ref, snew_ref, wnew_ref, win_ref, qb_ref, li_ref, dl_ref, ng_ref,
                       qd_ref, dnew_ref, *rest, n_pages, past, half, topk, cb):
    n = n_pages
    cmp_refs, sel_refs, sb_refs, diff_refs = rest[:n], rest[n:2 * n], rest[2 * n:3 * n], rest[3 * n:4 * n]
    oa_ref, ob_ref, oc_ref, kvc_ref = rest[4 * n:4 * n + 4]
    _nsa_sample_kernel(pt_ref, qh_ref, gs_ref, snew_ref, wnew_ref, win_ref, *cmp_refs, *sel_refs, oa_ref, kvc_ref,
                       n_pages=n, past=past, half=half, topk=topk)
    _sb_sample_kernel(pt_ref, qb_ref, *sb_refs, ob_ref, n_pages=n, past=past, cb=cb)
    _diff_sample_kernel(pt_ref, li_ref, dl_ref, ng_ref, qd_ref, dnew_ref, *diff_refs, oc_ref, n_pages=n)


def _sample_mix(layer, page_table, qh, gs, snew, wnew, cmp_pool, sel_cache, win_cache,
                qb, sb_cache, lam_init, dl, ng, qd, dnew, diff_cache):
    nb, n_pages = page_table.shape
    past = n_pages * 128
    half = max(128, -(-(past // SEL_BLOCK + 1) // 128) * 128)
    n_sel = past // SEL_BLOCK + 1
    kern = functools.partial(_sample_mix_kernel, n_pages=n_pages, past=past, half=half,
                             topk=min(SEL_TOPK, n_sel), cb=math.gcd(past, 256))
    per_seq = lambda a: pl.BlockSpec((None,) + a.shape[1:], lambda b, pt: (b,) + (0,) * (a.ndim - 1))
    const = lambda a: pl.BlockSpec(a.shape, lambda b, pt: (0,) * a.ndim)
    in_specs = [per_seq(qh), per_seq(gs), per_seq(snew), per_seq(wnew),
                pl.BlockSpec((None, None) + win_cache.shape[2:], lambda b, pt: (layer, b, 0, 0)),
                per_seq(qb), pl.BlockSpec(memory_space=pltpu.SMEM), const(dl), const(ng),
                per_seq(qd), per_seq(dnew)]
    in_specs += [pl.BlockSpec((None, 4, 128), functools.partial(lambda b, pt, p: (pt[b, p], 0, 0), p=p))
                 for p in range(n_pages)]
    in_specs += [_page_spec((None, None, 128, 128), layer, p) for p in range(n_pages)]
    in_specs += [_page_spec((None, None, 2, 256, 128), layer, p) for p in range(n_pages)]
    in_specs += [_page_spec((None, None, 2, 256, 128), layer, p) for p in range(n_pages)]
    row_out = pl.BlockSpec((None, 1, 256), lambda b, pt: (b, 0, 0))
    grid_spec = pltpu.PrefetchScalarGridSpec(
        num_scalar_prefetch=1, grid=(nb,), in_specs=in_specs,
        out_specs=[pl.BlockSpec((None, 8, 128), lambda b, pt: (b, 0, 0)), row_out, row_out],
        scratch_shapes=[pltpu.VMEM((2 * half, 128), F32)])
    return pl.pallas_call(
        kern, grid_spec=grid_spec,
        out_shape=[jax.ShapeDtypeStruct((nb, 8, 128), F32), jax.ShapeDtypeStruct((nb, 1, 256), BF16),
                   jax.ShapeDtypeStruct((nb, 1, 256), BF16)],
        compiler_params=_params(("arbitrary",)), name="sample_mix",
    )(page_table, qh, gs, snew, wnew, win_cache, qb, lam_init, dl, ng, qd, dnew,
      *([cmp_pool] * n_pages), *([sel_cache] * n_pages), *([sb_cache] * n_pages), *([diff_cache] * n_pages))


def _tail_kernel(x_ref, oa_ref, ob_ref, oc_ref, od_ref, wo_ref, g1_ref, b1_ref,
                 wg_ref, wu_ref, wd_ref, g2_ref, b2_ref, y_ref, *, alpha):
    mix = (_dot(oa_ref[...], wo_ref[0:256, :]) + _dot(ob_ref[...], wo_ref[256:512, :])
           + _dot(oc_ref[...], wo_ref[512:768, :]) + _dot(od_ref[...], wo_ref[768:1024, :]))
    x1 = _layer_norm(alpha * x_ref[...] + mix, g1_ref[...], b1_ref[...])
    x1b = x1.astype(BF16)
    hg = _dot(x1b, wg_ref[...])
    hu = _dot(x1b, wu_ref[...])
    act = (hg * _sigmoid(hg) * hu).astype(BF16)
    y_ref[...] = _layer_norm(alpha * x1 + _dot(act, wd_ref[...]), g2_ref[...], b2_ref[...])


def _tail(x, oa, ob, oc, od, wo, g1, b1, wg, wu, wd, g2, b2, alpha):
    m = x.shape[0]
    tm = math.gcd(m, TOKEN_TILE)
    row = lambda n: pl.BlockSpec((tm, n), lambda i: (i, 0))
    consts = (wo, g1, b1, wg, wu, wd, g2, b2)
    return pl.pallas_call(
        functools.partial(_tail_kernel, alpha=alpha), grid=(m // tm,),
        in_specs=[row(D_MODEL), row(256), row(256), row(256), row(256)] + [_const_spec(c.shape) for c in consts],
        out_specs=row(D_MODEL), out_shape=jax.ShapeDtypeStruct((m, D_MODEL), F32),
        compiler_params=_params(("parallel",)), name="tail",
    )(x, oa, ob, oc, od, *consts)


def _pad_heads(w):
    k = w.shape[0]
    wr = w.reshape(k, -1, HEAD_DIM)
    return jnp.concatenate([wr, jnp.zeros_like(wr)], axis=-1).reshape(k, -1)


def _layer_weights(w):
    off = [0]
    for n in (256, 64, 64, 64, 64, 64, 64, 12, 256, 256, 256, 256, 256, 256, 256, 256):
        off.append(off[-1] + n)
    col = lambda i: w[:, off[i]:off[i + 1]]
    qa, kc, vc, ks, vs, kw, vw, ga, qb, kb, vb, qd, kd, vd, u, v = [col(i) for i in range(16)]
    qa = qa * (HEAD_DIM ** -0.5)
    qb = qb * (HEAD_DIM ** -0.5)
    qd = qd * (DIFF_QK ** -0.5)
    rm = [_pad_heads(qa), qb, qd, u, v,
          jnp.concatenate([ga, jnp.zeros((w.shape[0], 128 - ga.shape[1]), F32)], 1), kc, vc]
    tr = [kc, vc, ks, vs, kw, vw, kb, vb, kd, vd]
    return (jnp.concatenate(rm, 1).astype(BF16), jnp.concatenate(tr, 1).T.astype(BF16))


def _rope_tables(pos):
    pos = pos.astype(F32)[:, None]
    n = pos.shape[0]

    def table(half, plain, reps):
        freq = ROPE_THETA ** (-jnp.arange(half, dtype=F32) / half)
        ang = pos * freq[None, :]
        c = jnp.concatenate([jnp.cos(ang), jnp.cos(ang), jnp.ones((n, plain), F32)], 1)
        s = jnp.concatenate([jnp.sin(ang), jnp.sin(ang), jnp.zeros((n, plain), F32)], 1)
        return jnp.tile(c, (1, reps)), jnp.tile(s, (1, reps))

    ca, sa = table(ROT_HALF, 128 - 2 * ROT_HALF, 1)
    cd, sd = table(DIFF_ROT_HALF, DIFF_QK - 2 * DIFF_ROT_HALF, 4)
    return (ca, sa, cd, sd, ca.T, sa.T, cd.T, sd.T)


def _compress_weights(pe, w1, w2):
    z = jnp.zeros((CMP_BLOCK, HEAD_DIM, HEAD_DIM), F32)
    w1k = w1[0].reshape(CMP_BLOCK, HEAD_DIM, HEAD_DIM)
    w1v = w1[1].reshape(CMP_BLOCK, HEAD_DIM, HEAD_DIM)
    w1c = jnp.concatenate([jnp.concatenate([w1k, z], 2), jnp.concatenate([z, w1v], 2)], 1)
    z2 = jnp.zeros((HEAD_DIM, HEAD_DIM), F32)
    w2c = jnp.concatenate([jnp.concatenate([w2[0], z2], 1), jnp.concatenate([z2, w2[1]], 1)], 0)
    pe_rows = jnp.concatenate([pe[0], pe[1]], 1)
    return pe_rows, w1c.astype(BF16), w2c.astype(BF16)


def _feature_major(x, tail_shape):
    lead = x.shape[:-2]
    n = x.shape[-1]
    y = x.reshape(lead + tail_shape + (n,))
    nd = y.ndim
    return jnp.transpose(y, tuple(range(len(lead))) + (nd - 1,) + tuple(range(len(lead), nd - 1)))


def kernel(x_prompt, x_sample, cache_nsa_cmp, cache_nsa_sel, cache_nsa_win, cache_sb, cache_diff,
           page_table, w_in, gate_b, cmp_pe, cmp_w1, cmp_w2, diff_lam, diff_norm_g, gmlp_ln_g,
           gmlp_ln_b, gmlp_ws, gmlp_bs, w_out, ln1_g, ln1_b, w_gate, w_up, w_down, ln2_g, ln2_b):
    B, S, _ = x_prompt.shape
    nb, T, _ = x_sample.shape
    depth = w_in.shape[0]
    n_pool = cache_sb.shape[1]
    n_pages = page_table.shape[1]
    past = n_pages * cache_sb.shape[2]
    w_past = cache_nsa_win.shape[2]
    assert T == 1 and cache_sb.shape[2] == 128 and S % Q_TILE == 0 and past % SEL_BLOCK == 0
    alpha = (2 * depth) ** 0.25

    to_pages = lambda c: jnp.transpose(c, (0, 1, 3, 4, 5, 2))
    cmp_cache = to_pages(cache_nsa_cmp).reshape(depth, n_pool, 128, 128)
    sel_cache = to_pages(cache_nsa_sel).reshape(depth, n_pool, 128, 128)
    win_cache = to_pages(cache_nsa_win).reshape(depth, nb, 128, w_past)
    sb_cache = to_pages(cache_sb).reshape(depth, n_pool, 2, 256, 128)
    diff_cache = to_pages(cache_diff).reshape(depth, n_pool, 2, 256, 128)

    tabs_p = _rope_tables(jnp.arange(S, dtype=jnp.int32))
    tabs_s = _rope_tables(jnp.full((nb,), past, jnp.int32))
    row = lambda a: a.reshape(1, -1)

    xp, xs = x_prompt, x_sample.reshape(nb, D_MODEL)
    outs = [[] for _ in range(11)]
    for l in range(depth):
        lam_init = jnp.full((1,), 0.8 - 0.6 * math.exp(-0.3 * l), F32)
        wrm, wt = _layer_weights(w_in[l])
        gb = jnp.concatenate([gate_b[l], jnp.zeros((128 - gate_b.shape[1],), F32)]).reshape(1, 128)
        lng, lnb = row(gmlp_ln_g[l]), row(gmlp_ln_b[l])
        bsm = jnp.repeat(gmlp_bs[l].T, HEAD_DIM, axis=1)
        w00 = row(jnp.repeat(gmlp_ws[l][:, 0, 0], HEAD_DIM))
        b0 = row(jnp.repeat(gmlp_bs[l][:, 0], HEAD_DIM))
        pe_rows, w1c, w2c = _compress_weights(cmp_pe[l], cmp_w1[l], cmp_w2[l])
        ng = row(jnp.tile(diff_norm_g[l], N_HEADS))
        tail_w = (w_out[l].astype(BF16), row(ln1_g[l]), row(ln1_b[l]), w_gate[l].astype(BF16),
                  w_up[l].astype(BF16), w_down[l].astype(BF16), row(ln2_g[l]), row(ln2_b[l]))

        (qaf, qb, qd, gates, od, cmp_rm, cmp_t, sel_t, win_t, sb_t, diff_t,
         selbf, winbf, sbbf, diffbf) = _proj_prompt(xp, wrm, wt, tabs_p, gb, lng, lnb, gmlp_ws[l], bsm)
        kvc = _compress_prompt(cmp_rm, pe_rows, w1c, w2c)
        oa = _nsa_prompt(qaf, gates, kvc, selbf, winbf)
        ob = _sb_prompt(qb, sbbf)
        oc = _diff_prompt(lam_init, diff_lam[l], ng, qd, diffbf)
        flat = lambda a: a.reshape(B * S, a.shape[-1])
        xp = _tail(flat(xp), flat(oa), flat(ob), flat(oc), flat(od), *tail_w, alpha).reshape(B, S, D_MODEL)
        outs[0].append(_feature_major(cmp_t, (2, 1, HEAD_DIM)))
        outs[2].append(_feature_major(sel_t, (2, 1, HEAD_DIM)))
        outs[4].append(_feature_major(win_t[:, :, S - min(WINDOW, S):], (2, 1, HEAD_DIM)))
        outs[6].append(_feature_major(sb_t, (2, N_HEADS, HEAD_DIM)))
        outs[8].append(_feature_major(diff_t, (2, N_HEADS, HEAD_DIM)))

        (qaf, qb, qd, gates, od, vn, sel_rm, win_rm, diff_rm,
         cmp_t, sel_t, win_t, sb_t, diff_t) = _proj_sample(xs, wrm, wt, tabs_s, gb, lng, lnb, w00, b0)
        cmp_pool = _compress_pool(l, cmp_cache, pe_rows, w1c, w2c).reshape(n_pool, 4, 128)[:, jnp.array([0, 2, 1, 3])]
        qh = jnp.pad(qaf.reshape(nb, N_HEADS, 128), ((0, 0), (0, 8 - N_HEADS), (0, 0)))
        gs = jnp.pad(gates[:, :3 * N_HEADS].reshape(nb, N_HEADS, 3), ((0, 0), (0, 8 - N_HEADS), (0, 125)))
        oa, ob, oc = _sample_mix(
            l, page_table, qh, gs, sel_rm.reshape(nb, 1, 128), win_rm.reshape(nb, 1, 128), cmp_pool, sel_cache,
            win_cache, qb.reshape(nb, 1, 256), sb_cache, lam_init, diff_lam[l], ng, qd.reshape(nb, 1, 256),
            diff_rm.reshape(nb, 1, 512), diff_cache)
        oa = oa[:, :N_HEADS, HEAD_DIM:].reshape(nb, GROUP_WIDTH).astype(BF16)
        ob, oc = ob.reshape(nb, 256), oc.reshape(nb, 256)
        xs = _tail(xs, oa, ob, oc, od, *tail_w, alpha)
        new_row = lambda t, shape: _feature_major(t, shape)[:, None]
        outs[1].append(new_row(cmp_t, (2, 1, HEAD_DIM)))
        outs[3].append(new_row(sel_t, (2, 1, HEAD_DIM)))
        outs[5].append(jnp.concatenate([cache_nsa_win[l][:, T:], new_row(win_t, (2, 1, HEAD_DIM))], axis=1))
        outs[7].append(new_row(sb_t, (2, N_HEADS, HEAD_DIM)))
        outs[9].append(new_row(diff_t, (2, N_HEADS, HEAD_DIM)))
        outs[10].append(vn.reshape(nb, 1, GROUP_WIDTH))

    stacked = [jnp.stack(o) for o in outs]
    return (xp, xs.reshape(nb, 1, D_MODEL), *stacked)
```

```python
import functools
import math

import jax
import jax.numpy as jnp
from jax import lax
from jax.experimental import pallas as pl
from jax.experimental.pallas import tpu as pltpu

D_MODEL = 1024
HEAD_DIM = 64
N_HEADS = 4
GROUP_WIDTH = N_HEADS * HEAD_DIM
CMP_BLOCK = 32
SEL_BLOCK = 64
SEL_TOPK = 16
WINDOW = 512
FORCE_SCORE = 1.0e4
DIFF_QK = HEAD_DIM // 2
CHUNK = 128
ROPE_THETA = 500000.0
ROT_HALF = HEAD_DIM // 8
DIFF_ROT_HALF = DIFF_QK // 8
LN_EPS = 1e-5
NEG_INF = -1e30
TINY = 1e-30
NEG_BIG = -3.0e38
SB_UNDERFLOW = -110.0
EXP_HEADROOM = 50.0

LANES = 128
VMEM_LIMIT_BYTES = 56 * 1024 * 1024

TOKEN_TILE = 256
Q_TILE = 256
BIG_Q_TILE = 512
KV_TILE = 256

F32 = jnp.float32
BF16 = jnp.bfloat16

RM = dict(qa=0, qb=512, qd=768, u=1024, v=1280, gate=1536, cmp=1664)
TR = dict(cmp=0, sel=128, win=256, sb=384, diff=896)


def _dot(a, b):
    return jnp.dot(a, b, preferred_element_type=F32)


def _dot_nt(a, b):
    return lax.dot_general(a, b, (((1,), (1,)), ((), ())), preferred_element_type=F32)


def _iota(shape, dim):
    return lax.broadcasted_iota(jnp.int32, shape, dim)


def _gelu(x):
    return 0.5 * x * (1.0 + jnp.tanh(0.7978845608028654 * (x + 0.044715 * (x * x * x))))


def _sigmoid(x):
    return 1.0 / (1.0 + jnp.exp(-x))


def _layer_norm(x, g, b):
    mu = jnp.mean(x, axis=-1, keepdims=True)
    xc = x - mu
    var = jnp.mean(xc * xc, axis=-1, keepdims=True)
    return xc * lax.rsqrt(var + LN_EPS) * g + b


def _softplus(z):
    return jnp.maximum(z, 0.0) + jnp.log(1.0 + jnp.exp(-jnp.abs(z)))


def _split_bf16(x):
    hi = x.astype(BF16)
    lo = (x - hi.astype(F32)).astype(BF16)
    return hi, lo


def _suffix_matrix(n):
    return (_iota((n, n), 0) >= _iota((n, n), 1)).astype(BF16)


def _rope(h, cos, sin, half, axis):
    n = h.shape[axis]
    above = pltpu.roll(h, n - half, axis)
    below = pltpu.roll(h, half, axis)
    first = (_iota(h.shape, axis) & (2 * half - 1)) < half
    return h * cos + jnp.where(first, -above, below) * sin


def _params(sem):
    return pltpu.CompilerParams(dimension_semantics=sem, vmem_limit_bytes=VMEM_LIMIT_BYTES)


def _const_spec(shape):
    zeros = (0,) * len(shape)
    return pl.BlockSpec(shape, lambda *_: zeros, pipeline_mode=pl.Buffered(1))


def _proj_common(xb, wrm_ref, cq_ref, sq_ref, cd_ref, sd_ref, gb_ref, lng_ref, lnb_ref,
                 qaf_ref, qb_ref, qd_ref, g_ref):
    def rm(name, k=0, width=128):
        a = RM[name] + k * 128
        return _dot(xb, wrm_ref[:, a:a + width])

    cq, sq = cq_ref[...], sq_ref[...]
    for h in range(N_HEADS):
        qaf_ref[:, h * 128:(h + 1) * 128] = _rope(rm("qa", h), cq, sq, ROT_HALF, 1).astype(BF16)
    qb_ref[...] = rm("qb", 0, 256).astype(BF16)
    cd, sd = cd_ref[...], sd_ref[...]
    for k in range(2):
        qd_ref[:, k * 128:(k + 1) * 128] = _rope(rm("qd", k), cd, sd, DIFF_ROT_HALF, 1).astype(BF16)
    g_ref[...] = _sigmoid(rm("gate") + gb_ref[...])
    u = _gelu(rm("u", 0, 256))
    vn = _layer_norm(_gelu(rm("v", 0, 256)), lng_ref[...], lnb_ref[...])
    return rm, u, vn


def _proj_feature_major(xb, wt_ref, cta_ref, sta_ref, ctd_ref, std_ref):
    def tr(name, k=0, height=128):
        a = TR[name] + k * 128
        return _dot_nt(wt_ref[a:a + height, :], xb)

    cta, sta = cta_ref[...], sta_ref[...]
    ctd, std = ctd_ref[...], std_ref[...]
    cmp_t = tr("cmp")
    sel_t = _rope(tr("sel"), cta, sta, ROT_HALF, 0)
    win_t = _rope(tr("win"), cta, sta, ROT_HALF, 0)
    sb_t = tr("sb", 0, 512)
    kd_t = [_rope(tr("diff", k), ctd, std, DIFF_ROT_HALF, 0) for k in range(2)]
    vd_t = tr("diff", 2, 256)
    return cmp_t, sel_t, win_t, sb_t, kd_t, vd_t


def _proj_prompt_kernel(x_ref, wrm_ref, wt_ref, cq_ref, sq_ref, cd_ref, sd_ref,
                        cta_ref, sta_ref, ctd_ref, std_ref, gb_ref, lng_ref, lnb_ref, ws_ref, bsm_ref,
                        qaf_ref, qb_ref, qd_ref, g_ref, od_ref, cmprm_ref,
                        cmpt_ref, selt_ref, wint_ref, sbt_ref, difft_ref,
                        selbf_ref, winbf_ref, sbbf_ref, diffbf_ref, *, tm, tk):
    xb = x_ref[...].astype(BF16)
    rm, u, vn = _proj_common(xb, wrm_ref, cq_ref, sq_ref, cd_ref, sd_ref, gb_ref, lng_ref, lnb_ref,
                             qaf_ref, qb_ref, qd_ref, g_ref)
    cmprm_ref[...] = rm("cmp")

    tril = _iota((CHUNK, CHUNK), 1) <= _iota((CHUNK, CHUNK), 0)
    lane_group = _iota((CHUNK, GROUP_WIDTH), 1) >> 6
    wsg = [jnp.where(tril, ws_ref[g], 0.0).astype(BF16) for g in range(N_HEADS)]
    bsm = bsm_ref[...]
    for c in range(tm // CHUNK):
        rows = slice(c * CHUNK, (c + 1) * CHUNK)
        vnc = vn[rows, :]
        mixed = bsm
        for g in range(N_HEADS):
            mixed = mixed + _dot(wsg[g], jnp.where(lane_group == g, vnc, 0.0).astype(BF16))
        od_ref[rows, :] = (u[rows, :] * mixed).astype(BF16)

    cmp_t, sel_t, win_t, sb_t, kd_t, vd_t = _proj_feature_major(
        xb, wt_ref, cta_ref, sta_ref, ctd_ref, std_ref)
    cmpt_ref[...] = cmp_t
    selt_ref[...] = sel_t
    wint_ref[...] = win_t
    sbt_ref[...] = sb_t
    difft_ref[0:128, :] = kd_t[0]
    difft_ref[128:256, :] = kd_t[1]
    difft_ref[256:512, :] = vd_t
    for t in range(tm // tk):
        cols = slice(t * tk, (t + 1) * tk)
        selbf_ref[t] = sel_t[:, cols].astype(BF16)
        winbf_ref[t] = win_t[:, cols].astype(BF16)
        sbbf_ref[t] = sb_t[:, cols].astype(BF16)
        diffbf_ref[t, 0:128, :] = kd_t[0][:, cols].astype(BF16)
        diffbf_ref[t, 128:256, :] = kd_t[1][:, cols].astype(BF16)
        diffbf_ref[t, 256:512, :] = vd_t[:, cols].astype(BF16)


def _proj_sample_kernel(x_ref, wrm_ref, wt_ref, cq_ref, sq_ref, cd_ref, sd_ref,
                        cta_ref, sta_ref, ctd_ref, std_ref, gb_ref, lng_ref, lnb_ref, w00_ref, b0_ref,
                        qaf_ref, qb_ref, qd_ref, g_ref, od_ref, vn_ref,
                        selrm_ref, winrm_ref, diffrm_ref,
                        cmpt_ref, selt_ref, wint_ref, sbt_ref, difft_ref):
    xb = x_ref[...].astype(BF16)
    rm, u, vn = _proj_common(xb, wrm_ref, cq_ref, sq_ref, cd_ref, sd_ref, gb_ref, lng_ref, lnb_ref,
                             qaf_ref, qb_ref, qd_ref, g_ref)
    vn_ref[...] = vn
    od_ref[...] = (u * (vn * w00_ref[...] + b0_ref[...])).astype(BF16)

    def rows(name, k=0, width=128):
        a = TR[name] + k * 128
        return _dot_nt(xb, wt_ref[a:a + width, :])

    cq, sq = cq_ref[...], sq_ref[...]
    cd, sd = cd_ref[...], sd_ref[...]
    selrm_ref[...] = _rope(rows("sel"), cq, sq, ROT_HALF, 1)
    winrm_ref[...] = _rope(rows("win"), cq, sq, ROT_HALF, 1)
    for k in range(2):
        diffrm_ref[:, k * 128:(k + 1) * 128] = _rope(rows("diff", k), cd, sd, DIFF_ROT_HALF, 1)
    diffrm_ref[:, 256:512] = rows("diff", 2, 256)

    cmp_t, sel_t, win_t, sb_t, kd_t, vd_t = _proj_feature_major(
        xb, wt_ref, cta_ref, sta_ref, ctd_ref, std_ref)
    cmpt_ref[...] = cmp_t
    selt_ref[...] = sel_t
    wint_ref[...] = win_t
    sbt_ref[...] = sb_t
    difft_ref[0:128, :] = kd_t[0]
    difft_ref[128:256, :] = kd_t[1]
    difft_ref[256:512, :] = vd_t


def _proj_prompt(x, wrm, wt, tabs, gb, lng, lnb, ws, bsm):
    B, S, _ = x.shape
    tm, tk = TOKEN_TILE, KV_TILE
    nk = S // tk
    row = lambda n: pl.BlockSpec((None, tm, n), lambda b, i: (b, i, 0))
    feat = lambda n: pl.BlockSpec((None, n, tm), lambda b, i: (b, 0, i))
    tile = lambda n: pl.BlockSpec((None, tm // tk, n, tk), lambda b, i: (b, i, 0, 0))
    tab_rm = pl.BlockSpec((tm, 128), lambda b, i: (i, 0))
    tab_t = pl.BlockSpec((128, tm), lambda b, i: (0, i))
    in_specs = [row(D_MODEL), _const_spec(wrm.shape), _const_spec(wt.shape),
                tab_rm, tab_rm, tab_rm, tab_rm, tab_t, tab_t, tab_t, tab_t,
                _const_spec(gb.shape), _const_spec(lng.shape), _const_spec(lnb.shape),
                _const_spec(ws.shape), _const_spec(bsm.shape)]
    sd = jax.ShapeDtypeStruct
    out_shape = [sd((B, S, 512), BF16), sd((B, S, 256), BF16), sd((B, S, 256), BF16),
                 sd((B, S, 128), F32), sd((B, S, 256), BF16), sd((B, S, 128), F32),
                 sd((B, 128, S), F32), sd((B, 128, S), F32), sd((B, 128, S), F32),
                 sd((B, 512, S), F32), sd((B, 512, S), F32),
                 sd((B, nk, 128, tk), BF16), sd((B, nk, 128, tk), BF16),
                 sd((B, nk, 512, tk), BF16), sd((B, nk, 512, tk), BF16)]
    out_specs = [row(512), row(256), row(256), row(128), row(256), row(128),
                 feat(128), feat(128), feat(128), feat(512), feat(512),
                 tile(128), tile(128), tile(512), tile(512)]
    return pl.pallas_call(
        functools.partial(_proj_prompt_kernel, tm=tm, tk=tk),
        grid=(B, S // tm), in_specs=in_specs, out_specs=out_specs, out_shape=out_shape,
        compiler_params=_params(("parallel", "parallel")), name="proj_prompt",
    )(x, wrm, wt, *tabs, gb, lng, lnb, ws, bsm)


def _proj_sample(x, wrm, wt, tabs, gb, lng, lnb, w00, b0):
    nb = x.shape[0]
    full = lambda a: pl.BlockSpec(a.shape, lambda i: (0,) * a.ndim)
    ins = (x, wrm, wt, *tabs, gb, lng, lnb, w00, b0)
    sd = jax.ShapeDtypeStruct
    out_shape = [sd((nb, 512), BF16), sd((nb, 256), BF16), sd((nb, 256), BF16), sd((nb, 128), F32),
                 sd((nb, 256), BF16), sd((nb, 256), F32),
                 sd((nb, 128), F32), sd((nb, 128), F32), sd((nb, 512), F32),
                 sd((128, nb), F32), sd((128, nb), F32), sd((128, nb), F32),
                 sd((512, nb), F32), sd((512, nb), F32)]
    return pl.pallas_call(
        _proj_sample_kernel, grid=(1,), in_specs=[full(a) for a in ins],
        out_specs=[pl.BlockSpec(s.shape, lambda i: (0, 0)) for s in out_shape], out_shape=out_shape,
        compiler_params=_params(("arbitrary",)), name="proj_sample",
    )(*ins)


def _compress_rows(rows_ref, first, stride, n_out, pe_ref, w1_ref, w2_ref):
    h = jnp.zeros((n_out, 128), F32)
    for r in range(CMP_BLOCK):
        x = rows_ref[pl.ds(first + r, n_out, stride=stride), :] + pe_ref[r:r + 1, :]
        h = h + _dot(x.astype(BF16), w1_ref[r])
    return _dot(_gelu(h).astype(BF16), w2_ref[...])


def _compress_prompt_kernel(x_ref, pe_ref, w1_ref, w2_ref, o_ref, *, ns):
    for parity in range(2):
        o_ref[parity * ns:(parity + 1) * ns, :] = _compress_rows(
            x_ref, parity * CMP_BLOCK, SEL_BLOCK, ns, pe_ref, w1_ref, w2_ref).astype(BF16)


def _compress_prompt(cmp_rm, pe, w1, w2):
    B, S, _ = cmp_rm.shape
    ns = S // SEL_BLOCK
    return pl.pallas_call(
        functools.partial(_compress_prompt_kernel, ns=ns), grid=(B,),
        in_specs=[pl.BlockSpec((None, S, 128), lambda b: (b, 0, 0)), _const_spec(pe.shape),
                  _const_spec(w1.shape), _const_spec(w2.shape)],
        out_specs=pl.BlockSpec((None, 2 * ns, 128), lambda b: (b, 0, 0)),
        out_shape=jax.ShapeDtypeStruct((B, 2 * ns, 128), BF16),
        compiler_params=_params(("parallel",)), name="compress_prompt",
    )(cmp_rm, pe, w1, w2)


def _compress_pool_kernel(x_ref, pe_ref, w1_ref, w2_ref, o_ref, rows_ref, *, pages):
    for p in range(pages):
        rows_ref[p * 128:(p + 1) * 128, :] = x_ref[p].T
    n_out = pages * (128 // CMP_BLOCK)
    o_ref[...] = _compress_rows(rows_ref, 0, CMP_BLOCK, n_out, pe_ref, w1_ref, w2_ref)


def _compress_pool(layer, cmp_cache, pe, w1, w2):
    n_pool = cmp_cache.shape[1]
    pages = math.gcd(n_pool, 64)
    n_out = pages * (128 // CMP_BLOCK)
    return pl.pallas_call(
        functools.partial(_compress_pool_kernel, pages=pages), grid=(n_pool // pages,),
        in_specs=[pl.BlockSpec((None, pages, 128, 128), lambda g: (layer, g, 0, 0)), _const_spec(pe.shape),
                  _const_spec(w1.shape), _const_spec(w2.shape)],
        out_specs=pl.BlockSpec((n_out, 128), lambda g: (g, 0)),
        out_shape=jax.ShapeDtypeStruct((n_pool * (128 // CMP_BLOCK), 128), F32),
        scratch_shapes=[pltpu.VMEM((pages * 128, 128), F32)],
        compiler_params=_params(("parallel",)), name="compress_pool",
    )(cmp_cache, pe, w1, w2)


def _select_topk(score, blk_f, n_blocks, k):
    work = score
    for _ in range(k):
        mx = jnp.max(work, axis=0, keepdims=True)
        idx = jnp.min(jnp.where(work == mx, blk_f, float(n_blocks)), axis=0, keepdims=True)
        work = jnp.where(blk_f == idx, NEG_BIG, work)
    return jnp.where(work == NEG_BIG, 1.0, 0.0)


def _select_topk_row(score, k):
    n = score.shape[1]
    mine = jnp.broadcast_to(score, (n, n))
    other = mine.T
    beats = (other > mine) | ((other == mine) & (_iota((n, n), 0) < _iota((n, n), 1)))
    rank = jnp.sum(jnp.where(beats, 1.0, 0.0), axis=0, keepdims=True)
    return jnp.where(rank < k, 1.0, 0.0)


def _masked_softmax(logits, mask):
    lm = jnp.where(mask, logits, NEG_INF)
    mx = jnp.max(lm, axis=1, keepdims=True)
    e = jnp.where(mask, jnp.exp(lm - mx), 0.0)
    return e / jnp.maximum(jnp.sum(e, axis=1, keepdims=True), TINY)


def _cmp_block_end(n_half):
    r = _iota((1, 2 * n_half), 1)
    return jnp.where(r < n_half, 64 * r + 31, 64 * (r - n_half) + 63)


def _nsa_prompt_kernel(qf_ref, g_ref, kvc_ref, sel_ref, win_ref, o_ref, qs_ref, acc_ref, m_ref, l_ref, ex_ref,
                       bias_ref, *, tq, sub, ns, topk):
    tk = tq
    i = pl.program_id(1)
    tpos = i * tq + _iota((tq, 1), 0)
    qh = [qf_ref[:, h * 128:(h + 1) * 128] for h in range(N_HEADS)]
    kvc = kvc_ref[...]

    for h in range(N_HEADS):
        qs_ref[h * tq:(h + 1) * tq, :] = qh[h]

    maskc = jnp.broadcast_to((_cmp_block_end(ns) <= tpos)[None], (N_HEADS, tq, 2 * ns)).reshape(N_HEADS * tq, 2 * ns)
    pc = _masked_softmax(_dot_nt(qs_ref[...], kvc), maskc)
    oc_all = _dot(pc.astype(BF16), kvc)
    oc = [oc_all[h * tq:(h + 1) * tq, :] for h in range(N_HEADS)]
    impc = functools.reduce(jnp.add, [pc[h * tq:(h + 1) * tq, :] for h in range(N_HEADS)])
    imp = impc[:, :ns] + impc[:, ns:]

    blk = _iota((ns, tq), 0)
    tpos_row = i * tq + _iota((1, tq), 1)
    cur = tpos_row >> 6
    forced = (blk == 0) | (blk == cur) | (blk == cur - 1)
    valid = (blk << 6) <= tpos_row
    score = jnp.where(forced, FORCE_SCORE, jnp.where(valid, imp.T, -1.0))
    sel_bf = _select_topk(score, blk.astype(F32), ns, topk).T.astype(BF16)

    row, col = _iota((tq, tk), 0), _iota((tq, tk), 1)

    def chosen_in_tile(j):
        expand = (_iota((ns, tk), 0) == j * (tk // SEL_BLOCK) + (_iota((ns, tk), 1) >> 6))
        return _dot(sel_bf, expand.astype(BF16)) > 0.5

    def sel_bias_step(j, carry):
        bias_ref[j] = jnp.where(chosen_in_tile(j), 0.0, NEG_INF).astype(BF16)
        return carry

    lax.fori_loop(0, i, sel_bias_step, 0)
    bias_ref[i] = jnp.where(chosen_in_tile(i) & (col <= row), 0.0, NEG_INF).astype(BF16)

    def attend(kv_ref, j_lo, bias_of):
        def scores(j):
            kt = _key_rows(kv_ref, j, sub, 0, 128)
            s = _dot(qs_ref[...], kt).reshape(N_HEADS, tq, tk) + bias_of(j)[None]
            return s.reshape(N_HEADS * tq, tk), kt

        def max_step(j, carry):
            s, _ = scores(j)
            m_ref[...] = jnp.maximum(m_ref[...], _chunk_reduce(s, jnp.maximum))
            return carry

        def sum_step(j, carry, track=False, scored=None):
            s, kt = scores(j) if scored is None else scored
            m = m_ref[...]
            d = [c - m for c in _lane_chunks(s)]
            if track:
                ex_ref[...] = jnp.maximum(ex_ref[...], functools.reduce(jnp.maximum, d))
            p = [jnp.exp(c) for c in d]
            l_ref[...] += functools.reduce(jnp.add, p)
            acc_ref[...] += _dot_nt(jnp.concatenate(p, axis=1).astype(BF16), kt)
            return carry

        def reset_sums():
            acc_ref[...] = jnp.zeros(acc_ref.shape, F32)
            l_ref[...] = jnp.zeros(l_ref.shape, F32)

        def row_max_to_lanes():
            m_ref[...] = jnp.broadcast_to(jnp.max(m_ref[...], axis=1, keepdims=True), m_ref.shape)

        diag = scores(i)
        m_ref[...] = _chunk_reduce(diag[0], jnp.maximum)
        row_max_to_lanes()
        reset_sums()
        ex_ref[...] = jnp.zeros(ex_ref.shape, F32)
        sum_step(i, 0, scored=diag)
        lax.fori_loop(j_lo, i, functools.partial(sum_step, track=True), 0)

        @pl.when(jnp.max(ex_ref[...]) > EXP_HEADROOM)
        def _():
            m_ref[...] = jnp.full(m_ref.shape, NEG_INF, F32)
            lax.fori_loop(j_lo, i + 1, max_step, 0)
            row_max_to_lanes()
            reset_sums()
            lax.fori_loop(j_lo, i + 1, sum_step, 0)

        out = acc_ref[...] / jnp.maximum(jnp.sum(l_ref[...], axis=1, keepdims=True), TINY)
        return [out[h * tq:(h + 1) * tq, :] for h in range(N_HEADS)]

    def win_bias(j):
        d = (i - j) * tk + row - col
        return jnp.where((d >= 0) & (d < WINDOW), 0.0, NEG_INF)

    osel = attend(sel_ref, 0, lambda j: bias_ref[j].astype(F32))
    ow = attend(win_ref, jnp.maximum(i - WINDOW // tk, 0), win_bias)

    g = g_ref[...]
    lane = _iota((tq, 128), 1)
    mixed = [g[:, 3 * h:3 * h + 1] * oc[h] + g[:, 3 * h + 1:3 * h + 2] * osel[h]
             + g[:, 3 * h + 2:3 * h + 3] * ow[h] for h in range(N_HEADS)]
    for k in range(2):
        even = pltpu.roll(mixed[2 * k], 64, 1)
        o_ref[:, k * 128:(k + 1) * 128] = jnp.where(lane < 64, even, mixed[2 * k + 1]).astype(BF16)


def _nsa_prompt(qaf, gates, kvc, selbf, winbf):
    B, S, _ = qaf.shape
    tq = math.gcd(S, BIG_Q_TILE)
    ns = S // SEL_BLOCK
    kern = functools.partial(_nsa_prompt_kernel, tq=tq, sub=tq // KV_TILE, ns=ns, topk=min(SEL_TOPK, ns))
    row = lambda n: pl.BlockSpec((None, tq, n), lambda b, i: (b, i, 0))
    whole = lambda a: pl.BlockSpec((None,) + a.shape[1:], lambda b, i: (b,) + (0,) * (a.ndim - 1))
    return pl.pallas_call(
        kern, grid=(B, S // tq),
        in_specs=[row(512), row(128), whole(kvc), whole(selbf), whole(winbf)],
        out_specs=row(256), out_shape=jax.ShapeDtypeStruct((B, S, 256), BF16),
        scratch_shapes=[pltpu.VMEM((N_HEADS * tq, 128), BF16), pltpu.VMEM((N_HEADS * tq, 128), F32),
                        pltpu.VMEM((N_HEADS * tq, 128), F32), pltpu.VMEM((N_HEADS * tq, 128), F32),
                        pltpu.VMEM((N_HEADS * tq, 128), F32), pltpu.VMEM((S // tq, tq, tq), BF16)],
        compiler_params=_params(("parallel", "arbitrary")), name="nsa_prompt",
    )(qaf, gates, kvc, selbf, winbf)


def _sb_prompt_kernel(q_ref, kv_ref, o_ref, qs_ref, acc_ref, car_ref, *, tq, tk):
    assert tq == tk
    i = pl.program_id(1)
    rows = N_HEADS * tq
    q = q_ref[...].astype(F32)
    lane_head = _iota((1, GROUP_WIDTH), 1) >> 6
    for h in range(N_HEADS):
        qs_ref[h * tq:(h + 1) * tq, :] = jnp.where(lane_head == h, q, 0.0).astype(BF16)
    suffix = _suffix_matrix(tk)
    acc_ref[...] = jnp.zeros(acc_ref.shape, F32)
    car_ref[...] = jnp.zeros(car_ref.shape, F32)

    def step(j, strict):
        z = _dot(qs_ref[...], kv_ref[j, 0:256, :])
        log_keep = -_softplus(z)
        if strict is not None:
            keep = jnp.broadcast_to(strict[None], (N_HEADS, tq, tk)).reshape(rows, tk)
            log_keep = jnp.where(keep, log_keep, 0.0)
            z = jnp.where(keep, z, NEG_INF)
        hi, lo = _split_bf16(log_keep)
        incl = _dot(hi, suffix) + _dot(lo, suffix)
        car = car_ref[...]
        ab = jnp.exp(z + incl + jnp.concatenate([car, car], axis=1)).astype(BF16)
        for k in range(2):
            pair = slice(2 * k * tq, (2 * k + 2) * tq)
            acc_ref[pair, :] += _dot_nt(ab[pair, :], kv_ref[j, 256 + k * 128:384 + k * 128, :])
        car_ref[...] = car + jnp.broadcast_to(incl[:, 0:1], (rows, 128))

    def any_weight_left():
        return jnp.max(car_ref[...]) > SB_UNDERFLOW

    step(i, _iota((tq, tk), 1) < _iota((tq, tk), 0))

    def left_tile(state):
        j, _ = state
        step(j, None)
        return j - 1, any_weight_left()

    lax.while_loop(lambda state: (state[0] >= 0) & state[1], left_tile, (i - 1, any_weight_left()))
    lane = _iota((tq, 128), 1)
    for k in range(2):
        even, odd = acc_ref[2 * k * tq:(2 * k + 1) * tq, :], acc_ref[(2 * k + 1) * tq:(2 * k + 2) * tq, :]
        o_ref[:, k * 128:(k + 1) * 128] = jnp.where(lane < 64, even, odd).astype(BF16)


def _sb_prompt(qb, sbbf):
    B, S, _ = qb.shape
    tq, tk = Q_TILE, KV_TILE
    row = lambda n: pl.BlockSpec((None, tq, n), lambda b, i: (b, i, 0))
    return pl.pallas_call(
        functools.partial(_sb_prompt_kernel, tq=tq, tk=tk), grid=(B, S // tq),
        in_specs=[row(256), pl.BlockSpec((None,) + sbbf.shape[1:], lambda b, i: (b, 0, 0, 0))],
        out_specs=row(256), out_shape=jax.ShapeDtypeStruct((B, S, 256), BF16),
        scratch_shapes=[pltpu.VMEM((N_HEADS * tq, GROUP_WIDTH), BF16), pltpu.VMEM((N_HEADS * tq, 128), F32),
                        pltpu.VMEM((N_HEADS * tq, 128), F32)],
        compiler_params=_params(("parallel", "arbitrary")), name="sb_prompt",
    )(qb, sbbf)


def _diff_lambda(dl_ref, lam_init):
    lq = dl_ref[...]
    s1 = jnp.sum(lq[0:1, :] * lq[1:2, :], axis=1, keepdims=True)
    s2 = jnp.sum(lq[2:3, :] * lq[3:4, :], axis=1, keepdims=True)
    return jnp.exp(s1) - jnp.exp(s2) + lam_init


def _head_rms_norm(out, lane_head, ng, lam_init):
    sq = out * out
    rs = jnp.zeros(out.shape, F32)
    for h in range(N_HEADS):
        ms = jnp.sum(jnp.where(lane_head == h, sq, 0.0), axis=1, keepdims=True) * (1.0 / HEAD_DIM)
        rs = jnp.where(lane_head == h, lax.rsqrt(ms + LN_EPS), rs)
    return out * rs * ng * (1.0 - lam_init)


def _lane_chunks(x):
    return [x[:, c * LANES:(c + 1) * LANES] for c in range(x.shape[1] // LANES)]


def _chunk_reduce(x, op):
    chunks = _lane_chunks(x)
    out = chunks[0]
    for c in chunks[1:]:
        out = op(out, c)
    return out


def _key_rows(kv_ref, j, sub, r0, r1):
    return jnp.concatenate([kv_ref[j * sub + u, r0:r1, :] for u in range(sub)], axis=1)


def _diff_prompt_kernel(li_ref, dl_ref, ng_ref, q_ref, kv_ref, o_ref, qs_ref, acc_ref, m_ref, l_ref, ex_ref,
                        *, tq, sub):
    tk = tq
    i = pl.program_id(1)
    lam_init = li_ref[0]
    lam = _diff_lambda(dl_ref, lam_init)
    q = q_ref[...].astype(F32)
    lane = _iota((1, GROUP_WIDTH), 1)
    lane_head, lane_half = lane >> 6, (lane >> 5) & 1
    n_soft = 2 * N_HEADS
    rows = n_soft * tq
    for a in range(n_soft):
        qa = jnp.where((lane_head == a // 2) & (lane_half == a % 2), q, 0.0)
        qs_ref[a * tq:(a + 1) * tq, :] = qa.astype(BF16)
    causal_bias = jnp.where(_iota((tq, tk), 1) <= _iota((tq, tk), 0), 0.0, NEG_INF)

    def scores(j, bias):
        s = _dot(qs_ref[...], _key_rows(kv_ref, j, sub, 0, 256))
        if bias is not None:
            s = (s.reshape(n_soft, tq, tk) + bias[None]).reshape(rows, tk)
        return s

    def max_step(j, bias):
        m_ref[...] = jnp.maximum(m_ref[...], _chunk_reduce(scores(j, bias), jnp.maximum))

    def sum_step(j, bias, s=None, track=False):
        s = scores(j, bias) if s is None else s
        m = m_ref[...]
        d = [c - m for c in _lane_chunks(s)]
        if track:
            ex_ref[...] = jnp.maximum(ex_ref[...], functools.reduce(jnp.maximum, d))
        p = [jnp.exp(c) for c in d]
        l_ref[...] += functools.reduce(jnp.add, p)
        pb = jnp.concatenate(p, axis=1).astype(BF16)
        for k in range(2):
            part = slice(k * (rows // 2), (k + 1) * (rows // 2))
            acc_ref[part, :] += _dot_nt(pb[part, :], _key_rows(kv_ref, j, sub, 256 + k * 128, 384 + k * 128))

    def below(step):
        def body(j, carry):
            step(j, None)
            return carry
        lax.fori_loop(0, i, body, 0)

    def reset_sums():
        acc_ref[...] = jnp.zeros(acc_ref.shape, F32)
        l_ref[...] = jnp.zeros(l_ref.shape, F32)

    def row_max_to_lanes():
        m_ref[...] = jnp.broadcast_to(jnp.max(m_ref[...], axis=1, keepdims=True), m_ref.shape)

    s_diag = scores(i, causal_bias)
    m_ref[...] = _chunk_reduce(s_diag, jnp.maximum)
    row_max_to_lanes()
    reset_sums()
    ex_ref[...] = jnp.zeros(ex_ref.shape, F32)
    sum_step(i, None, s=s_diag)
    below(functools.partial(sum_step, track=True))

    @pl.when(jnp.max(ex_ref[...]) > EXP_HEADROOM)
    def _():
        m_ref[...] = jnp.full(m_ref.shape, NEG_INF, F32)
        below(max_step)
        max_step(i, causal_bias)
        row_max_to_lanes()
        reset_sums()
        below(sum_step)
        sum_step(i, causal_bias)

    lane128 = _iota((tq, 128), 1)
    ng = ng_ref[...]
    blk = lambda ref, a: ref[a * tq:(a + 1) * tq, :]
    for k in range(2):
        o = []
        for h in (2 * k, 2 * k + 1):
            l1 = jnp.maximum(jnp.sum(blk(l_ref, 2 * h), axis=1, keepdims=True), TINY)
            l2 = jnp.maximum(jnp.sum(blk(l_ref, 2 * h + 1), axis=1, keepdims=True), TINY)
            o.append(blk(acc_ref, 2 * h) / l1 - lam * (blk(acc_ref, 2 * h + 1) / l2))
        low = lane128 < HEAD_DIM
        out = jnp.where(low, o[0], o[1])
        sq = out * out
        ms_lo = jnp.sum(jnp.where(low, sq, 0.0), axis=1, keepdims=True)
        ms_hi = jnp.sum(jnp.where(low, 0.0, sq), axis=1, keepdims=True)
        rs = lax.rsqrt(jnp.where(low, ms_lo, ms_hi) * (1.0 / HEAD_DIM) + LN_EPS)
        o_ref[:, k * 128:(k + 1) * 128] = (out * rs * ng[:, k * 128:(k + 1) * 128] * (1.0 - lam_init)).astype(BF16)


def _diff_prompt(lam_init, dl, ng, qd, diffbf):
    B, S, _ = qd.shape
    tq = math.gcd(S, BIG_Q_TILE)
    row = lambda n: pl.BlockSpec((None, tq, n), lambda b, i: (b, i, 0))
    return pl.pallas_call(
        functools.partial(_diff_prompt_kernel, tq=tq, sub=tq // KV_TILE), grid=(B, S // tq),
        in_specs=[pl.BlockSpec(memory_space=pltpu.SMEM), _const_spec(dl.shape), _const_spec(ng.shape),
                  row(256), pl.BlockSpec((None,) + diffbf.shape[1:], lambda b, i: (b, 0, 0, 0))],
        out_specs=row(256), out_shape=jax.ShapeDtypeStruct((B, S, 256), BF16),
        scratch_shapes=[pltpu.VMEM((2 * N_HEADS * tq, GROUP_WIDTH), BF16), pltpu.VMEM((2 * N_HEADS * tq, 128), F32),
                        pltpu.VMEM((2 * N_HEADS * tq, 128), F32), pltpu.VMEM((2 * N_HEADS * tq, 128), F32),
                        pltpu.VMEM((2 * N_HEADS * tq, 128), F32)],
        compiler_params=_params(("parallel", "arbitrary")), name="diff_prompt",
    )(lam_init, dl, ng, qd, diffbf)


def _round_bf16(x):
    return x.astype(BF16).astype(F32)


def _softmax_with_self(s, ok, s_self):
    sm = jnp.where(ok, s, NEG_INF)
    mx = jnp.maximum(jnp.max(sm, axis=1, keepdims=True), s_self)
    e = jnp.where(ok, jnp.exp(sm - mx), 0.0)
    e_self = jnp.exp(s_self - mx)
    inv = 1.0 / (jnp.sum(e, axis=1, keepdims=True) + e_self)
    return e * inv, e_self * inv


def _nsa_sample_kernel(pt_ref, qh_ref, gs_ref, snew_ref, wnew_ref, win_ref, *rest,
                       n_pages, past, half, topk):
    del pt_ref
    cmp_refs, sel_refs = rest[:n_pages], rest[n_pages:2 * n_pages]
    o_ref, kvc_ref = rest[2 * n_pages], rest[2 * n_pages + 1]
    qh = qh_ref[...]
    qf = qh.astype(F32)

    kvc_ref[...] = jnp.zeros(kvc_ref.shape, F32)
    for p in range(n_pages):
        piece = cmp_refs[p][...]
        kvc_ref[2 * p:2 * p + 2, :] = piece[0:2, :]
        kvc_ref[half + 2 * p:half + 2 * p + 2, :] = piece[2:4, :]
    kvc = kvc_ref[...].astype(BF16)
    maskc = _cmp_block_end(half) <= past
    pc = _masked_softmax(_dot_nt(qh, kvc), maskc)
    oc = _dot(pc.astype(BF16), kvc)
    head_row = _iota((8, 1), 0) < N_HEADS
    impc = jnp.sum(jnp.where(head_row, pc, 0.0), axis=0, keepdims=True)
    imp = impc[:, :half] + impc[:, half:]

    n_sel = past // SEL_BLOCK + 1
    blk = _iota((1, half), 1)
    cur = past // SEL_BLOCK
    forced = (blk == 0) | (blk == cur) | (blk == cur - 1)
    valid = blk * SEL_BLOCK <= past
    score = jnp.where(blk < n_sel, jnp.where(forced, FORCE_SCORE, jnp.where(valid, imp, -1.0)), NEG_BIG)
    sel8 = jnp.broadcast_to(_select_topk_row(score, topk), (8, half)).astype(BF16)
    expand = _iota((half, past), 0) == (_iota((half, past), 1) >> 6)
    chosen = _dot(sel8, expand.astype(BF16)) > 0.5

    def branch(kv, new_row, ok):
        new_b = _round_bf16(new_row)
        s_self = jnp.sum(qf * new_b, axis=1, keepdims=True)
        p, p_self = _softmax_with_self(_dot(qh, kv), ok, s_self)
        return p_self * new_b + _dot_nt(p.astype(BF16), kv)

    osel = branch(jnp.concatenate([r[...].astype(BF16) for r in sel_refs], axis=1), snew_ref[...], chosen)
    w_past = win_ref.shape[1]
    in_window = _iota((1, w_past), 1) >= (w_past - WINDOW + 1)
    ow = branch(win_ref[...].astype(BF16), wnew_ref[...], in_window)
    g = gs_ref[...]
    o_ref[...] = g[:, 0:1] * oc + g[:, 1:2] * osel + g[:, 2:3] * ow


def _sb_sample_kernel(pt_ref, q_ref, *rest, n_pages, past, cb):
    del pt_ref
    pages, o_ref = rest[:n_pages], rest[n_pages]
    q8 = jnp.broadcast_to(q_ref[...].astype(F32), (8, GROUP_WIDTH))
    row_head = _iota((8, GROUP_WIDTH), 0) == (_iota((8, GROUP_WIDTH), 1) >> 6)
    qm = jnp.where(row_head, q8, 0.0).astype(BF16)
    kt = jnp.concatenate([pages[p][0].astype(BF16) for p in range(n_pages)], axis=1)
    vt = jnp.concatenate([pages[p][1].astype(BF16) for p in range(n_pages)], axis=1)
    z = _dot(qm, kt)
    log_keep = -_softplus(z)
    n_blk = past // cb
    stack = lambda x: jnp.concatenate([x[:, b * cb:(b + 1) * cb] for b in range(n_blk)], axis=0)
    hi, lo = _split_bf16(stack(log_keep))
    suffix = _suffix_matrix(cb)
    incl = _dot(hi, suffix) + _dot(lo, suffix)
    carry = jnp.zeros((8, 1), F32)
    carries = [None] * n_blk
    for b in reversed(range(n_blk)):
        carries[b] = carry
        carry = carry + incl[b * 8:(b + 1) * 8, 0:1]
    a = jnp.exp(stack(z) + incl + jnp.concatenate(carries, axis=0))
    a = jnp.concatenate([a[b * 8:(b + 1) * 8, :] for b in range(n_blk)], axis=1)
    out = _dot_nt(a.astype(BF16), vt)
    o_ref[...] = jnp.sum(jnp.where(row_head, out, 0.0), axis=0, keepdims=True).astype(BF16)


def _diff_sample_kernel(pt_ref, li_ref, dl_ref, ng_ref, q_ref, dnew_ref, *rest, n_pages):
    del pt_ref
    pages, o_ref = rest[:n_pages], rest[n_pages]
    lam_init = li_ref[0]
    lam = _diff_lambda(dl_ref, lam_init)
    q8 = jnp.broadcast_to(q_ref[...].astype(F32), (8, GROUP_WIDTH))
    lane = _iota((8, GROUP_WIDTH), 1)
    row_head = _iota((8, GROUP_WIDTH), 0) == (lane >> 6)
    lane_half = (lane >> 5) & 1
    new_k = _round_bf16(dnew_ref[:, 0:256])
    new_v = _round_bf16(dnew_ref[:, 256:512])
    kt = jnp.concatenate([pages[p][0].astype(BF16) for p in range(n_pages)], axis=1)
    vt = jnp.concatenate([pages[p][1].astype(BF16) for p in range(n_pages)], axis=1)
    ps, ps_self = [], []
    for m in range(2):
        qm = jnp.where(row_head & (lane_half == m), q8, 0.0)
        s = _dot(qm.astype(BF16), kt)
        s_self = jnp.sum(qm * new_k, axis=1, keepdims=True)
        p, p_self = _softmax_with_self(s, jnp.full(s.shape, True), s_self)
        ps.append(p)
        ps_self.append(p_self)
    out = (ps_self[0] - lam * ps_self[1]) * new_v + _dot_nt((ps[0] - lam * ps[1]).astype(BF16), vt)
    o4 = jnp.sum(jnp.where(row_head, out, 0.0), axis=0, keepdims=True)
    o_ref[...] = _head_rms_norm(o4, _iota((1, GROUP_WIDTH), 1) >> 6, ng_ref[...], lam_init).astype(BF16)


def _page_spec(block, layer, p):
    nz = (0,) * (len(block) - 2)
    return pl.BlockSpec(block, lambda b, pt: (layer, pt[b, p]) + nz)


def _sample_mix_kernel(pt_ref, qh_ref, gs_ref, snew_ref, wnew_ref, win_ref, qb_ref, li_ref, dl_ref, ng_ref,
                       qd_ref, dnew_ref, *rest, n_pages, past, half, topk, cb):
    n = n_pages
    cmp_refs, sel_refs, sb_refs, diff_refs = rest[:n], rest[n:2 * n], rest[2 * n:3 * n], rest[3 * n:4 * n]
    oa_ref, ob_ref, oc_ref, kvc_ref = rest[4 * n:4 * n + 4]
    _nsa_sample_kernel(pt_ref, qh_ref, gs_ref, snew_ref, wnew_ref, win_ref, *cmp_refs, *sel_refs, oa_ref, kvc_ref,
                       n_pages=n, past=past, half=half, topk=topk)
    _sb_sample_kernel(pt_ref, qb_ref, *sb_refs, ob_ref, n_pages=n, past=past, cb=cb)
    _diff_sample_kernel(pt_ref, li_ref, dl_ref, ng_ref, qd_ref, dnew_ref, *diff_refs, oc_ref, n_pages=n)


def _sample_mix(layer, page_table, qh, gs, snew, wnew, cmp_pool, sel_cache, win_cache,
                qb, sb_cache, lam_init, dl, ng, qd, dnew, diff_cache):
    nb, n_pages = page_table.shape
    past = n_pages * 128
    half = max(128, -(-(past // SEL_BLOCK + 1) // 128) * 128)
    n_sel = past // SEL_BLOCK + 1
    kern = functools.partial(_sample_mix_kernel, n_pages=n_pages, past=past, half=half,
                             topk=min(SEL_TOPK, n_sel), cb=math.gcd(past, 256))
    per_seq = lambda a: pl.BlockSpec((None,) + a.shape[1:], lambda b, pt: (b,) + (0,) * (a.ndim - 1))
    const = lambda a: pl.BlockSpec(a.shape, lambda b, pt: (0,) * a.ndim)
    in_specs = [per_seq(qh), per_seq(gs), per_seq(snew), per_seq(wnew),
                pl.BlockSpec((None, None) + win_cache.shape[2:], lambda b, pt: (layer, b, 0, 0)),
                per_seq(qb), pl.BlockSpec(memory_space=pltpu.SMEM), const(dl), const(ng),
                per_seq(qd), per_seq(dnew)]
    in_specs += [pl.BlockSpec((None, 4, 128), functools.partial(lambda b, pt, p: (pt[b, p], 0, 0), p=p))
                 for p in range(n_pages)]
    in_specs += [_page_spec((None, None, 128, 128), layer, p) for p in range(n_pages)]
    in_specs += [_page_spec((None, None, 2, 256, 128), layer, p) for p in range(n_pages)]
    in_specs += [_page_spec((None, None, 2, 256, 128), layer, p) for p in range(n_pages)]
    row_out = pl.BlockSpec((None, 1, 256), lambda b, pt: (b, 0, 0))
    grid_spec = pltpu.PrefetchScalarGridSpec(
        num_scalar_prefetch=1, grid=(nb,), in_specs=in_specs,
        out_specs=[pl.BlockSpec((None, 8, 128), lambda b, pt: (b, 0, 0)), row_out, row_out],
        scratch_shapes=[pltpu.VMEM((2 * half, 128), F32)])
    return pl.pallas_call(
        kern, grid_spec=grid_spec,
        out_shape=[jax.ShapeDtypeStruct((nb, 8, 128), F32), jax.ShapeDtypeStruct((nb, 1, 256), BF16),
                   jax.ShapeDtypeStruct((nb, 1, 256), BF16)],
        compiler_params=_params(("arbitrary",)), name="sample_mix",
    )(page_table, qh, gs, snew, wnew, win_cache, qb, lam_init, dl, ng, qd, dnew,
      *([cmp_pool] * n_pages), *([sel_cache] * n_pages), *([sb_cache] * n_pages), *([diff_cache] * n_pages))


def _tail_kernel(x_ref, oa_ref, ob_ref, oc_ref, od_ref, wo_ref, g1_ref, b1_ref,
                 wg_ref, wu_ref, wd_ref, g2_ref, b2_ref, y_ref, *, alpha):
    mix = (_dot(oa_ref[...], wo_ref[0:256, :]) + _dot(ob_ref[...], wo_ref[256:512, :])
           + _dot(oc_ref[...], wo_ref[512:768, :]) + _dot(od_ref[...], wo_ref[768:1024, :]))
    x1 = _layer_norm(alpha * x_ref[...] + mix, g1_ref[...], b1_ref[...])
    x1b = x1.astype(BF16)
    hg = _dot(x1b, wg_ref[...])
    hu = _dot(x1b, wu_ref[...])
    act = (hg * _sigmoid(hg) * hu).astype(BF16)
    y_ref[...] = _layer_norm(alpha * x1 + _dot(act, wd_ref[...]), g2_ref[...], b2_ref[...])


def _tail(x, oa, ob, oc, od, wo, g1, b1, wg, wu, wd, g2, b2, alpha):
    m = x.shape[0]
    tm = math.gcd(m, TOKEN_TILE)
    row = lambda n: pl.BlockSpec((tm, n), lambda i: (i, 0))
    consts = (wo, g1, b1, wg, wu, wd, g2, b2)
    return pl.pallas_call(
        functools.partial(_tail_kernel, alpha=alpha), grid=(m // tm,),
        in_specs=[row(D_MODEL), row(256), row(256), row(256), row(256)] + [_const_spec(c.shape) for c in consts],
        out_specs=row(D_MODEL), out_shape=jax.ShapeDtypeStruct((m, D_MODEL), F32),
        compiler_params=_params(("parallel",)), name="tail",
    )(x, oa, ob, oc, od, *consts)


def _pad_heads(w):
    k = w.shape[0]
    wr = w.reshape(k, -1, HEAD_DIM)
    return jnp.concatenate([wr, jnp.zeros_like(wr)], axis=-1).reshape(k, -1)


def _layer_weights(w):
    off = [0]
    for n in (256, 64, 64, 64, 64, 64, 64, 12, 256, 256, 256, 256, 256, 256, 256, 256):
        off.append(off[-1] + n)
    col = lambda i: w[:, off[i]:off[i + 1]]
    qa, kc, vc, ks, vs, kw, vw, ga, qb, kb, vb, qd, kd, vd, u, v = [col(i) for i in range(16)]
    qa = qa * (HEAD_DIM ** -0.5)
    qb = qb * (HEAD_DIM ** -0.5)
    qd = qd * (DIFF_QK ** -0.5)
    rm = [_pad_heads(qa), qb, qd, u, v,
          jnp.concatenate([ga, jnp.zeros((w.shape[0], 128 - ga.shape[1]), F32)], 1), kc, vc]
    tr = [kc, vc, ks, vs, kw, vw, kb, vb, kd, vd]
    return (jnp.concatenate(rm, 1).astype(BF16), jnp.concatenate(tr, 1).T.astype(BF16))


def _rope_tables(pos):
    pos = pos.astype(F32)[:, None]
    n = pos.shape[0]

    def table(half, plain, reps):
        freq = ROPE_THETA ** (-jnp.arange(half, dtype=F32) / half)
        ang = pos * freq[None, :]
        c = jnp.concatenate([jnp.cos(ang), jnp.cos(ang), jnp.ones((n, plain), F32)], 1)
        s = jnp.concatenate([jnp.sin(ang), jnp.sin(ang), jnp.zeros((n, plain), F32)], 1)
        return jnp.tile(c, (1, reps)), jnp.tile(s, (1, reps))

    ca, sa = table(ROT_HALF, 128 - 2 * ROT_HALF, 1)
    cd, sd = table(DIFF_ROT_HALF, DIFF_QK - 2 * DIFF_ROT_HALF, 4)
    return (ca, sa, cd, sd, ca.T, sa.T, cd.T, sd.T)


def _compress_weights(pe, w1, w2):
    z = jnp.zeros((CMP_BLOCK, HEAD_DIM, HEAD_DIM), F32)
    w1k = w1[0].reshape(CMP_BLOCK, HEAD_DIM, HEAD_DIM)
    w1v = w1[1].reshape(CMP_BLOCK, HEAD_DIM, HEAD_DIM)
    w1c = jnp.concatenate([jnp.concatenate([w1k, z], 2), jnp.concatenate([z, w1v], 2)], 1)
    z2 = jnp.zeros((HEAD_DIM, HEAD_DIM), F32)
    w2c = jnp.concatenate([jnp.concatenate([w2[0], z2], 1), jnp.concatenate([z2, w2[1]], 1)], 0)
    pe_rows = jnp.concatenate([pe[0], pe[1]], 1)
    return pe_rows, w1c.astype(BF16), w2c.astype(BF16)


def _feature_major(x, tail_shape):
    lead = x.shape[:-2]
    n = x.shape[-1]
    y = x.reshape(lead + tail_shape + (n,))
    nd = y.ndim
    return jnp.transpose(y, tuple(range(len(lead))) + (nd - 1,) + tuple(range(len(lead), nd - 1)))


def kernel(x_prompt, x_sample, cache_nsa_cmp, cache_nsa_sel, cache_nsa_win, cache_sb, cache_diff,
           page_table, w_in, gate_b, cmp_pe, cmp_w1, cmp_w2, diff_lam, diff_norm_g, gmlp_ln_g,
           gmlp_ln_b, gmlp_ws, gmlp_bs, w_out, ln1_g, ln1_b, w_gate, w_up, w_down, ln2_g, ln2_b):
    B, S, _ = x_prompt.shape
    nb, T, _ = x_sample.shape
    depth = w_in.shape[0]
    n_pool = cache_sb.shape[1]
    n_pages = page_table.shape[1]
    past = n_pages * cache_sb.shape[2]
    w_past = cache_nsa_win.shape[2]
    assert T == 1 and cache_sb.shape[2] == 128 and S % Q_TILE == 0 and past % SEL_BLOCK == 0
    alpha = (2 * depth) ** 0.25

    to_pages = lambda c: jnp.transpose(c, (0, 1, 3, 4, 5, 2))
    cmp_cache = to_pages(cache_nsa_cmp).reshape(depth, n_pool, 128, 128)
    sel_cache = to_pages(cache_nsa_sel).reshape(depth, n_pool, 128, 128)
    win_cache = to_pages(cache_nsa_win).reshape(depth, nb, 128, w_past)
    sb_cache = to_pages(cache_sb).reshape(depth, n_pool, 2, 256, 128)
    diff_cache = to_pages(cache_diff).reshape(depth, n_pool, 2, 256, 128)

    tabs_p = _rope_tables(jnp.arange(S, dtype=jnp.int32))
    tabs_s = _rope_tables(jnp.full((nb,), past, jnp.int32))
    row = lambda a: a.reshape(1, -1)

    xp, xs = x_prompt, x_sample.reshape(nb, D_MODEL)
    outs = [[] for _ in range(11)]
    for l in range(depth):
        lam_init = jnp.full((1,), 0.8 - 0.6 * math.exp(-0.3 * l), F32)
        wrm, wt = _layer_weights(w_in[l])
        gb = jnp.concatenate([gate_b[l], jnp.zeros((128 - gate_b.shape[1],), F32)]).reshape(1, 128)
        lng, lnb = row(gmlp_ln_g[l]), row(gmlp_ln_b[l])
        bsm = jnp.repeat(gmlp_bs[l].T, HEAD_DIM, axis=1)
        w00 = row(jnp.repeat(gmlp_ws[l][:, 0, 0], HEAD_DIM))
        b0 = row(jnp.repeat(gmlp_bs[l][:, 0], HEAD_DIM))
        pe_rows, w1c, w2c = _compress_weights(cmp_pe[l], cmp_w1[l], cmp_w2[l])
        ng = row(jnp.tile(diff_norm_g[l], N_HEADS))
        tail_w = (w_out[l].astype(BF16), row(ln1_g[l]), row(ln1_b[l]), w_gate[l].astype(BF16),
                  w_up[l].astype(BF16), w_down[l].astype(BF16), row(ln2_g[l]), row(ln2_b[l]))

        (qaf, qb, qd, gates, od, cmp_rm, cmp_t, sel_t, win_t, sb_t, diff_t,
         selbf, winbf, sbbf, diffbf) = _proj_prompt(xp, wrm, wt, tabs_p, gb, lng, lnb, gmlp_ws[l], bsm)
        kvc = _compress_prompt(cmp_rm, pe_rows, w1c, w2c)
        oa = _nsa_prompt(qaf, gates, kvc, selbf, winbf)
        ob = _sb_prompt(qb, sbbf)
        oc = _diff_prompt(lam_init, diff_lam[l], ng, qd, diffbf)
        flat = lambda a: a.reshape(B * S, a.shape[-1])
        xp = _tail(flat(xp), flat(oa), flat(ob), flat(oc), flat(od), *tail_w, alpha).reshape(B, S, D_MODEL)
        outs[0].append(_feature_major(cmp_t, (2, 1, HEAD_DIM)))
        outs[2].append(_feature_major(sel_t, (2, 1, HEAD_DIM)))
        outs[4].append(_feature_major(win_t[:, :, S - min(WINDOW, S):], (2, 1, HEAD_DIM)))
        outs[6].append(_feature_major(sb_t, (2, N_HEADS, HEAD_DIM)))
        outs[8].append(_feature_major(diff_t, (2, N_HEADS, HEAD_DIM)))

        (qaf, qb, qd, gates, od, vn, sel_rm, win_rm, diff_rm,
         cmp_t, sel_t, win_t, sb_t, diff_t) = _proj_sample(xs, wrm, wt, tabs_s, gb, lng, lnb, w00, b0)
        cmp_pool = _compress_pool(l, cmp_cache, pe_rows, w1c, w2c).reshape(n_pool, 4, 128)[:, jnp.array([0, 2, 1, 3])]
        qh = jnp.pad(qaf.reshape(nb, N_HEADS, 128), ((0, 0), (0, 8 - N_HEADS), (0, 0)))
        gs = jnp.pad(gates[:, :3 * N_HEADS].reshape(nb, N_HEADS, 3), ((0, 0), (0, 8 - N_HEADS), (0, 125)))
        oa, ob, oc = _sample_mix(
            l, page_table, qh, gs, sel_rm.reshape(nb, 1, 128), win_rm.reshape(nb, 1, 128), cmp_pool, sel_cache,
            win_cache, qb.reshape(nb, 1, 256), sb_cache, lam_init, diff_lam[l], ng, qd.reshape(nb, 1, 256),
            diff_rm.reshape(nb, 1, 512), diff_cache)
        oa = oa[:, :N_HEADS, HEAD_DIM:].reshape(nb, GROUP_WIDTH).astype(BF16)
        ob, oc = ob.reshape(nb, 256), oc.reshape(nb, 256)
        xs = _tail(xs, oa, ob, oc, od, *tail_w, alpha)
        new_row = lambda t, shape: _feature_major(t, shape)[:, None]
        outs[1].append(new_row(cmp_t, (2, 1, HEAD_DIM)))
        outs[3].append(new_row(sel_t, (2, 1, HEAD_DIM)))
        outs[5].append(new_row(win_t, (2, 1, HEAD_DIM)))
        outs[7].append(new_row(sb_t, (2, N_HEADS, HEAD_DIM)))
        outs[9].append(new_row(diff_t, (2, N_HEADS, HEAD_DIM)))
        outs[10].append(vn.reshape(nb, 1, GROUP_WIDTH))

    stacked = [jnp.stack(o) for o in outs]
    stacked[5] = jnp.concatenate([cache_nsa_win[:, :, T:], stacked[5]], axis=2)
    return (xp, xs.reshape(nb, 1, D_MODEL), *stacked)
```
